```python
import math
import jax
import jax.numpy as jnp
from jax import lax

D_MODEL = 1024
BATCH = 16
SEQ = 2048
DEPTH = 2
DEC_BATCH = 32
DEC_SEQ = 64
PAST_LEN = 2048

CHUNK = 64
PREV_CHUNKS = 8
BAND_PAST = PREV_CHUNKS * CHUNK
D_A = D_MODEL // 4
D_B = D_MODEL // 2
D_C = D_MODEL // 4
A_HEADS = 4
A_HEAD_DIM = D_A // A_HEADS
CONV_W = 4
RG_C = 8.0
B_HEADS = 8
B_HEAD_DIM = D_B // B_HEADS
REL_CLIP = 128
N_REL = 2 * REL_CLIP + 1
C_GW = 16
C_GROUPS = D_C // C_GW
C_STATE = 64
OFF_GA = D_A
OFF_Q = 2 * D_A
OFF_K = OFF_Q + D_B
OFF_V = OFF_K + D_B
OFF_C = OFF_V + D_B
IN_COLS = OFF_C + D_C
N_EXPERTS = 32
TOP_K = 4
D_FF = D_MODEL
SWIGLU_LIMIT = 7.0
SWIGLU_ALPHA = 1.702
MOE_BLOCK = 256
DEEPNORM_ALPHA = (2 * DEPTH) ** 0.25
DEEPNORM_BETA = (8 * DEPTH) ** -0.25
LN_EPS = 1e-5
RMS_EPS = 1e-6
NEG_INF = -1e30

kernel_name = 'hybrid_streaming_encoder_step'


def layer_norm(x, g, b):
    xf = x.astype(jnp.float32)
    mu = jnp.mean(xf, -1, keepdims=True)
    var = jnp.mean(jnp.square(xf - mu), -1, keepdims=True)
    return ((xf - mu) * lax.rsqrt(var + LN_EPS) * g.astype(jnp.float32) + b.astype(jnp.float32)).astype(x.dtype)


def rms_norm(x, g):
    xf = x.astype(jnp.float32)
    return (xf * lax.rsqrt(jnp.mean(jnp.square(xf), -1, keepdims=True) + RMS_EPS) * g.astype(jnp.float32)).astype(x.dtype)


def linear_scan(a, b):
    def combine(left, right):
        a1, b1 = left
        a2, b2 = right
        return a1 * a2, a2 * b1 + b2
    return lax.associative_scan(combine, (a, b), axis=1)[1]


def rglru_mixer(xa, ga, conv_buf, h0, conv_w, conv_b, w_r, b_r, w_i, b_i, lam):
    bn, seq, _ = xa.shape
    xp = jnp.concatenate([conv_buf.astype(xa.dtype), xa], axis=1)
    xc = conv_b + sum(xp[:, j:j + seq] * conv_w[j] for j in range(CONV_W))
    new_buf = xp[:, xp.shape[1] - (CONV_W - 1):]
    xh = xc.reshape(bn, seq, A_HEADS, A_HEAD_DIM)
    r = jax.nn.sigmoid(jnp.einsum('blhi,hij->blhj', xh, w_r).reshape(bn, seq, D_A) + b_r).astype(jnp.float32)
    i = jax.nn.sigmoid(jnp.einsum('blhi,hij->blhj', xh, w_i).reshape(bn, seq, D_A) + b_i).astype(jnp.float32)
    log_a = -RG_C * r * jax.nn.softplus(-lam.astype(jnp.float32))
    a = jnp.exp(log_a)
    b = jnp.sqrt(-jnp.expm1(2.0 * log_a)) * (i * xc.astype(jnp.float32))
    b = b.at[:, 0].add(a[:, 0] * h0.astype(jnp.float32))
    h = linear_scan(a, b)
    y = h.astype(xa.dtype) * jax.nn.gelu(ga)
    return y, new_buf, h[:, -1]


def attend(q, k, v, rel, rel_bias, valid=None):
    scores = jnp.einsum('bqhd,bkhd->bhqk', q, k).astype(jnp.float32) * (B_HEAD_DIM ** -0.5)
    idx = jnp.clip(rel, -REL_CLIP, REL_CLIP) + REL_CLIP
    scores = scores + rel_bias.astype(jnp.float32)[:, idx][None]
    if valid is not None:
        scores = jnp.where(valid[None, None], scores, NEG_INF)
    p = jax.nn.softmax(scores, axis=-1)
    return jnp.einsum('bhqk,bkhd->bqhd', p.astype(v.dtype), v)


def chunk_band_attention_prompt(q, k, v, rel_bias):
    bn, seq = q.shape[:2]
    n_chunks = seq // CHUNK
    band = BAND_PAST + CHUNK
    pad = ((0, 0), (BAND_PAST, 0), (0, 0), (0, 0))
    kp = jnp.pad(k, pad)
    vp = jnp.pad(v, pad)
    qc = jnp.moveaxis(q.reshape(bn, n_chunks, CHUNK, B_HEADS, B_HEAD_DIM), 1, 0)
    rel = jnp.arange(CHUNK)[:, None] + BAND_PAST - jnp.arange(band)[None, :]

    def one_chunk(args):
        c, qb = args
        start = c * CHUNK
        kb = lax.dynamic_slice_in_dim(kp, start, band, axis=1)
        vb = lax.dynamic_slice_in_dim(vp, start, band, axis=1)
        valid = jnp.broadcast_to((start - BAND_PAST + jnp.arange(band)) >= 0, (CHUNK, band))
        return attend(qb, kb, vb, rel, rel_bias, valid)

    out = lax.map(one_chunk, (jnp.arange(n_chunks), qc))
    return jnp.moveaxis(out, 0, 1).reshape(bn, seq, D_B)


def chunk_band_attention_sample(q, k_new, v_new, k_cache, v_cache, rel_bias):
    bn, s = q.shape[:2]
    w = k_cache.shape[1]
    k = jnp.concatenate([k_cache.astype(k_new.dtype), k_new], axis=1)
    v = jnp.concatenate([v_cache.astype(v_new.dtype), v_new], axis=1)
    rel = jnp.arange(s)[:, None] + w - jnp.arange(w + s)[None, :]
    return attend(q, k, v, rel, rel_bias).reshape(bn, s, D_B)


def s5_mixer(uc, s0_re, s0_im, a_re, a_im, log_dt, b_re, b_im, c_re, c_im, d_skip, w_glu, b_glu):
    bn, seq, _ = uc.shape
    f32 = jnp.float32
    u = uc.astype(f32).reshape(bn, seq, C_GROUPS, C_GW)
    lam = lax.complex(a_re.astype(f32), a_im.astype(f32))
    dt = jnp.exp(log_dt.astype(f32))[:, None]
    a_bar = jnp.exp(lam * dt)
    b_bar = ((a_bar - 1.0) / lam)[:, :, None] * lax.complex(b_re.astype(f32), b_im.astype(f32))
    bu = jnp.einsum('blgi,gpi->blgp', u.astype(jnp.complex64), b_bar)
    s0 = lax.complex(s0_re.astype(f32), s0_im.astype(f32))
    bu = bu.at[:, 0].add(a_bar * s0)
    s = linear_scan(jnp.broadcast_to(a_bar, bu.shape), bu)
    c_mat = lax.complex(c_re.astype(f32), c_im.astype(f32))
    y = jnp.real(jnp.einsum('blgp,gip->blgi', s, c_mat)) + d_skip.astype(f32).reshape(C_GROUPS, C_GW) * u
    y = jax.nn.gelu(y.reshape(bn, seq, D_C))
    y = y * jax.nn.sigmoid(y @ w_glu.astype(f32) + b_glu.astype(f32))
    s_last = s[:, -1]
    return y.astype(uc.dtype), jnp.real(s_last), jnp.imag(s_last)


def mixer_sublayer(x, conv_buf, h0, s0_re, s0_im, k_cache, v_cache,
                   w_in, conv_w, conv_b, w_r, b_r, w_i, b_i, lam, rel_bias,
                   a_re, a_im, log_dt, b_re, b_im, c_re, c_im, d_skip, w_glu, b_glu,
                   g_a, g_b, g_c, w_out):
    bn, seq, _ = x.shape
    u = x @ w_in
    xa = u[..., :OFF_GA]
    ga = u[..., OFF_GA:OFF_Q]
    q = u[..., OFF_Q:OFF_K].reshape(bn, seq, B_HEADS, B_HEAD_DIM)
    k = u[..., OFF_K:OFF_V].reshape(bn, seq, B_HEADS, B_HEAD_DIM)
    v = u[..., OFF_V:OFF_C].reshape(bn, seq, B_HEADS, B_HEAD_DIM)
    uc = u[..., OFF_C:]
    y_a, new_buf, h_last = rglru_mixer(xa, ga, conv_buf, h0, conv_w, conv_b, w_r, b_r, w_i, b_i, lam)
    if k_cache is None:
        y_b = chunk_band_attention_prompt(q, k, v, rel_bias)
        keep = min(BAND_PAST, seq)
        k_rows = k[:, seq - keep:]
        v_rows = v[:, seq - keep:]
    else:
        y_b = chunk_band_attention_sample(q, k, v, k_cache, v_cache, rel_bias)
        k_rows = k
        v_rows = v
    y_c, s_re, s_im = s5_mixer(uc, s0_re, s0_im, a_re, a_im, log_dt, b_re, b_im, c_re, c_im, d_skip, w_glu, b_glu)
    merged = jnp.concatenate([rms_norm(y_a, g_a), rms_norm(y_b, g_b), rms_norm(y_c, g_c)], axis=-1)
    return merged @ w_out, (new_buf, h_last, k_rows, v_rows, s_re, s_im)


def moe_sublayer(x, w_router, b_router, w_gu, b_gu, w_dn, b_dn):
    bn, seq, d = x.shape
    x2 = x.reshape(bn * seq, d)
    n_tok = x2.shape[0]
    n_slots = n_tok * TOP_K
    logits = (x2 @ w_router).astype(jnp.float32) + b_router.astype(jnp.float32)
    top_v, top_i = lax.top_k(logits, TOP_K)
    gates = jax.nn.softmax(top_v, axis=-1)
    flat_e = top_i.reshape(-1)
    order = jnp.argsort(flat_e)
    sorted_e = flat_e[order]
    tok = order // TOP_K
    sorted_g = gates.reshape(-1)[order]
    counts = jnp.bincount(flat_e, length=N_EXPERTS)
    padded = ((counts + MOE_BLOCK - 1) // MOE_BLOCK) * MOE_BLOCK
    pad_end = jnp.cumsum(padded)
    pad_start = pad_end - padded
    start = jnp.cumsum(counts) - counts
    dest = pad_start[sorted_e] + (jnp.arange(n_slots) - start[sorted_e])
    n_blocks = -(-n_slots // MOE_BLOCK) + N_EXPERTS
    n_rows = n_blocks * MOE_BLOCK
    row_tok = jnp.full((n_rows,), n_tok, jnp.int32).at[dest].set(tok.astype(jnp.int32))
    row_gate = jnp.zeros((n_rows,), jnp.float32).at[dest].set(sorted_g)
    blk_e = jnp.minimum(jnp.searchsorted(pad_end, jnp.arange(n_blocks) * MOE_BLOCK, side='right'), N_EXPERTS - 1)
    x_pad = jnp.concatenate([x2, jnp.zeros((1, d), x2.dtype)], axis=0)
    xb = x_pad[row_tok].reshape(n_blocks, MOE_BLOCK, d)

    def expert(args):
        xe, e = args
        h = xe @ w_gu[e] + b_gu[e]
        gate = jnp.minimum(h[..., :D_FF], SWIGLU_LIMIT)
        up = jnp.clip(h[..., D_FF:], -SWIGLU_LIMIT, SWIGLU_LIMIT)
        glu = gate * jax.nn.sigmoid(gate * SWIGLU_ALPHA)
        return ((up + 1.0) * glu) @ w_dn[e] + b_dn[e]

    yb = lax.map(expert, (xb, blk_e)).reshape(n_rows, d)
    out = jnp.zeros((n_tok + 1, d), jnp.float32).at[row_tok].add(yb.astype(jnp.float32) * row_gate[:, None])
    return out[:n_tok].astype(x.dtype).reshape(bn, seq, d)


def setup_inputs(seed: int = 0) -> dict:
    key = jax.random.key(seed)
    ks = iter(jax.random.split(key, 64))

    def nrm(shape, scale=1.0):
        return scale * jax.random.normal(next(ks), shape, jnp.float32)

    kv_rows = min(BAND_PAST, PAST_LEN)
    u_lam = jax.random.uniform(next(ks), (DEPTH, D_A), jnp.float32, 0.9, 0.999)
    a_base = u_lam ** (1.0 / RG_C)
    lambda_a = jnp.log(a_base) - jnp.log1p(-a_base)
    log_dt_c = jax.random.uniform(next(ks), (DEPTH, C_GROUPS), jnp.float32, math.log(1e-3), math.log(1e-1))
    a_im_c = jnp.broadcast_to(math.pi * jnp.arange(C_STATE, dtype=jnp.float32), (DEPTH, C_GROUPS, C_STATE)) + nrm((DEPTH, C_GROUPS, C_STATE), 0.01)
    return {
        'x_prompt': nrm((BATCH, SEQ, D_MODEL)),
        'x_sample': nrm((DEC_BATCH, DEC_SEQ, D_MODEL)),
        'cache_conv_a': nrm((DEPTH, DEC_BATCH, CONV_W - 1, D_A)),
        'state_h_a': nrm((DEPTH, DEC_BATCH, D_A), 0.5),
        'cache_k_b': nrm((DEPTH, DEC_BATCH, kv_rows, B_HEADS, B_HEAD_DIM)),
        'cache_v_b': nrm((DEPTH, DEC_BATCH, kv_rows, B_HEADS, B_HEAD_DIM)),
        'state_s_re_c': nrm((DEPTH, DEC_BATCH, C_GROUPS, C_STATE), 0.1),
        'state_s_im_c': nrm((DEPTH, DEC_BATCH, C_GROUPS, C_STATE), 0.1),
        'w_in': nrm((DEPTH, D_MODEL, IN_COLS), D_MODEL ** -0.5),
        'conv_w_a': nrm((DEPTH, CONV_W, D_A), CONV_W ** -0.5),
        'conv_b_a': nrm((DEPTH, D_A), 0.01),
        'w_r_a': nrm((DEPTH, A_HEADS, A_HEAD_DIM, A_HEAD_DIM), A_HEAD_DIM ** -0.5),
        'b_r_a': nrm((DEPTH, D_A), 0.01),
        'w_i_a': nrm((DEPTH, A_HEADS, A_HEAD_DIM, A_HEAD_DIM), A_HEAD_DIM ** -0.5),
        'b_i_a': nrm((DEPTH, D_A), 0.01),
        'lambda_a': lambda_a,
        'rel_bias_b': nrm((DEPTH, B_HEADS, N_REL), 0.1),
        'a_re_c': -0.5 + nrm((DEPTH, C_GROUPS, C_STATE), 0.01),
        'a_im_c': a_im_c,
        'log_dt_c': log_dt_c,
        'b_re_c': nrm((DEPTH, C_GROUPS, C_STATE, C_GW), (2 * C_GW) ** -0.5),
        'b_im_c': nrm((DEPTH, C_GROUPS, C_STATE, C_GW), (2 * C_GW) ** -0.5),
        'c_re_c': nrm((DEPTH, C_GROUPS, C_GW, C_STATE), C_STATE ** -0.5),
        'c_im_c': nrm((DEPTH, C_GROUPS, C_GW, C_STATE), C_STATE ** -0.5),
        'd_c': nrm((DEPTH, D_C)),
        'w_glu_c': nrm((DEPTH, D_C, D_C), D_C ** -0.5),
        'b_glu_c': nrm((DEPTH, D_C), 0.01),
        'g_norm_a': 1.0 + nrm((DEPTH, D_A), 0.01),
        'g_norm_b': 1.0 + nrm((DEPTH, D_B), 0.01),
        'g_norm_c': 1.0 + nrm((DEPTH, D_C), 0.01),
        'w_out': nrm((DEPTH, D_MODEL, D_MODEL), DEEPNORM_BETA * D_MODEL ** -0.5),
        'ln1_g': 1.0 + nrm((DEPTH, D_MODEL), 0.01),
        'ln1_b': nrm((DEPTH, D_MODEL), 0.01),
        'w_router': nrm((DEPTH, D_MODEL, N_EXPERTS), D_MODEL ** -0.5),
        'b_router': nrm((DEPTH, N_EXPERTS), 0.01),
        'w_gu': nrm((DEPTH, N_EXPERTS, D_MODEL, 2 * D_FF), D_MODEL ** -0.5),
        'b_gu': nrm((DEPTH, N_EXPERTS, 2 * D_FF), 0.01),
        'w_dn': nrm((DEPTH, N_EXPERTS, D_FF, D_MODEL), DEEPNORM_BETA * D_FF ** -0.5),
        'b_dn': nrm((DEPTH, N_EXPERTS, D_MODEL), 0.01),
        'ln2_g': 1.0 + nrm((DEPTH, D_MODEL), 0.01),
        'ln2_b': nrm((DEPTH, D_MODEL), 0.01),
    }


def reference(x_prompt, x_sample, cache_conv_a, state_h_a, cache_k_b, cache_v_b, state_s_re_c, state_s_im_c,
              w_in, conv_w_a, conv_b_a, w_r_a, b_r_a, w_i_a, b_i_a, lambda_a, rel_bias_b,
              a_re_c, a_im_c, log_dt_c, b_re_c, b_im_c, c_re_c, c_im_c, d_c, w_glu_c, b_glu_c,
              g_norm_a, g_norm_b, g_norm_c, w_out, ln1_g, ln1_b,
              w_router, b_router, w_gu, b_gu, w_dn, b_dn, ln2_g, ln2_b):
    bp = x_prompt.shape[0]
    yp = x_prompt
    ys = x_sample
    p_states = ([], [], [], [], [], [])
    s_states = ([], [], [], [], [], [])
    for l in range(DEPTH):
        mp = (w_in[l], conv_w_a[l], conv_b_a[l], w_r_a[l], b_r_a[l], w_i_a[l], b_i_a[l], lambda_a[l], rel_bias_b[l],
              a_re_c[l], a_im_c[l], log_dt_c[l], b_re_c[l], b_im_c[l], c_re_c[l], c_im_c[l], d_c[l], w_glu_c[l], b_glu_c[l],
              g_norm_a[l], g_norm_b[l], g_norm_c[l], w_out[l])
        ep = (w_router[l], b_router[l], w_gu[l], b_gu[l], w_dn[l], b_dn[l])
        zbuf = jnp.zeros((bp, CONV_W - 1, D_A), yp.dtype)
        zh = jnp.zeros((bp, D_A), jnp.float32)
        zs = jnp.zeros((bp, C_GROUPS, C_STATE), jnp.float32)
        m, st = mixer_sublayer(yp, zbuf, zh, zs, zs, None, None, *mp)
        yp = layer_norm(DEEPNORM_ALPHA * yp + m, ln1_g[l], ln1_b[l])
        yp = layer_norm(DEEPNORM_ALPHA * yp + moe_sublayer(yp, *ep), ln2_g[l], ln2_b[l])
        for lst, s in zip(p_states, st):
            lst.append(s)
        m, st = mixer_sublayer(ys, cache_conv_a[l], state_h_a[l], state_s_re_c[l], state_s_im_c[l],
                               cache_k_b[l], cache_v_b[l], *mp)
        ys = layer_norm(DEEPNORM_ALPHA * ys + m, ln1_g[l], ln1_b[l])
        ys = layer_norm(DEEPNORM_ALPHA * ys + moe_sublayer(ys, *ep), ln2_g[l], ln2_b[l])
        for lst, s in zip(s_states, st):
            lst.append(s)
    p_conv, p_h, p_k, p_v, p_sre, p_sim = [jnp.stack(s) for s in p_states]
    s_conv, s_h, s_k, s_v, s_sre, s_sim = [jnp.stack(s) for s in s_states]
    return (yp, ys, p_conv, p_h, p_k, p_v, p_sre, p_sim, s_conv, s_h, s_k, s_v, s_sre, s_sim)
```

```python
import functools
import math

import jax
import jax.numpy as jnp
import numpy as np
from jax import lax
from jax.experimental import pallas as pl
from jax.experimental.pallas import tpu as pltpu

F32 = jnp.float32
BF16 = jnp.bfloat16

D_MODEL = 1024
DEPTH = 2
CHUNK = 64
PREV_CHUNKS = 8
BAND_PAST = PREV_CHUNKS * CHUNK
D_A = D_MODEL // 4
D_B = D_MODEL // 2
D_C = D_MODEL // 4
A_HEADS = 4
A_HEAD_DIM = D_A // A_HEADS
CONV_W = 4
RG_C = 8.0
B_HEADS = 8
B_HEAD_DIM = D_B // B_HEADS
REL_CLIP = 128
C_GW = 16
C_GROUPS = D_C // C_GW
C_STATE = 64
C_LANES = C_GROUPS * C_STATE
OFF_GA = D_A
OFF_Q = 2 * D_A
OFF_K = OFF_Q + D_B
OFF_V = OFF_K + D_B
OFF_C = OFF_V + D_B
IN_COLS = OFF_C + D_C
N_EXPERTS = 32
TOP_K = 4
D_FF = D_MODEL
SWIGLU_LIMIT = 7.0
SWIGLU_ALPHA = 1.702
DEEPNORM_ALPHA = (2 * DEPTH) ** 0.25
LN_EPS = 1e-5
RMS_EPS = 1e-6
NEG_INF = -1e30

LANE = 128
HEADS_PER_SLAB = LANE // B_HEAD_DIM
VMEM_LIMIT = 56 * 1024 * 1024

ROW_TILE = 512
MOE_ROWS = 256


def _cparams(*sem):
    return pltpu.CompilerParams(dimension_semantics=sem, vmem_limit_bytes=VMEM_LIMIT)


def _full(shape):
    return pl.BlockSpec(shape, lambda *_: (0,) * len(shape))


def _in_proj_kernel(x_ref, w_ref, o_ref):
    o_ref[...] = jnp.dot(x_ref[...].astype(BF16), w_ref[...], preferred_element_type=F32)


def in_proj(x2d, w_bf16):
    n = x2d.shape[0]
    tm = min(ROW_TILE, n)
    return pl.pallas_call(
        _in_proj_kernel,
        grid=(n // tm,),
        in_specs=[pl.BlockSpec((tm, D_MODEL), lambda i: (i, 0)), _full((D_MODEL, IN_COLS))],
        out_specs=pl.BlockSpec((tm, IN_COLS), lambda i: (i, 0)),
        out_shape=jax.ShapeDtypeStruct((n, IN_COLS), F32),
        compiler_params=_cparams("parallel"),
        name="in_proj",
    )(x2d, w_bf16)


def _rglru_kernel(xa_ref, ga_ref, buf_ref, h0_ref, cw_ref, cb_ref, wr_ref, br_ref, wi_ref, bi_ref, lam_ref,
                  y_ref, nbuf_ref, hl_ref, xp_ref, hc_ref, *, t):
    pad = 8
    hist = CONV_W - 1

    @pl.when(pl.program_id(1) == 0)
    def _():
        xp_ref[0:pad, :] = jnp.zeros((pad, D_A), F32)
        xp_ref[pad - hist:pad, :] = buf_ref[0]
        hc_ref[...] = h0_ref[0]

    xa = xa_ref[0]
    xp_ref[pad:pad + t, :] = xa
    xc = cb_ref[...] + xa * cw_ref[hist:hist + 1, :]
    for j in range(hist):
        xc = xc + xp_ref[pad - hist + j:pad - hist + j + t, :] * cw_ref[j:j + 1, :]
    tail = xp_ref[pad + t - hist:pad + t, :]
    nbuf_ref[0] = tail
    xp_ref[pad - hist:pad, :] = tail

    r = jax.nn.sigmoid(jnp.dot(xc, wr_ref[...], preferred_element_type=F32, precision=lax.Precision.HIGHEST) + br_ref[...])
    i = jax.nn.sigmoid(jnp.dot(xc, wi_ref[...], preferred_element_type=F32, precision=lax.Precision.HIGHEST) + bi_ref[...])
    lam = lam_ref[...]
    softplus_neg_lam = jnp.maximum(-lam, 0.0) + jnp.log(1.0 + jnp.exp(-jnp.abs(lam)))
    log_a = (-RG_C) * r * softplus_neg_lam
    a = jnp.exp(log_a)
    b = jnp.sqrt(1.0 - jnp.exp(2.0 * log_a)) * (i * xc)

    row = lax.broadcasted_iota(jnp.int32, (t, D_A), 0)
    b = jnp.where(row == 0, b + a * hc_ref[...], b)
    acc_a, acc_b = a, b
    sh = 1
    while sh < t:
        m = row >= sh
        prev_b = pltpu.roll(acc_b, sh, 0)
        new_b = jnp.where(m, acc_a * prev_b + acc_b, acc_b)
        if sh * 2 < t:
            prev_a = pltpu.roll(acc_a, sh, 0)
            acc_a = jnp.where(m, acc_a * prev_a, acc_a)
        acc_b = new_b
        sh *= 2
    h = acc_b
    h_last = h[t - 1:t, :]
    hc_ref[...] = h_last
    hl_ref[0] = h_last
    y_ref[0] = h * jax.nn.gelu(ga_ref[0])


def rglru(u, conv_buf, h0, cw, cb, wr_bd, br, wi_bd, bi, lam):
    bn, seq, _ = u.shape
    t = min(ROW_TILE, seq)
    vec = _full((1, D_A))
    return pl.pallas_call(
        functools.partial(_rglru_kernel, t=t),
        grid=(bn, seq // t),
        in_specs=[
            pl.BlockSpec((1, t, D_A), lambda b, s: (b, s, 0)),
            pl.BlockSpec((1, t, D_A), lambda b, s: (b, s, OFF_GA // D_A)),
            pl.BlockSpec((1, CONV_W - 1, D_A), lambda b, s: (b, 0, 0)),
            pl.BlockSpec((1, 1, D_A), lambda b, s: (b, 0, 0)),
            _full((CONV_W, D_A)), vec, _full((D_A, D_A)), vec, _full((D_A, D_A)), vec, vec,
        ],
        out_specs=[
            pl.BlockSpec((1, t, D_A), lambda b, s: (b, s, 0)),
            pl.BlockSpec((1, CONV_W - 1, D_A), lambda b, s: (b, 0, 0)),
            pl.BlockSpec((1, 1, D_A), lambda b, s: (b, 0, 0)),
        ],
        out_shape=[
            jax.ShapeDtypeStruct((bn, seq, D_A), F32),
            jax.ShapeDtypeStruct((bn, CONV_W - 1, D_A), F32),
            jax.ShapeDtypeStruct((bn, 1, D_A), F32),
        ],
        scratch_shapes=[pltpu.VMEM((t + 8, D_A), F32), pltpu.VMEM((1, D_A), F32)],
        compiler_params=_cparams("parallel", "arbitrary"),
        name="rglru",
    )(u, u, conv_buf, h0.reshape(bn, 1, D_A), cw, cb.reshape(1, D_A), wr_bd, br.reshape(1, D_A),
      wi_bd, bi.reshape(1, D_A), lam.reshape(1, D_A))


def _attn_kernel(q_ref, kp_ref, kc_ref, vp_ref, vc_ref, bias_ref, o_ref, *, pb, mask_first_prev):
    scale = B_HEAD_DIM ** -0.5
    q = q_ref[0].astype(BF16)
    kp = kp_ref[0].astype(BF16)
    kc = kc_ref[0].astype(BF16)
    vp = vp_ref[0].astype(BF16)
    vc = vc_ref[0].astype(BF16)
    lane = lax.broadcasted_iota(jnp.int32, (1, LANE), 1)
    contract_last = (((1,), (1,)), ((), ()))
    out = None
    for hh in range(HEADS_PER_SLAB):
        in_head = (lane // B_HEAD_DIM) == hh
        qh = jnp.where(in_head, q, jnp.zeros_like(q))
        sp = lax.dot_general(qh, kp, contract_last, preferred_element_type=F32) * scale + bias_ref[hh, :, 0:pb]
        sc = lax.dot_general(qh, kc, contract_last, preferred_element_type=F32) * scale + bias_ref[hh, :, pb:]
        if mask_first_prev:
            sp = jnp.where(pl.program_id(2) == 0, NEG_INF, sp)
        m = jnp.maximum(jnp.max(sp, axis=-1, keepdims=True), jnp.max(sc, axis=-1, keepdims=True))
        ep = jnp.exp(sp - m)
        ec = jnp.exp(sc - m)
        denom = jnp.sum(ep, axis=-1, keepdims=True) + jnp.sum(ec, axis=-1, keepdims=True)
        o = (jnp.dot(ep.astype(BF16), vp, preferred_element_type=F32)
             + jnp.dot(ec.astype(BF16), vc, preferred_element_type=F32)) / denom
        out = o if out is None else jnp.where(in_head, o, out)
    o_ref[0] = out


def attention(q_arr, q_col, kprev_arr, kprev_col, vprev_arr, vprev_col, kv_arr, k_col, v_col, bias, *, qb, pb,
              prev_is_same_array):
    bn, seq, _ = q_arr.shape
    n_slabs = B_HEADS // HEADS_PER_SLAB
    if prev_is_same_array:
        prev_map = lambda col: (lambda hp, b, s: (b, jnp.maximum(s - 1, 0), col + hp))
    else:
        prev_map = lambda col: (lambda hp, b, s: (b, 0, col + hp))
    cur_map = lambda col: (lambda hp, b, s: (b, s, col + hp))
    return pl.pallas_call(
        functools.partial(_attn_kernel, pb=pb, mask_first_prev=prev_is_same_array),
        grid=(n_slabs, bn, seq // qb),
        in_specs=[
            pl.BlockSpec((1, qb, LANE), cur_map(q_col)),
            pl.BlockSpec((1, pb, LANE), prev_map(kprev_col)),
            pl.BlockSpec((1, qb, LANE), cur_map(k_col)),
            pl.BlockSpec((1, pb, LANE), prev_map(vprev_col)),
            pl.BlockSpec((1, qb, LANE), cur_map(v_col)),
            pl.BlockSpec((HEADS_PER_SLAB, qb, pb + qb), lambda hp, b, s: (hp, 0, 0)),
        ],
        out_specs=pl.BlockSpec((1, qb, LANE), lambda hp, b, s: (b, s, hp)),
        out_shape=jax.ShapeDtypeStruct((bn, seq, D_B), F32),
        compiler_params=_cparams("arbitrary", "arbitrary", "arbitrary"),
        name="attention",
    )(q_arr, kprev_arr, kv_arr, vprev_arr, kv_arr, bias)


def _rel_index(n_q, n_k):
    i = np.arange(n_q)[:, None]
    j = np.arange(n_k)[None, :]
    return i, j, np.clip(i + BAND_PAST - j, -REL_CLIP, REL_CLIP) + REL_CLIP


def prompt_bias(rel_bias, qb):
    i, j, idx = _rel_index(qb, BAND_PAST + qb)
    dc = i // CHUNK - j // CHUNK + PREV_CHUNKS
    valid = (dc >= 0) & (dc <= PREV_CHUNKS)
    return jnp.where(valid[None], rel_bias.astype(F32)[:, idx], NEG_INF)


def sample_bias(rel_bias, s, w):
    i = np.arange(s)[:, None]
    j = np.arange(w + s)[None, :]
    idx = np.clip(i + w - j, -REL_CLIP, REL_CLIP) + REL_CLIP
    return rel_bias.astype(F32)[:, idx]


def _s5_kernel(u_ref, s0_ref, bmat_ref, pr_ref, pi_ref, cmat_ref, d_ref, wg_ref, bg_ref,
               y_ref, sl_ref, sc_ref, *, t):
    @pl.when(pl.program_id(1) == 0)
    def _():
        sc_ref[...] = s0_ref[0]

    def cmul(level, x):
        return pr_ref[level:level + 1, :] * x + pi_ref[level:level + 1, :] * pltpu.roll(x, C_LANES, 1)

    u = u_ref[0]
    x = jnp.dot(u.astype(BF16), bmat_ref[...], preferred_element_type=F32)
    row = lax.broadcasted_iota(jnp.int32, (t, 2 * C_LANES), 0)
    x = jnp.where(row == 0, x + cmul(0, sc_ref[...]), x)
    sh, level = 1, 0
    while sh < t:
        x = x + jnp.where(row >= sh, cmul(level, pltpu.roll(x, sh, 0)), 0.0)
        sh *= 2
        level += 1
    s_last = x[t - 1:t, :]
    sc_ref[...] = s_last
    sl_ref[0] = s_last
    y = jnp.dot(x.astype(BF16), cmat_ref[...], preferred_element_type=F32) + d_ref[...] * u
    y = jax.nn.gelu(y)
    gate = jax.nn.sigmoid(jnp.dot(y.astype(BF16), wg_ref[...], preferred_element_type=F32) + bg_ref[...])
    y_ref[0] = y * gate


S5_TILE = 256


def s5(u, s0, bmat, p_re, p_im, cmat, d_skip, w_glu, b_glu):
    bn, seq, _ = u.shape
    t = min(S5_TILE, seq)
    n_levels = p_re.shape[0]
    return pl.pallas_call(
        functools.partial(_s5_kernel, t=t),
        grid=(bn, seq // t),
        in_specs=[
            pl.BlockSpec((1, t, D_C), lambda b, s: (b, s, OFF_C // D_C)),
            pl.BlockSpec((1, 1, 2 * C_LANES), lambda b, s: (b, 0, 0)),
            _full((D_C, 2 * C_LANES)), _full((n_levels, 2 * C_LANES)), _full((n_levels, 2 * C_LANES)),
            _full((2 * C_LANES, D_C)), _full((1, D_C)), _full((D_C, D_C)), _full((1, D_C)),
        ],
        out_specs=[
            pl.BlockSpec((1, t, D_C), lambda b, s: (b, s, 0)),
            pl.BlockSpec((1, 1, 2 * C_LANES), lambda b, s: (b, 0, 0)),
        ],
        out_shape=[
            jax.ShapeDtypeStruct((bn, seq, D_C), F32),
            jax.ShapeDtypeStruct((bn, 1, 2 * C_LANES), F32),
        ],
        scratch_shapes=[pltpu.VMEM((1, 2 * C_LANES), F32)],
        compiler_params=_cparams("parallel", "arbitrary"),
        name="s5",
    )(u, s0, bmat, p_re, p_im, cmat, d_skip.reshape(1, D_C), w_glu, b_glu.reshape(1, D_C))


def s5_params(a_re, a_im, log_dt, b_re, b_im, c_re, c_im, n_levels):
    lam = lax.complex(a_re.astype(F32), a_im.astype(F32))
    dt = jnp.exp(log_dt.astype(F32))[:, None]
    a_bar = jnp.exp(lam * dt)
    b_bar = ((a_bar - 1.0) / lam)[:, :, None] * lax.complex(b_re.astype(F32), b_im.astype(F32))
    eye = jnp.eye(C_GROUPS, dtype=F32)

    def block_in(m):
        return jnp.einsum('gpi,gh->gihp', m, eye).reshape(D_C, C_LANES)

    def block_out(m):
        return jnp.einsum('gip,gh->gphi', m, eye).reshape(C_LANES, D_C)

    bmat = jnp.concatenate([block_in(jnp.real(b_bar)), block_in(jnp.imag(b_bar))], axis=1)
    cmat = jnp.concatenate([block_out(c_re.astype(F32)), -block_out(c_im.astype(F32))], axis=0)
    p_re, p_im = [], []
    power = a_bar
    for _ in range(n_levels):
        re = jnp.real(power).reshape(C_LANES)
        im = jnp.imag(power).reshape(C_LANES)
        p_re.append(jnp.concatenate([re, re]))
        p_im.append(jnp.concatenate([-im, im]))
        power = power * power
    return bmat.astype(BF16), jnp.stack(p_re), jnp.stack(p_im), cmat.astype(BF16)


def _rms(x, g):
    return x * lax.rsqrt(jnp.mean(jnp.square(x), axis=-1, keepdims=True) + RMS_EPS) * g


def _layer_norm(x, g, b):
    mu = jnp.mean(x, axis=-1, keepdims=True)
    xc = x - mu
    var = jnp.mean(jnp.square(xc), axis=-1, keepdims=True)
    return xc * lax.rsqrt(var + LN_EPS) * g + b


def _out_proj_kernel(ya_ref, yb_ref, yc_ref, x_ref, ga_ref, gb_ref, gc_ref, w_ref, lg_ref, lb_ref, wr_ref, br_ref,
                     x1_ref, x1h_ref, logit_ref):
    m = jnp.dot(_rms(ya_ref[...], ga_ref[...]).astype(BF16), w_ref[0:D_A, :], preferred_element_type=F32)
    m += jnp.dot(_rms(yb_ref[...], gb_ref[...]).astype(BF16), w_ref[D_A:D_A + D_B, :], preferred_element_type=F32)
    m += jnp.dot(_rms(yc_ref[...], gc_ref[...]).astype(BF16), w_ref[D_A + D_B:, :], preferred_element_type=F32)
    x1 = _layer_norm(DEEPNORM_ALPHA * x_ref[...] + m, lg_ref[...], lb_ref[...])
    x1_ref[...] = x1
    x1h_ref[...] = x1.astype(BF16)
    logit_ref[...] = jnp.dot(x1, wr_ref[...], preferred_element_type=F32,
                             precision=lax.Precision.HIGHEST) + br_ref[...]


def out_proj(ya, yb, yc, x2d, g_a, g_b, g_c, w_out_bf16, ln_g, ln_b, w_router, b_router):
    n = x2d.shape[0]
    tm = min(ROW_TILE, n)
    rows = lambda w: pl.BlockSpec((tm, w), lambda i: (i, 0))
    return pl.pallas_call(
        _out_proj_kernel,
        grid=(n // tm,),
        in_specs=[rows(D_A), rows(D_B), rows(D_C), rows(D_MODEL),
                  _full((1, D_A)), _full((1, D_B)), _full((1, D_C)), _full((D_MODEL, D_MODEL)),
                  _full((1, D_MODEL)), _full((1, D_MODEL)), _full((D_MODEL, N_EXPERTS)), _full((1, N_EXPERTS))],
        out_specs=[rows(D_MODEL), rows(D_MODEL), rows(N_EXPERTS)],
        out_shape=[jax.ShapeDtypeStruct((n, D_MODEL), F32), jax.ShapeDtypeStruct((n, D_MODEL), BF16),
                   jax.ShapeDtypeStruct((n, N_EXPERTS), F32)],
        compiler_params=_cparams("parallel"),
        name="out_proj",
    )(ya, yb, yc, x2d, g_a.reshape(1, D_A), g_b.reshape(1, D_B), g_c.reshape(1, D_C), w_out_bf16,
      ln_g.reshape(1, D_MODEL), ln_b.reshape(1, D_MODEL), w_router, b_router.reshape(1, N_EXPERTS))


def _moe_kernel(blk_e_ref, n_used_ref, x_ref, wgu_ref, bgu_ref, wdn_ref, bdn_ref, o_ref):
    i = pl.program_id(0)

    @pl.when(i < n_used_ref[0])
    def _():
        h = jnp.dot(x_ref[...], wgu_ref[0], preferred_element_type=F32) + bgu_ref[0]
        gate = jnp.minimum(h[:, :D_FF], SWIGLU_LIMIT)
        up = jnp.clip(h[:, D_FF:], -SWIGLU_LIMIT, SWIGLU_LIMIT)
        glu = gate * jax.nn.sigmoid(gate * SWIGLU_ALPHA)
        act = ((up + 1.0) * glu).astype(BF16)
        o_ref[...] = jnp.dot(act, wdn_ref[0], preferred_element_type=F32) + bdn_ref[0]

    @pl.when(i >= n_used_ref[0])
    def _():
        o_ref[...] = jnp.zeros_like(o_ref)


def moe_experts(xb, blk_e, n_used, w_gu, b_gu, w_dn, b_dn):
    n_rows = xb.shape[0]
    n_blocks = n_rows // MOE_ROWS
    grid_spec = pltpu.PrefetchScalarGridSpec(
        num_scalar_prefetch=2,
        grid=(n_blocks,),
        in_specs=[
            pl.BlockSpec((MOE_ROWS, D_MODEL), lambda i, be, nu: (i, 0)),
            pl.BlockSpec((1, D_MODEL, 2 * D_FF), lambda i, be, nu: (be[i], 0, 0)),
            pl.BlockSpec((1, 1, 2 * D_FF), lambda i, be, nu: (be[i], 0, 0)),
            pl.BlockSpec((1, D_FF, D_MODEL), lambda i, be, nu: (be[i], 0, 0)),
            pl.BlockSpec((1, 1, D_MODEL), lambda i, be, nu: (be[i], 0, 0)),
        ],
        out_specs=pl.BlockSpec((MOE_ROWS, D_MODEL), lambda i, be, nu: (i, 0)),
    )
    return pl.pallas_call(
        _moe_kernel,
        grid_spec=grid_spec,
        out_shape=jax.ShapeDtypeStruct((n_rows, D_MODEL), F32),
        compiler_params=_cparams("arbitrary"),
        name="moe_experts",
    )(blk_e, n_used, xb, w_gu, b_gu.reshape(N_EXPERTS, 1, 2 * D_FF), w_dn, b_dn.reshape(N_EXPERTS, 1, D_MODEL))


def route(logits):
    n_tok = logits.shape[0]
    n_slots = n_tok * TOP_K
    top_v, top_i = lax.top_k(logits, TOP_K)
    gates = jax.nn.softmax(top_v, axis=-1)
    flat_e = top_i.reshape(-1)
    onehot = (flat_e[:, None] == jnp.arange(N_EXPERTS)[None, :]).astype(jnp.int32)
    before = jnp.cumsum(onehot, axis=0) - onehot
    pos = jnp.sum(before * onehot, axis=1)
    counts = jnp.sum(onehot, axis=0)
    padded = ((counts + MOE_ROWS - 1) // MOE_ROWS) * MOE_ROWS
    pad_end = jnp.cumsum(padded)
    pad_start = pad_end - padded
    dest = (pad_start[flat_e] + pos).astype(jnp.int32)
    n_blocks = -(-n_slots // MOE_ROWS) + N_EXPERTS
    n_rows = n_blocks * MOE_ROWS
    row_tok = jnp.zeros((n_rows,), jnp.int32).at[dest].set(jnp.arange(n_slots, dtype=jnp.int32) // TOP_K)
    blk_e = jnp.minimum(jnp.searchsorted(pad_end, jnp.arange(n_blocks) * MOE_ROWS, side='right'),
                        N_EXPERTS - 1).astype(jnp.int32)
    n_used = (pad_end[-1:] // MOE_ROWS).astype(jnp.int32)
    return gates, dest.reshape(n_tok, TOP_K), row_tok, blk_e, n_used


def _add_ln_kernel(x_ref, m_ref, g_ref, b_ref, o_ref):
    o_ref[...] = _layer_norm(DEEPNORM_ALPHA * x_ref[...] + m_ref[...], g_ref[...], b_ref[...])


def add_ln(x2d, m2d, ln_g, ln_b):
    n = x2d.shape[0]
    tm = min(ROW_TILE, n)
    rows = pl.BlockSpec((tm, D_MODEL), lambda i: (i, 0))
    return pl.pallas_call(
        _add_ln_kernel,
        grid=(n // tm,),
        in_specs=[rows, rows, _full((1, D_MODEL)), _full((1, D_MODEL))],
        out_specs=rows,
        out_shape=jax.ShapeDtypeStruct((n, D_MODEL), F32),
        compiler_params=_cparams("parallel"),
        name="add_ln",
    )(x2d, m2d, ln_g.reshape(1, D_MODEL), ln_b.reshape(1, D_MODEL))


def _block_diag(w):
    h, d, _ = w.shape
    return jnp.einsum('hij,hg->higj', w.astype(F32), jnp.eye(h, dtype=F32)).reshape(h * d, h * d)


def _layer(x, state, p, attn_bias):
    bn, seq, _ = x.shape
    n = bn * seq
    conv_buf, h0, s0, k_cache, v_cache = state
    x2d = x.reshape(n, D_MODEL)
    u = in_proj(x2d, p['w_in']).reshape(bn, seq, IN_COLS)

    y_a, new_buf, h_last = rglru(u, conv_buf, h0, p['conv_w'], p['conv_b'], p['w_r'], p['b_r'], p['w_i'], p['b_i'],
                                 p['lam'])
    q_col, k_col, v_col = OFF_Q // LANE, OFF_K // LANE, OFF_V // LANE
    if k_cache is None:
        qb = min(BAND_PAST, seq)
        y_b = attention(u, q_col, u, k_col, u, v_col, u, k_col, v_col, attn_bias, qb=qb, pb=qb,
                        prev_is_same_array=True)
    else:
        y_b = attention(u, q_col, k_cache, 0, v_cache, 0, u, k_col, v_col, attn_bias, qb=seq,
                        pb=k_cache.shape[1], prev_is_same_array=False)
    y_c, s_last = s5(u, s0, p['bmat'], p['p_re'], p['p_im'], p['cmat'], p['d_c'], p['w_glu'], p['b_glu'])

    x1, x1h, logits = out_proj(y_a.reshape(n, D_A), y_b.reshape(n, D_B), y_c.reshape(n, D_C), x2d,
                               p['g_a'], p['g_b'], p['g_c'], p['w_out'], p['ln1_g'], p['ln1_b'],
                               p['w_router'], p['b_router'])
    gates, dest, row_tok, blk_e, n_used = route(logits)
    yb = moe_experts(x1h[row_tok], blk_e, n_used, p['w_gu'], p['b_gu'], p['w_dn'], p['b_dn'])
    moe = jnp.sum(yb[dest] * gates[:, :, None], axis=1)
    x2 = add_ln(x1, moe, p['ln2_g'], p['ln2_b']).reshape(bn, seq, D_MODEL)

    k_rows = u[:, :, OFF_K:OFF_V]
    v_rows = u[:, :, OFF_V:OFF_C]
    if k_cache is None:
        keep = min(BAND_PAST, seq)
        k_rows = k_rows[:, seq - keep:]
        v_rows = v_rows[:, seq - keep:]
    k_rows = k_rows.reshape(bn, -1, B_HEADS, B_HEAD_DIM)
    v_rows = v_rows.reshape(bn, -1, B_HEADS, B_HEAD_DIM)
    s_re = s_last[:, 0, :C_LANES].reshape(bn, C_GROUPS, C_STATE)
    s_im = s_last[:, 0, C_LANES:].reshape(bn, C_GROUPS, C_STATE)
    return x2, (new_buf, h_last.reshape(bn, D_A), k_rows, v_rows, s_re, s_im)


def kernel(x_prompt, x_sample, cache_conv_a, state_h_a, cache_k_b, cache_v_b, state_s_re_c, state_s_im_c, w_in, conv_w_a, conv_b_a, w_r_a, b_r_a, w_i_a, b_i_a, lambda_a, rel_bias_b, a_re_c, a_im_c, log_dt_c, b_re_c, b_im_c, c_re_c, c_im_c, d_c, w_glu_c, b_glu_c, g_norm_a, g_norm_b, g_norm_c, w_out, ln1_g, ln1_b, w_router, b_router, w_gu, b_gu, w_dn, b_dn, ln2_g, ln2_b):
    bp, seq_p, _ = x_prompt.shape
    bs, seq_s, _ = x_sample.shape
    kv_rows = cache_k_b.shape[2]
    n_levels = max(1, int(math.log2(min(S5_TILE, max(seq_p, seq_s)))))
    yp, ys = x_prompt, x_sample
    p_states = [[] for _ in range(6)]
    s_states = [[] for _ in range(6)]
    for l in range(DEPTH):
        bmat, p_re, p_im, cmat = s5_params(a_re_c[l], a_im_c[l], log_dt_c[l], b_re_c[l], b_im_c[l], c_re_c[l],
                                           c_im_c[l], n_levels)
        p = dict(
            w_in=w_in[l].astype(BF16), conv_w=conv_w_a[l], conv_b=conv_b_a[l],
            w_r=_block_diag(w_r_a[l]), b_r=b_r_a[l], w_i=_block_diag(w_i_a[l]), b_i=b_i_a[l], lam=lambda_a[l],
            bmat=bmat, p_re=p_re, p_im=p_im, cmat=cmat, d_c=d_c[l], w_glu=w_glu_c[l].astype(BF16), b_glu=b_glu_c[l],
            g_a=g_norm_a[l], g_b=g_norm_b[l], g_c=g_norm_c[l], w_out=w_out[l].astype(BF16),
            ln1_g=ln1_g[l], ln1_b=ln1_b[l], w_router=w_router[l], b_router=b_router[l],
            w_gu=w_gu[l].astype(BF16), b_gu=b_gu[l], w_dn=w_dn[l].astype(BF16), b_dn=b_dn[l],
            ln2_g=ln2_g[l], ln2_b=ln2_b[l],
        )
        zero_state = (jnp.zeros((bp, CONV_W - 1, D_A), F32), jnp.zeros((bp, D_A), F32),
                      jnp.zeros((bp, 1, 2 * C_LANES), F32), None, None)
        yp, st = _layer(yp, zero_state, p, prompt_bias(rel_bias_b[l], min(BAND_PAST, seq_p)))
        for lst, s in zip(p_states, st):
            lst.append(s)
        s0 = jnp.concatenate([state_s_re_c[l].reshape(bs, 1, C_LANES), state_s_im_c[l].reshape(bs, 1, C_LANES)],
                             axis=-1)
        sample_state = (cache_conv_a[l], state_h_a[l], s0,
                        cache_k_b[l].reshape(bs, kv_rows, D_B), cache_v_b[l].reshape(bs, kv_rows, D_B))
        ys, st = _layer(ys, sample_state, p, sample_bias(rel_bias_b[l], seq_s, kv_rows))
        for lst, s in zip(s_states, st):
            lst.append(s)
    return (yp, ys) + tuple(jnp.stack(s) for s in p_states) + tuple(jnp.stack(s) for s in s_states)
```

```python
import functools
import math

import jax
import jax.numpy as jnp
import numpy as np
from jax import lax
from jax.experimental import pallas as pl
from jax.experimental.pallas import tpu as pltpu

F32 = jnp.float32
BF16 = jnp.bfloat16

D_MODEL = 1024
DEPTH = 2
CHUNK = 64
PREV_CHUNKS = 8
BAND_PAST = PREV_CHUNKS * CHUNK
D_A = D_MODEL // 4
D_B = D_MODEL // 2
D_C = D_MODEL // 4
A_HEADS = 4
A_HEAD_DIM = D_A // A_HEADS
CONV_W = 4
RG_C = 8.0
B_HEADS = 8
B_HEAD_DIM = D_B // B_HEADS
REL_CLIP = 128
C_GW = 16
C_GROUPS = D_C // C_GW
C_STATE = 64
C_LANES = C_GROUPS * C_STATE
OFF_GA = D_A
OFF_Q = 2 * D_A
OFF_K = OFF_Q + D_B
OFF_V = OFF_K + D_B
OFF_C = OFF_V + D_B
IN_COLS = OFF_C + D_C
N_EXPERTS = 32
TOP_K = 4
D_FF = D_MODEL
SWIGLU_LIMIT = 7.0
SWIGLU_ALPHA = 1.702
DEEPNORM_ALPHA = (2 * DEPTH) ** 0.25
LN_EPS = 1e-5
RMS_EPS = 1e-6
NEG_INF = -1e30

LANE = 128
HEADS_PER_SLAB = LANE // B_HEAD_DIM
VMEM_LIMIT = 56 * 1024 * 1024

ROW_TILE = 512
MOE_ROWS = 512


def _cparams(*sem):
    return pltpu.CompilerParams(dimension_semantics=sem, vmem_limit_bytes=VMEM_LIMIT)


def _full(shape):
    return pl.BlockSpec(shape, lambda *_: (0,) * len(shape))


def _in_proj_kernel(x_ref, w_ref, o_ref):
    o_ref[...] = jnp.dot(x_ref[...].astype(BF16), w_ref[...], preferred_element_type=F32)


def in_proj(x2d, w_bf16):
    n = x2d.shape[0]
    tm = min(ROW_TILE, n)
    return pl.pallas_call(
        _in_proj_kernel,
        grid=(n // tm,),
        in_specs=[pl.BlockSpec((tm, D_MODEL), lambda i: (i, 0)), _full((D_MODEL, IN_COLS))],
        out_specs=pl.BlockSpec((tm, IN_COLS), lambda i: (i, 0)),
        out_shape=jax.ShapeDtypeStruct((n, IN_COLS), F32),
        compiler_params=_cparams("parallel"),
        name="in_proj",
    )(x2d, w_bf16)


def _rglru_kernel(xa_ref, ga_ref, buf_ref, h0_ref, cw_ref, cb_ref, wr_ref, br_ref, wi_ref, bi_ref, lam_ref,
                  y_ref, nbuf_ref, hl_ref, xp_ref, hc_ref, *, t):
    pad = 8
    hist = CONV_W - 1

    @pl.when(pl.program_id(1) == 0)
    def _():
        xp_ref[0:pad, :] = jnp.zeros((pad, D_A), F32)
        xp_ref[pad - hist:pad, :] = buf_ref[0]
        hc_ref[...] = h0_ref[0]

    xa = xa_ref[0]
    xp_ref[pad:pad + t, :] = xa
    xc = cb_ref[...] + xa * cw_ref[hist:hist + 1, :]
    for j in range(hist):
        xc = xc + xp_ref[pad - hist + j:pad - hist + j + t, :] * cw_ref[j:j + 1, :]
    tail = xp_ref[pad + t - hist:pad + t, :]
    nbuf_ref[0] = tail
    xp_ref[pad - hist:pad, :] = tail

    r = jax.nn.sigmoid(jnp.dot(xc, wr_ref[...], preferred_element_type=F32, precision=lax.Precision.HIGHEST) + br_ref[...])
    i = jax.nn.sigmoid(jnp.dot(xc, wi_ref[...], preferred_element_type=F32, precision=lax.Precision.HIGHEST) + bi_ref[...])
    lam = lam_ref[...]
    softplus_neg_lam = jnp.maximum(-lam, 0.0) + jnp.log(1.0 + jnp.exp(-jnp.abs(lam)))
    log_a = (-RG_C) * r * softplus_neg_lam
    a = jnp.exp(log_a)
    b = jnp.sqrt(1.0 - jnp.exp(2.0 * log_a)) * (i * xc)

    row = lax.broadcasted_iota(jnp.int32, (t, D_A), 0)
    b = jnp.where(row == 0, b + a * hc_ref[...], b)
    acc_a, acc_b = a, b
    sh = 1
    while sh < t:
        m = row >= sh
        prev_b = pltpu.roll(acc_b, sh, 0)
        new_b = jnp.where(m, acc_a * prev_b + acc_b, acc_b)
        if sh * 2 < t:
            prev_a = pltpu.roll(acc_a, sh, 0)
            acc_a = jnp.where(m, acc_a * prev_a, acc_a)
        acc_b = new_b
        sh *= 2
    h = acc_b
    h_last = h[t - 1:t, :]
    hc_ref[...] = h_last
    hl_ref[0] = h_last
    y_ref[0] = h * jax.nn.gelu(ga_ref[0])


def rglru(u, conv_buf, h0, cw, cb, wr_bd, br, wi_bd, bi, lam):
    bn, seq, _ = u.shape
    t = min(ROW_TILE, seq)
    vec = _full((1, D_A))
    return pl.pallas_call(
        functools.partial(_rglru_kernel, t=t),
        grid=(bn, seq // t),
        in_specs=[
            pl.BlockSpec((1, t, D_A), lambda b, s: (b, s, 0)),
            pl.BlockSpec((1, t, D_A), lambda b, s: (b, s, OFF_GA // D_A)),
            pl.BlockSpec((1, CONV_W - 1, D_A), lambda b, s: (b, 0, 0)),
            pl.BlockSpec((1, 1, D_A), lambda b, s: (b, 0, 0)),
            _full((CONV_W, D_A)), vec, _full((D_A, D_A)), vec, _full((D_A, D_A)), vec, vec,
        ],
        out_specs=[
            pl.BlockSpec((1, t, D_A), lambda b, s: (b, s, 0)),
            pl.BlockSpec((1, CONV_W - 1, D_A), lambda b, s: (b, 0, 0)),
            pl.BlockSpec((1, 1, D_A), lambda b, s: (b, 0, 0)),
        ],
        out_shape=[
            jax.ShapeDtypeStruct((bn, seq, D_A), F32),
            jax.ShapeDtypeStruct((bn, CONV_W - 1, D_A), F32),
            jax.ShapeDtypeStruct((bn, 1, D_A), F32),
        ],
        scratch_shapes=[pltpu.VMEM((t + 8, D_A), F32), pltpu.VMEM((1, D_A), F32)],
        compiler_params=_cparams("parallel", "arbitrary"),
        name="rglru",
    )(u, u, conv_buf, h0.reshape(bn, 1, D_A), cw, cb.reshape(1, D_A), wr_bd, br.reshape(1, D_A),
      wi_bd, bi.reshape(1, D_A), lam.reshape(1, D_A))


def _attn_kernel(q_ref, kp_ref, kc_ref, vp_ref, vc_ref, bias_ref, o_ref, *, pb, mask_first_prev):
    scale = B_HEAD_DIM ** -0.5
    q = q_ref[0].astype(BF16)
    kp = kp_ref[0].astype(BF16)
    kc = kc_ref[0].astype(BF16)
    vp = vp_ref[0].astype(BF16)
    vc = vc_ref[0].astype(BF16)
    lane = lax.broadcasted_iota(jnp.int32, (1, LANE), 1)
    contract_last = (((1,), (1,)), ((), ()))
    out = None
    for hh in range(HEADS_PER_SLAB):
        in_head = (lane // B_HEAD_DIM) == hh
        qh = jnp.where(in_head, q, jnp.zeros_like(q))
        sp = lax.dot_general(qh, kp, contract_last, preferred_element_type=F32) * scale + bias_ref[hh, :, 0:pb]
        sc = lax.dot_general(qh, kc, contract_last, preferred_element_type=F32) * scale + bias_ref[hh, :, pb:]
        if mask_first_prev:
            sp = jnp.where(pl.program_id(2) == 0, NEG_INF, sp)
        m = jnp.maximum(jnp.max(sp, axis=-1, keepdims=True), jnp.max(sc, axis=-1, keepdims=True))
        ep = jnp.exp(sp - m)
        ec = jnp.exp(sc - m)
        denom = jnp.sum(ep, axis=-1, keepdims=True) + jnp.sum(ec, axis=-1, keepdims=True)
        o = (jnp.dot(ep.astype(BF16), vp, preferred_element_type=F32)
             + jnp.dot(ec.astype(BF16), vc, preferred_element_type=F32)) / denom
        out = o if out is None else jnp.where(in_head, o, out)
    o_ref[0] = out


def attention(q_arr, q_col, kprev_arr, kprev_col, vprev_arr, vprev_col, kv_arr, k_col, v_col, bias, *, qb, pb,
              prev_is_same_array):
    bn, seq, _ = q_arr.shape
    n_slabs = B_HEADS // HEADS_PER_SLAB
    if prev_is_same_array:
        prev_map = lambda col: (lambda hp, b, s: (b, jnp.maximum(s - 1, 0), col + hp))
    else:
        prev_map = lambda col: (lambda hp, b, s: (b, 0, col + hp))
    cur_map = lambda col: (lambda hp, b, s: (b, s, col + hp))
    return pl.pallas_call(
        functools.partial(_attn_kernel, pb=pb, mask_first_prev=prev_is_same_array),
        grid=(n_slabs, bn, seq // qb),
        in_specs=[
            pl.BlockSpec((1, qb, LANE), cur_map(q_col)),
            pl.BlockSpec((1, pb, LANE), prev_map(kprev_col)),
            pl.BlockSpec((1, qb, LANE), cur_map(k_col)),
            pl.BlockSpec((1, pb, LANE), prev_map(vprev_col)),
            pl.BlockSpec((1, qb, LANE), cur_map(v_col)),
            pl.BlockSpec((HEADS_PER_SLAB, qb, pb + qb), lambda hp, b, s: (hp, 0, 0)),
        ],
        out_specs=pl.BlockSpec((1, qb, LANE), lambda hp, b, s: (b, s, hp)),
        out_shape=jax.ShapeDtypeStruct((bn, seq, D_B), F32),
        compiler_params=_cparams("arbitrary", "arbitrary", "arbitrary"),
        name="attention",
    )(q_arr, kprev_arr, kv_arr, vprev_arr, kv_arr, bias)


def _toeplitz_bias(rel_bias, qb, pb):
    n_heads = rel_bias.shape[0]
    period = pb + 2 * qb
    m = np.arange(period)
    d = np.where(m < pb + qb, m, m - period)
    idx = np.clip(pb - d, -REL_CLIP, REL_CLIP) + REL_CLIP
    v = rel_bias.astype(F32)[:, idx]
    flat = jnp.broadcast_to(v[:, None, :], (n_heads, qb, period)).reshape(n_heads, qb * period)
    return flat[:, :qb * (period - 1)].reshape(n_heads, qb, period - 1)[:, :, :pb + qb]


def prompt_bias(rel_bias, qb):
    i = np.arange(qb)[:, None]
    j = np.arange(BAND_PAST + qb)[None, :]
    dc = i // CHUNK - j // CHUNK + PREV_CHUNKS
    valid = (dc >= 0) & (dc <= PREV_CHUNKS)
    return jnp.where(valid[None], _toeplitz_bias(rel_bias, qb, BAND_PAST), NEG_INF)


def sample_bias(rel_bias, s, w):
    return _toeplitz_bias(rel_bias, s, w)


def _s5_kernel(u_ref, s0_ref, bmat_ref, pr_ref, pi_ref, cmat_ref, d_ref, wg_ref, bg_ref,
               y_ref, sl_ref, sc_ref, *, t):
    @pl.when(pl.program_id(1) == 0)
    def _():
        sc_ref[...] = s0_ref[0]

    def cmul(level, x):
        return pr_ref[level:level + 1, :] * x + pi_ref[level:level + 1, :] * pltpu.roll(x, C_LANES, 1)

    u = u_ref[0]
    x = jnp.dot(u.astype(BF16), bmat_ref[...], preferred_element_type=F32)
    row = lax.broadcasted_iota(jnp.int32, (t, 2 * C_LANES), 0)
    x = jnp.where(row == 0, x + cmul(0, sc_ref[...]), x)
    sh, level = 1, 0
    while sh < t:
        x = x + jnp.where(row >= sh, cmul(level, pltpu.roll(x, sh, 0)), 0.0)
        sh *= 2
        level += 1
    s_last = x[t - 1:t, :]
    sc_ref[...] = s_last
    sl_ref[0] = s_last
    y = jnp.dot(x.astype(BF16), cmat_ref[...], preferred_element_type=F32) + d_ref[...] * u
    y = jax.nn.gelu(y)
    gate = jax.nn.sigmoid(jnp.dot(y.astype(BF16), wg_ref[...], preferred_element_type=F32) + bg_ref[...])
    y_ref[0] = y * gate


S5_TILE = 256


def s5(u, s0, bmat, p_re, p_im, cmat, d_skip, w_glu, b_glu):
    bn, seq, _ = u.shape
    t = min(S5_TILE, seq)
    n_levels = p_re.shape[0]
    return pl.pallas_call(
        functools.partial(_s5_kernel, t=t),
        grid=(bn, seq // t),
        in_specs=[
            pl.BlockSpec((1, t, D_C), lambda b, s: (b, s, OFF_C // D_C)),
            pl.BlockSpec((1, 1, 2 * C_LANES), lambda b, s: (b, 0, 0)),
            _full((D_C, 2 * C_LANES)), _full((n_levels, 2 * C_LANES)), _full((n_levels, 2 * C_LANES)),
            _full((2 * C_LANES, D_C)), _full((1, D_C)), _full((D_C, D_C)), _full((1, D_C)),
        ],
        out_specs=[
            pl.BlockSpec((1, t, D_C), lambda b, s: (b, s, 0)),
            pl.BlockSpec((1, 1, 2 * C_LANES), lambda b, s: (b, 0, 0)),
        ],
        out_shape=[
            jax.ShapeDtypeStruct((bn, seq, D_C), F32),
            jax.ShapeDtypeStruct((bn, 1, 2 * C_LANES), F32),
        ],
        scratch_shapes=[pltpu.VMEM((1, 2 * C_LANES), F32)],
        compiler_params=_cparams("parallel", "arbitrary"),
        name="s5",
    )(u, s0, bmat, p_re, p_im, cmat, d_skip.reshape(1, D_C), w_glu, b_glu.reshape(1, D_C))


def s5_params(a_re, a_im, log_dt, b_re, b_im, c_re, c_im, n_levels):
    lam = lax.complex(a_re.astype(F32), a_im.astype(F32))
    dt = jnp.exp(log_dt.astype(F32))[:, None]
    a_bar = jnp.exp(lam * dt)
    b_bar = ((a_bar - 1.0) / lam)[:, :, None] * lax.complex(b_re.astype(F32), b_im.astype(F32))
    eye = jnp.eye(C_GROUPS, dtype=F32)

    def block_in(m):
        return jnp.einsum('gpi,gh->gihp', m, eye).reshape(D_C, C_LANES)

    def block_out(m):
        return jnp.einsum('gip,gh->gphi', m, eye).reshape(C_LANES, D_C)

    bmat = jnp.concatenate([block_in(jnp.real(b_bar)), block_in(jnp.imag(b_bar))], axis=1)
    cmat = jnp.concatenate([block_out(c_re.astype(F32)), -block_out(c_im.astype(F32))], axis=0)
    p_re, p_im = [], []
    power = a_bar
    for _ in range(n_levels):
        re = jnp.real(power).reshape(C_LANES)
        im = jnp.imag(power).reshape(C_LANES)
        p_re.append(jnp.concatenate([re, re]))
        p_im.append(jnp.concatenate([-im, im]))
        power = power * power
    return bmat.astype(BF16), jnp.stack(p_re), jnp.stack(p_im), cmat.astype(BF16)


def _rms(x, g):
    return x * lax.rsqrt(jnp.mean(jnp.square(x), axis=-1, keepdims=True) + RMS_EPS) * g


def _layer_norm(x, g, b):
    mu = jnp.mean(x, axis=-1, keepdims=True)
    xc = x - mu
    var = jnp.mean(jnp.square(xc), axis=-1, keepdims=True)
    return xc * lax.rsqrt(var + LN_EPS) * g + b


def _out_proj_kernel(ya_ref, yb_ref, yc_ref, x_ref, ga_ref, gb_ref, gc_ref, w_ref, lg_ref, lb_ref, wr_ref, br_ref,
                     x1_ref, x1h_ref, logit_ref):
    m = jnp.dot(_rms(ya_ref[...], ga_ref[...]).astype(BF16), w_ref[0:D_A, :], preferred_element_type=F32)
    m += jnp.dot(_rms(yb_ref[...], gb_ref[...]).astype(BF16), w_ref[D_A:D_A + D_B, :], preferred_element_type=F32)
    m += jnp.dot(_rms(yc_ref[...], gc_ref[...]).astype(BF16), w_ref[D_A + D_B:, :], preferred_element_type=F32)
    x1 = _layer_norm(DEEPNORM_ALPHA * x_ref[...] + m, lg_ref[...], lb_ref[...])
    x1_ref[...] = x1
    x1h_ref[...] = x1.astype(BF16)
    logit_ref[...] = jnp.dot(x1, wr_ref[...], preferred_element_type=F32,
                             precision=lax.Precision.HIGHEST) + br_ref[...]


def out_proj(ya, yb, yc, x2d, g_a, g_b, g_c, w_out_bf16, ln_g, ln_b, w_router, b_router):
    n = x2d.shape[0]
    tm = min(ROW_TILE, n)
    rows = lambda w: pl.BlockSpec((tm, w), lambda i: (i, 0))
    return pl.pallas_call(
        _out_proj_kernel,
        grid=(n // tm,),
        in_specs=[rows(D_A), rows(D_B), rows(D_C), rows(D_MODEL),
                  _full((1, D_A)), _full((1, D_B)), _full((1, D_C)), _full((D_MODEL, D_MODEL)),
                  _full((1, D_MODEL)), _full((1, D_MODEL)), _full((D_MODEL, N_EXPERTS)), _full((1, N_EXPERTS))],
        out_specs=[rows(D_MODEL), rows(D_MODEL), rows(N_EXPERTS)],
        out_shape=[jax.ShapeDtypeStruct((n, D_MODEL), F32), jax.ShapeDtypeStruct((n, D_MODEL), BF16),
                   jax.ShapeDtypeStruct((n, N_EXPERTS), F32)],
        compiler_params=_cparams("parallel"),
        name="out_proj",
    )(ya, yb, yc, x2d, g_a.reshape(1, D_A), g_b.reshape(1, D_B), g_c.reshape(1, D_C), w_out_bf16,
      ln_g.reshape(1, D_MODEL), ln_b.reshape(1, D_MODEL), w_router, b_router.reshape(1, N_EXPERTS))


CAST_ROWS = 128


def _cast_weight(src_ref, dst_ref):
    def body(c, carry):
        r = pl.multiple_of(c * CAST_ROWS, CAST_ROWS)
        dst_ref[pl.ds(r, CAST_ROWS), :] = src_ref[0, 0, pl.ds(r, CAST_ROWS), :].astype(BF16)
        return carry

    lax.fori_loop(0, dst_ref.shape[0] // CAST_ROWS, body, 0)


def _moe_kernel(blk_e_ref, n_used_ref, x_ref, wgu_ref, bgu_ref, wdn_ref, bdn_ref, o_ref, wgu_s, wdn_s):
    i = pl.program_id(0)
    expert_changed = (i == 0) | (blk_e_ref[i] != blk_e_ref[jnp.maximum(i - 1, 0)])

    @pl.when(expert_changed)
    def _():
        _cast_weight(wgu_ref, wgu_s)
        _cast_weight(wdn_ref, wdn_s)

    @pl.when(i < n_used_ref[0])
    def _():
        h = jnp.dot(x_ref[...], wgu_s[...], preferred_element_type=F32) + bgu_ref[0, 0]
        gate = jnp.minimum(h[:, :D_FF], SWIGLU_LIMIT)
        up = jnp.clip(h[:, D_FF:], -SWIGLU_LIMIT, SWIGLU_LIMIT)
        glu = gate * jax.nn.sigmoid(gate * SWIGLU_ALPHA)
        act = ((up + 1.0) * glu).astype(BF16)
        o_ref[...] = jnp.dot(act, wdn_s[...], preferred_element_type=F32) + bdn_ref[0, 0]

    @pl.when(i >= n_used_ref[0])
    def _():
        o_ref[...] = jnp.zeros_like(o_ref)


def moe_experts(xb, blk_e, n_used, layer, w_gu, b_gu, w_dn, b_dn):
    n_rows = xb.shape[0]
    n_blocks = n_rows // MOE_ROWS
    depth = w_gu.shape[0]
    grid_spec = pltpu.PrefetchScalarGridSpec(
        num_scalar_prefetch=2,
        grid=(n_blocks,),
        in_specs=[
            pl.BlockSpec((MOE_ROWS, D_MODEL), lambda i, be, nu: (i, 0)),
            pl.BlockSpec((1, 1, D_MODEL, 2 * D_FF), lambda i, be, nu: (layer, be[i], 0, 0)),
            pl.BlockSpec((1, 1, 1, 2 * D_FF), lambda i, be, nu: (layer, be[i], 0, 0)),
            pl.BlockSpec((1, 1, D_FF, D_MODEL), lambda i, be, nu: (layer, be[i], 0, 0)),
            pl.BlockSpec((1, 1, 1, D_MODEL), lambda i, be, nu: (layer, be[i], 0, 0)),
        ],
        out_specs=pl.BlockSpec((MOE_ROWS, D_MODEL), lambda i, be, nu: (i, 0)),
        scratch_shapes=[pltpu.VMEM((D_MODEL, 2 * D_FF), BF16), pltpu.VMEM((D_FF, D_MODEL), BF16)],
    )
    return pl.pallas_call(
        _moe_kernel,
        grid_spec=grid_spec,
        out_shape=jax.ShapeDtypeStruct((n_rows, D_MODEL), F32),
        compiler_params=_cparams("arbitrary"),
        name="moe_experts",
    )(blk_e, n_used, xb, w_gu, b_gu.reshape(depth, N_EXPERTS, 1, 2 * D_FF), w_dn,
      b_dn.reshape(depth, N_EXPERTS, 1, D_MODEL))


def route(logits):
    n_tok = logits.shape[0]
    n_slots = n_tok * TOP_K
    top_v, top_i = lax.top_k(logits, TOP_K)
    gates = jax.nn.softmax(top_v, axis=-1)
    flat_e = top_i.reshape(-1)
    experts = jnp.arange(N_EXPERTS, dtype=jnp.int32)
    onehot = (flat_e[:, None] == experts[None, :]).astype(jnp.int32)
    before = jnp.cumsum(onehot, axis=0) - onehot
    pos = jnp.sum(before * onehot, axis=1)
    counts = jnp.sum(onehot, axis=0)
    padded = ((counts + MOE_ROWS - 1) // MOE_ROWS) * MOE_ROWS
    pad_end = jnp.cumsum(padded)
    pad_start = pad_end - padded
    start = jnp.cumsum(counts) - counts
    dest = (pad_start[flat_e] + pos).astype(jnp.int32).reshape(n_tok, TOP_K).T
    n_blocks = -(-n_slots // MOE_ROWS) + N_EXPERTS
    blk_first = jnp.arange(n_blocks, dtype=jnp.int32) * MOE_ROWS
    blk_e = jnp.minimum(jnp.sum((pad_end[None, :] <= blk_first[:, None]).astype(jnp.int32), axis=1), N_EXPERTS - 1)
    order = jnp.argsort(flat_e).astype(jnp.int32)
    rows = jnp.arange(n_blocks * MOE_ROWS, dtype=jnp.int32)
    row_e = jnp.repeat(blk_e, MOE_ROWS)
    within = rows - pad_start[row_e]
    src = jnp.clip(start[row_e] + within, 0, n_slots - 1)
    row_tok = jnp.where(within < counts[row_e], order[src] // TOP_K, 0).astype(jnp.int32)
    n_used = (pad_end[-1:] // MOE_ROWS).astype(jnp.int32)
    return gates, dest, row_tok, blk_e.astype(jnp.int32), n_used


def _combine_ln_kernel(x_ref, y0_ref, y1_ref, y2_ref, y3_ref, gate_ref, g_ref, b_ref, o_ref):
    gates = gate_ref[...]
    moe = (y0_ref[...] * gates[:, 0:1] + y1_ref[...] * gates[:, 1:2]
           + y2_ref[...] * gates[:, 2:3] + y3_ref[...] * gates[:, 3:4])
    o_ref[...] = _layer_norm(DEEPNORM_ALPHA * x_ref[...] + moe, g_ref[...], b_ref[...])


def combine_ln(x2d, ys, gates, ln_g, ln_b):
    n = x2d.shape[0]
    tm = min(ROW_TILE, n)
    rows = pl.BlockSpec((tm, D_MODEL), lambda i: (i, 0))
    return pl.pallas_call(
        _combine_ln_kernel,
        grid=(n // tm,),
        in_specs=[rows] * (1 + TOP_K) + [pl.BlockSpec((tm, TOP_K), lambda i: (i, 0)),
                                         _full((1, D_MODEL)), _full((1, D_MODEL))],
        out_specs=rows,
        out_shape=jax.ShapeDtypeStruct((n, D_MODEL), F32),
        compiler_params=_cparams("parallel"),
        name="combine_ln",
    )(x2d, *ys, gates, ln_g.reshape(1, D_MODEL), ln_b.reshape(1, D_MODEL))


def _block_diag(w):
    h, d, _ = w.shape
    return jnp.einsum('hij,hg->higj', w.astype(F32), jnp.eye(h, dtype=F32)).reshape(h * d, h * d)


def _layer(x, state, p, attn_bias):
    bn, seq, _ = x.shape
    n = bn * seq
    conv_buf, h0, s0, k_cache, v_cache = state
    x2d = x.reshape(n, D_MODEL)
    u = in_proj(x2d, p['w_in']).reshape(bn, seq, IN_COLS)

    y_a, new_buf, h_last = rglru(u, conv_buf, h0, p['conv_w'], p['conv_b'], p['w_r'], p['b_r'], p['w_i'], p['b_i'],
                                 p['lam'])
    q_col, k_col, v_col = OFF_Q // LANE, OFF_K // LANE, OFF_V // LANE
    if k_cache is None:
        qb = min(BAND_PAST, seq)
        y_b = attention(u, q_col, u, k_col, u, v_col, u, k_col, v_col, attn_bias, qb=qb, pb=qb,
                        prev_is_same_array=True)
    else:
        y_b = attention(u, q_col, k_cache, 0, v_cache, 0, u, k_col, v_col, attn_bias, qb=seq,
                        pb=k_cache.shape[1], prev_is_same_array=False)
    y_c, s_last = s5(u, s0, p['bmat'], p['p_re'], p['p_im'], p['cmat'], p['d_c'], p['w_glu'], p['b_glu'])

    x1, x1h, logits = out_proj(y_a.reshape(n, D_A), y_b.reshape(n, D_B), y_c.reshape(n, D_C), x2d,
                               p['g_a'], p['g_b'], p['g_c'], p['w_out'], p['ln1_g'], p['ln1_b'],
                               p['w_router'], p['b_router'])
    gates, dest, row_tok, blk_e, n_used = route(logits)
    yb = moe_experts(x1h[row_tok], blk_e, n_used, p['layer'], p['w_gu'], p['b_gu'], p['w_dn'], p['b_dn'])
    x2 = combine_ln(x1, [yb[dest[k]] for k in range(TOP_K)], gates, p['ln2_g'], p['ln2_b'])
    x2 = x2.reshape(bn, seq, D_MODEL)

    k_rows = u[:, :, OFF_K:OFF_V]
    v_rows = u[:, :, OFF_V:OFF_C]
    if k_cache is None:
        keep = min(BAND_PAST, seq)
        k_rows = k_rows[:, seq - keep:]
        v_rows = v_rows[:, seq - keep:]
    k_rows = k_rows.reshape(bn, -1, B_HEADS, B_HEAD_DIM)
    v_rows = v_rows.reshape(bn, -1, B_HEADS, B_HEAD_DIM)
    s_re = s_last[:, 0, :C_LANES].reshape(bn, C_GROUPS, C_STATE)
    s_im = s_last[:, 0, C_LANES:].reshape(bn, C_GROUPS, C_STATE)
    return x2, (new_buf, h_last.reshape(bn, D_A), k_rows, v_rows, s_re, s_im)


def kernel(x_prompt, x_sample, cache_conv_a, state_h_a, cache_k_b, cache_v_b, state_s_re_c, state_s_im_c, w_in, conv_w_a, conv_b_a, w_r_a, b_r_a, w_i_a, b_i_a, lambda_a, rel_bias_b, a_re_c, a_im_c, log_dt_c, b_re_c, b_im_c, c_re_c, c_im_c, d_c, w_glu_c, b_glu_c, g_norm_a, g_norm_b, g_norm_c, w_out, ln1_g, ln1_b, w_router, b_router, w_gu, b_gu, w_dn, b_dn, ln2_g, ln2_b):
    bp, seq_p, _ = x_prompt.shape
    bs, seq_s, _ = x_sample.shape
    kv_rows = cache_k_b.shape[2]
    n_levels = max(1, int(math.log2(min(S5_TILE, max(seq_p, seq_s)))))
    yp, ys = x_prompt, x_sample
    p_states = [[] for _ in range(6)]
    s_states = [[] for _ in range(6)]
    for l in range(DEPTH):
        bmat, p_re, p_im, cmat = s5_params(a_re_c[l], a_im_c[l], log_dt_c[l], b_re_c[l], b_im_c[l], c_re_c[l],
                                           c_im_c[l], n_levels)
        p = dict(
            w_in=w_in[l].astype(BF16), conv_w=conv_w_a[l], conv_b=conv_b_a[l],
            w_r=_block_diag(w_r_a[l]), b_r=b_r_a[l], w_i=_block_diag(w_i_a[l]), b_i=b_i_a[l], lam=lambda_a[l],
            bmat=bmat, p_re=p_re, p_im=p_im, cmat=cmat, d_c=d_c[l], w_glu=w_glu_c[l].astype(BF16), b_glu=b_glu_c[l],
            g_a=g_norm_a[l], g_b=g_norm_b[l], g_c=g_norm_c[l], w_out=w_out[l].astype(BF16),
            ln1_g=ln1_g[l], ln1_b=ln1_b[l], w_router=w_router[l], b_router=b_router[l],
            layer=l, w_gu=w_gu, b_gu=b_gu, w_dn=w_dn, b_dn=b_dn,
            ln2_g=ln2_g[l], ln2_b=ln2_b[l],
        )
        zero_state = (jnp.zeros((bp, CONV_W - 1, D_A), F32), jnp.zeros((bp, D_A), F32),
                      jnp.zeros((bp, 1, 2 * C_LANES), F32), None, None)
        yp, st = _layer(yp, zero_state, p, prompt_bias(rel_bias_b[l], min(BAND_PAST, seq_p)))
        for lst, s in zip(p_states, st):
            lst.append(s)
        s0 = jnp.concatenate([state_s_re_c[l].reshape(bs, 1, C_LANES), state_s_im_c[l].reshape(bs, 1, C_LANES)],
                             axis=-1)
        sample_state = (cache_conv_a[l], state_h_a[l], s0,
                        cache_k_b[l].reshape(bs, kv_rows, D_B), cache_v_b[l].reshape(bs, kv_rows, D_B))
        ys, st = _layer(ys, sample_state, p, sample_bias(rel_bias_b[l], seq_s, kv_rows))
        for lst, s in zip(s_states, st):
            lst.append(s)
    return (yp, ys) + tuple(jnp.stack(s) for s in p_states) + tuple(jnp.stack(s) for s in s_states)
```

```python
import functools
import math

import jax
import jax.numpy as jnp
import numpy as np
from jax import lax
from jax.experimental import pallas as pl
from jax.experimental.pallas import tpu as pltpu

F32 = jnp.float32
BF16 = jnp.bfloat16

D_MODEL = 1024
DEPTH = 2
CHUNK = 64
PREV_CHUNKS = 8
BAND_PAST = PREV_CHUNKS * CHUNK
D_A = D_MODEL // 4
D_B = D_MODEL // 2
D_C = D_MODEL // 4
A_HEADS = 4
A_HEAD_DIM = D_A // A_HEADS
CONV_W = 4
RG_C = 8.0
B_HEADS = 8
B_HEAD_DIM = D_B // B_HEADS
REL_CLIP = 128
C_GW = 16
C_GROUPS = D_C // C_GW
C_STATE = 64
C_LANES = C_GROUPS * C_STATE
OFF_GA = D_A
OFF_Q = 2 * D_A
OFF_K = OFF_Q + D_B
OFF_V = OFF_K + D_B
OFF_C = OFF_V + D_B
IN_COLS = OFF_C + D_C
N_EXPERTS = 32
TOP_K = 4
D_FF = D_MODEL
SWIGLU_LIMIT = 7.0
SWIGLU_ALPHA = 1.702
DEEPNORM_ALPHA = (2 * DEPTH) ** 0.25
LN_EPS = 1e-5
RMS_EPS = 1e-6
NEG_INF = -1e30

LANE = 128
SUBLANES = 8
HEADS_PER_SLAB = LANE // B_HEAD_DIM
VMEM_LIMIT = 56 * 1024 * 1024

ROW_TILE = 512
MOE_ROWS = 512
ATTN_SUB_ROWS = 256


def _cparams(*sem):
    return pltpu.CompilerParams(dimension_semantics=sem, vmem_limit_bytes=VMEM_LIMIT)


def _full(shape):
    return pl.BlockSpec(shape, lambda *_: (0,) * len(shape))


def _in_proj_kernel(x_ref, w_ref, o_ref):
    o_ref[...] = jnp.dot(x_ref[...].astype(BF16), w_ref[...], preferred_element_type=F32)


def in_proj(x2d, w_bf16):
    n = x2d.shape[0]
    tm = min(ROW_TILE, n)
    return pl.pallas_call(
        _in_proj_kernel,
        grid=(n // tm,),
        in_specs=[pl.BlockSpec((tm, D_MODEL), lambda i: (i, 0)), _full((D_MODEL, IN_COLS))],
        out_specs=pl.BlockSpec((tm, IN_COLS), lambda i: (i, 0)),
        out_shape=jax.ShapeDtypeStruct((n, IN_COLS), F32),
        compiler_params=_cparams("parallel"),
        name="in_proj",
    )(x2d, w_bf16)


def _rglru_kernel(xa_ref, ga_ref, buf_ref, h0_ref, cw_ref, cb_ref, wr_ref, br_ref, wi_ref, bi_ref, lam_ref,
                  y_ref, nbuf_ref, hl_ref, xp_ref, hc_ref, as_ref, hs_ref, *, t):
    pad = SUBLANES
    hist = CONV_W - 1

    @pl.when(pl.program_id(1) == 0)
    def _():
        xp_ref[0:pad, :] = jnp.zeros((pad, D_A), F32)
        xp_ref[pad - hist:pad, :] = buf_ref[0]
        hc_ref[...] = h0_ref[0]

    xa = xa_ref[0]
    xp_ref[pad:pad + t, :] = xa
    xc = cb_ref[...] + xa * cw_ref[hist:hist + 1, :]
    for j in range(hist):
        xc = xc + xp_ref[pad - hist + j:pad - hist + j + t, :] * cw_ref[j:j + 1, :]
    tail = xp_ref[pad + t - hist:pad + t, :]
    nbuf_ref[0] = tail
    xp_ref[pad - hist:pad, :] = tail

    xch = xc.astype(BF16)
    r = jax.nn.sigmoid(jnp.dot(xch, wr_ref[...], preferred_element_type=F32) + br_ref[...])
    i = jax.nn.sigmoid(jnp.dot(xch, wi_ref[...], preferred_element_type=F32) + bi_ref[...])
    lam = lam_ref[...]
    softplus_neg_lam = jnp.maximum(-lam, 0.0) + jnp.log(1.0 + jnp.exp(-jnp.abs(lam)))
    log_a = (-RG_C) * r * softplus_neg_lam
    a = jnp.exp(log_a)
    b = jnp.sqrt(1.0 - jnp.exp(2.0 * log_a)) * (i * xc)

    row_in_group = lax.broadcasted_iota(jnp.int32, (t, D_A), 0) & (SUBLANES - 1)
    acc_a, acc_b = a, b
    sh = 1
    while sh < SUBLANES:
        m = row_in_group >= sh
        acc_b = jnp.where(m, acc_a * pltpu.roll(acc_b, sh, 0) + acc_b, acc_b)
        acc_a = jnp.where(m, acc_a * pltpu.roll(acc_a, sh, 0), acc_a)
        sh *= 2
    as_ref[...] = acc_a
    hs_ref[...] = acc_b

    def group(g, carry):
        rows = pl.ds(pl.multiple_of(g * SUBLANES, SUBLANES), SUBLANES)
        h = hs_ref[rows, :] + as_ref[rows, :] * carry
        hs_ref[rows, :] = h
        return h[SUBLANES - 1:SUBLANES, :]

    h_last = lax.fori_loop(0, t // SUBLANES, group, hc_ref[...], unroll=8)
    hc_ref[...] = h_last
    hl_ref[0] = h_last
    y_ref[0] = hs_ref[...] * jax.nn.gelu(ga_ref[0])


def rglru(u, conv_buf, h0, cw, cb, wr_bd, br, wi_bd, bi, lam):
    bn, seq, _ = u.shape
    t = min(ROW_TILE, seq)
    vec = _full((1, D_A))
    return pl.pallas_call(
        functools.partial(_rglru_kernel, t=t),
        grid=(bn, seq // t),
        in_specs=[
            pl.BlockSpec((1, t, D_A), lambda b, s: (b, s, 0)),
            pl.BlockSpec((1, t, D_A), lambda b, s: (b, s, OFF_GA // D_A)),
            pl.BlockSpec((1, CONV_W - 1, D_A), lambda b, s: (b, 0, 0)),
            pl.BlockSpec((1, 1, D_A), lambda b, s: (b, 0, 0)),
            _full((CONV_W, D_A)), vec, _full((D_A, D_A)), vec, _full((D_A, D_A)), vec, vec,
        ],
        out_specs=[
            pl.BlockSpec((1, t, D_A), lambda b, s: (b, s, 0)),
            pl.BlockSpec((1, CONV_W - 1, D_A), lambda b, s: (b, 0, 0)),
            pl.BlockSpec((1, 1, D_A), lambda b, s: (b, 0, 0)),
        ],
        out_shape=[
            jax.ShapeDtypeStruct((bn, seq, D_A), F32),
            jax.ShapeDtypeStruct((bn, CONV_W - 1, D_A), F32),
            jax.ShapeDtypeStruct((bn, 1, D_A), F32),
        ],
        scratch_shapes=[pltpu.VMEM((t + SUBLANES, D_A), F32), pltpu.VMEM((1, D_A), F32),
                        pltpu.VMEM((t, D_A), F32), pltpu.VMEM((t, D_A), F32)],
        compiler_params=_cparams("parallel", "arbitrary"),
        name="rglru",
    )(u, u, conv_buf, h0.reshape(bn, 1, D_A), cw, cb.reshape(1, D_A), wr_bd, br.reshape(1, D_A),
      wi_bd, bi.reshape(1, D_A), lam.reshape(1, D_A))


def _attn_kernel(q_ref, kp_ref, kc_ref, vp_ref, vc_ref, bias_ref, o_ref, *, pb, sq, mask_first_prev):
    scale = B_HEAD_DIM ** -0.5
    qb = q_ref.shape[1]
    q_all = q_ref[0].astype(BF16)
    kp_all = kp_ref[0].astype(BF16)
    kc_all = kc_ref[0].astype(BF16)
    vp_all = vp_ref[0].astype(BF16)
    vc_all = vc_ref[0].astype(BF16)
    lane = lax.broadcasted_iota(jnp.int32, (1, LANE), 1)
    contract_last = (((1,), (1,)), ((), ()))
    for r in range(qb // sq):
        lo = max(pb + r * sq - BAND_PAST, 0)
        hi = (r + 1) * sq
        q = q_all[r * sq:hi, :]
        kp, vp = kp_all[lo:pb, :], vp_all[lo:pb, :]
        kc, vc = kc_all[0:hi, :], vc_all[0:hi, :]
        out = None
        for hh in range(HEADS_PER_SLAB):
            in_head = (lane // B_HEAD_DIM) == hh
            qh = jnp.where(in_head, q, jnp.zeros_like(q))
            sp = (lax.dot_general(qh, kp, contract_last, preferred_element_type=F32) * scale
                  + bias_ref[hh, r * sq:hi, lo:pb])
            sc = (lax.dot_general(qh, kc, contract_last, preferred_element_type=F32) * scale
                  + bias_ref[hh, r * sq:hi, pb:pb + hi])
            if mask_first_prev:
                sp = jnp.where(pl.program_id(2) == 0, NEG_INF, sp)
            m = jnp.maximum(jnp.max(sp, axis=-1, keepdims=True), jnp.max(sc, axis=-1, keepdims=True))
            ep = jnp.exp(sp - m)
            ec = jnp.exp(sc - m)
            denom = jnp.sum(ep, axis=-1, keepdims=True) + jnp.sum(ec, axis=-1, keepdims=True)
            o = (jnp.dot(ep.astype(BF16), vp, preferred_element_type=F32)
                 + jnp.dot(ec.astype(BF16), vc, preferred_element_type=F32)) / denom
            out = o if out is None else jnp.where(in_head, o, out)
        o_ref[0, r * sq:hi, :] = out


def attention(q_arr, q_col, kprev_arr, kprev_col, vprev_arr, vprev_col, kv_arr, k_col, v_col, bias, *, qb, pb,
              prev_is_same_array):
    bn, seq, _ = q_arr.shape
    n_slabs = B_HEADS // HEADS_PER_SLAB
    if prev_is_same_array:
        prev_map = lambda col: (lambda hp, b, s: (b, jnp.maximum(s - 1, 0), col + hp))
    else:
        prev_map = lambda col: (lambda hp, b, s: (b, 0, col + hp))
    cur_map = lambda col: (lambda hp, b, s: (b, s, col + hp))
    return pl.pallas_call(
        functools.partial(_attn_kernel, pb=pb, sq=min(qb, ATTN_SUB_ROWS), mask_first_prev=prev_is_same_array),
        grid=(n_slabs, bn, seq // qb),
        in_specs=[
            pl.BlockSpec((1, qb, LANE), cur_map(q_col)),
            pl.BlockSpec((1, pb, LANE), prev_map(kprev_col)),
            pl.BlockSpec((1, qb, LANE), cur_map(k_col)),
            pl.BlockSpec((1, pb, LANE), prev_map(vprev_col)),
            pl.BlockSpec((1, qb, LANE), cur_map(v_col)),
            pl.BlockSpec((HEADS_PER_SLAB, qb, pb + qb), lambda hp, b, s: (hp, 0, 0)),
        ],
        out_specs=pl.BlockSpec((1, qb, LANE), lambda hp, b, s: (b, s, hp)),
        out_shape=jax.ShapeDtypeStruct((bn, seq, D_B), F32),
        compiler_params=_cparams("arbitrary", "arbitrary", "arbitrary"),
        name="attention",
    )(q_arr, kprev_arr, kv_arr, vprev_arr, kv_arr, bias)


def _toeplitz_bias(rel_bias, qb, pb):
    n_heads = rel_bias.shape[0]
    period = pb + 2 * qb
    m = np.arange(period)
    d = np.where(m < pb + qb, m, m - period)
    idx = np.clip(pb - d, -REL_CLIP, REL_CLIP) + REL_CLIP
    v = rel_bias.astype(F32)[:, idx]
    flat = jnp.broadcast_to(v[:, None, :], (n_heads, qb, period)).reshape(n_heads, qb * period)
    return flat[:, :qb * (period - 1)].reshape(n_heads, qb, period - 1)[:, :, :pb + qb]


def prompt_bias(rel_bias, qb):
    i = np.arange(qb)[:, None]
    j = np.arange(BAND_PAST + qb)[None, :]
    dc = i // CHUNK - j // CHUNK + PREV_CHUNKS
    valid = (dc >= 0) & (dc <= PREV_CHUNKS)
    return jnp.where(valid[None], _toeplitz_bias(rel_bias, qb, BAND_PAST), NEG_INF)


def sample_bias(rel_bias, s, w):
    return _toeplitz_bias(rel_bias, s, w)


def _s5_kernel(u_ref, s0_ref, bmat_ref, ar_ref, ai_ref, cmat_ref, d_ref, wg_ref, bg_ref,
               y_ref, sl_ref, sc_ref, xs_ref, *, steps):
    @pl.when(pl.program_id(1) == 0)
    def _():
        sc_ref[...] = s0_ref[0]

    u = u_ref[0]
    xs_ref[...] = jnp.dot(u.astype(BF16), bmat_ref[...], preferred_element_type=F32)
    a_re = ar_ref[...]
    a_im = ai_ref[...]

    def step(t, s):
        rows = pl.ds(pl.multiple_of(t * SUBLANES, SUBLANES), SUBLANES)
        s = xs_ref[rows, :] + a_re * s + a_im * pltpu.roll(s, C_LANES, 1)
        xs_ref[rows, :] = s
        return s

    s_last = lax.fori_loop(0, steps, step, sc_ref[...], unroll=4)
    sc_ref[...] = s_last
    sl_ref[0] = s_last
    y = jnp.dot(xs_ref[...].astype(BF16), cmat_ref[...], preferred_element_type=F32) + d_ref[...] * u
    y = jax.nn.gelu(y)
    gate = jax.nn.sigmoid(jnp.dot(y.astype(BF16), wg_ref[...], preferred_element_type=F32) + bg_ref[...])
    y_ref[0] = y * gate


S5_STEPS = 64


def s5(u, s0, bmat, a_tables, cmat, d_skip, w_glu, b_glu):
    bn, seq, _ = u.shape
    groups = bn // SUBLANES
    steps = min(S5_STEPS, seq)
    rows = steps * SUBLANES
    uc = u[:, :, OFF_C:].reshape(groups, SUBLANES, seq, D_C).transpose(0, 2, 1, 3).reshape(groups, seq * SUBLANES, D_C)
    y_tm, s_last = pl.pallas_call(
        functools.partial(_s5_kernel, steps=steps),
        grid=(groups, seq // steps),
        in_specs=[
            pl.BlockSpec((1, rows, D_C), lambda g, s: (g, s, 0)),
            pl.BlockSpec((1, SUBLANES, 2 * C_LANES), lambda g, s: (g, 0, 0)),
            _full((D_C, 2 * C_LANES)), _full((SUBLANES, 2 * C_LANES)), _full((SUBLANES, 2 * C_LANES)),
            _full((2 * C_LANES, D_C)), _full((1, D_C)), _full((D_C, D_C)), _full((1, D_C)),
        ],
        out_specs=[
            pl.BlockSpec((1, rows, D_C), lambda g, s: (g, s, 0)),
            pl.BlockSpec((1, SUBLANES, 2 * C_LANES), lambda g, s: (g, 0, 0)),
        ],
        out_shape=[
            jax.ShapeDtypeStruct((groups, seq * SUBLANES, D_C), F32),
            jax.ShapeDtypeStruct((groups, SUBLANES, 2 * C_LANES), F32),
        ],
        scratch_shapes=[pltpu.VMEM((SUBLANES, 2 * C_LANES), F32), pltpu.VMEM((rows, 2 * C_LANES), F32)],
        compiler_params=_cparams("parallel", "arbitrary"),
        name="s5",
    )(uc, s0.reshape(groups, SUBLANES, 2 * C_LANES), bmat, *a_tables, cmat, d_skip.reshape(1, D_C), w_glu,
      b_glu.reshape(1, D_C))
    y = y_tm.reshape(groups, seq, SUBLANES, D_C).transpose(0, 2, 1, 3).reshape(bn, seq, D_C)
    return y, s_last.reshape(bn, 1, 2 * C_LANES)


def s5_params(a_re, a_im, log_dt, b_re, b_im, c_re, c_im):
    lam = lax.complex(a_re.astype(F32), a_im.astype(F32))
    dt = jnp.exp(log_dt.astype(F32))[:, None]
    a_bar = jnp.exp(lam * dt)
    b_bar = ((a_bar - 1.0) / lam)[:, :, None] * lax.complex(b_re.astype(F32), b_im.astype(F32))
    eye = jnp.eye(C_GROUPS, dtype=F32)

    def block_in(m):
        return jnp.einsum('gpi,gh->gihp', m, eye).reshape(D_C, C_LANES)

    def block_out(m):
        return jnp.einsum('gip,gh->gphi', m, eye).reshape(C_LANES, D_C)

    bmat = jnp.concatenate([block_in(jnp.real(b_bar)), block_in(jnp.imag(b_bar))], axis=1)
    cmat = jnp.concatenate([block_out(c_re.astype(F32)), -block_out(c_im.astype(F32))], axis=0)
    re = jnp.real(a_bar).reshape(1, C_LANES)
    im = jnp.imag(a_bar).reshape(1, C_LANES)
    a_re2 = jnp.broadcast_to(jnp.concatenate([re, re], axis=1), (SUBLANES, 2 * C_LANES))
    a_im2 = jnp.broadcast_to(jnp.concatenate([-im, im], axis=1), (SUBLANES, 2 * C_LANES))
    return bmat.astype(BF16), (a_re2, a_im2), cmat.astype(BF16)


def _rms(x, g):
    return x * lax.rsqrt(jnp.mean(jnp.square(x), axis=-1, keepdims=True) + RMS_EPS) * g


def _layer_norm(x, g, b):
    mu = jnp.mean(x, axis=-1, keepdims=True)
    xc = x - mu
    var = jnp.mean(jnp.square(xc), axis=-1, keepdims=True)
    return xc * lax.rsqrt(var + LN_EPS) * g + b


def _out_proj_kernel(ya_ref, yb_ref, yc_ref, x_ref, ga_ref, gb_ref, gc_ref, w_ref, lg_ref, lb_ref,
                     wrh_ref, wrl_ref, br_ref,
                     x1_ref, x1h_ref, topi_ref, gate_ref, pos_ref, cnt_ref, seen_ref):
    @pl.when(pl.program_id(0) == 0)
    def _():
        seen_ref[...] = jnp.zeros_like(seen_ref)

    m = jnp.dot(_rms(ya_ref[...], ga_ref[...]).astype(BF16), w_ref[0:D_A, :], preferred_element_type=F32)
    m += jnp.dot(_rms(yb_ref[...], gb_ref[...]).astype(BF16), w_ref[D_A:D_A + D_B, :], preferred_element_type=F32)
    m += jnp.dot(_rms(yc_ref[...], gc_ref[...]).astype(BF16), w_ref[D_A + D_B:, :], preferred_element_type=F32)
    x1 = _layer_norm(DEEPNORM_ALPHA * x_ref[...] + m, lg_ref[...], lb_ref[...])
    x1_ref[...] = x1
    x1h = x1.astype(BF16)
    x1h_ref[...] = x1h

    x1l = (x1 - x1h.astype(F32)).astype(BF16)
    logits = (jnp.dot(x1h, wrh_ref[...], preferred_element_type=F32)
              + jnp.dot(x1l, wrh_ref[...], preferred_element_type=F32)
              + jnp.dot(x1h, wrl_ref[...], preferred_element_type=F32)) + br_ref[...]

    tm = logits.shape[0]
    col = lax.broadcasted_iota(jnp.int32, (tm, N_EXPERTS), 1).astype(F32)
    work = logits
    sels, vals, idxs = [], [], []
    for _ in range(TOP_K):
        top = jnp.max(work, axis=1, keepdims=True)
        idx = jnp.min(jnp.where(work == top, col, float(N_EXPERTS)), axis=1, keepdims=True)
        sel = col == idx
        work = jnp.where(sel, -jnp.inf, work)
        sels.append(sel)
        vals.append(top)
        idxs.append(idx)
    exps = [jnp.exp(v - vals[0]) for v in vals]
    total = functools.reduce(lambda a, b: a + b, exps)

    chosen = functools.reduce(lambda a, b: a + b, [s.astype(F32) for s in sels])
    ri = lax.broadcasted_iota(jnp.int32, (tm, tm), 0)
    ci = lax.broadcasted_iota(jnp.int32, (tm, tm), 1)
    earlier = jnp.where(ci < ri, 1.0, 0.0).astype(BF16)
    before = jnp.dot(earlier, chosen.astype(BF16), preferred_element_type=F32) + seen_ref[...]
    seen = seen_ref[...] + jnp.sum(chosen, axis=0, keepdims=True)
    seen_ref[...] = seen
    cnt_ref[...] = seen.astype(jnp.int32)

    slot = lax.broadcasted_iota(jnp.int32, (tm, TOP_K), 1)
    top_i = jnp.zeros((tm, TOP_K), F32)
    gates = jnp.zeros((tm, TOP_K), F32)
    pos = jnp.zeros((tm, TOP_K), F32)
    for k in range(TOP_K):
        rank = jnp.sum(jnp.where(sels[k], before, 0.0), axis=1, keepdims=True)
        top_i = jnp.where(slot == k, idxs[k], top_i)
        gates = jnp.where(slot == k, exps[k] / total, gates)
        pos = jnp.where(slot == k, rank, pos)
    topi_ref[...] = top_i.astype(jnp.int32)
    gate_ref[...] = gates
    pos_ref[...] = pos.astype(jnp.int32)


def out_proj(ya, yb, yc, x2d, g_a, g_b, g_c, w_out_bf16, ln_g, ln_b, w_router, b_router):
    n = x2d.shape[0]
    tm = min(ROW_TILE, n)
    rows = lambda w: pl.BlockSpec((tm, w), lambda i: (i, 0))
    wr = w_router.astype(F32)
    wr_hi = wr.astype(BF16)
    wr_lo = (wr - wr_hi.astype(F32)).astype(BF16)
    return pl.pallas_call(
        _out_proj_kernel,
        grid=(n // tm,),
        in_specs=[rows(D_A), rows(D_B), rows(D_C), rows(D_MODEL),
                  _full((1, D_A)), _full((1, D_B)), _full((1, D_C)), _full((D_MODEL, D_MODEL)),
                  _full((1, D_MODEL)), _full((1, D_MODEL)),
                  _full((D_MODEL, N_EXPERTS)), _full((D_MODEL, N_EXPERTS)), _full((1, N_EXPERTS))],
        out_specs=[rows(D_MODEL), rows(D_MODEL), rows(TOP_K), rows(TOP_K), rows(TOP_K), _full((1, N_EXPERTS))],
        out_shape=[jax.ShapeDtypeStruct((n, D_MODEL), F32), jax.ShapeDtypeStruct((n, D_MODEL), BF16),
                   jax.ShapeDtypeStruct((n, TOP_K), jnp.int32), jax.ShapeDtypeStruct((n, TOP_K), F32),
                   jax.ShapeDtypeStruct((n, TOP_K), jnp.int32), jax.ShapeDtypeStruct((1, N_EXPERTS), jnp.int32)],
        scratch_shapes=[pltpu.VMEM((1, N_EXPERTS), F32)],
        compiler_params=_cparams("arbitrary"),
        name="out_proj",
    )(ya, yb, yc, x2d, g_a.reshape(1, D_A), g_b.reshape(1, D_B), g_c.reshape(1, D_C), w_out_bf16,
      ln_g.reshape(1, D_MODEL), ln_b.reshape(1, D_MODEL), wr_hi, wr_lo, b_router.reshape(1, N_EXPERTS))


CAST_ROWS = 128


def _cast_weight(src_ref, dst_ref):
    def body(c, carry):
        r = pl.multiple_of(c * CAST_ROWS, CAST_ROWS)
        dst_ref[pl.ds(r, CAST_ROWS), :] = src_ref[0, 0, pl.ds(r, CAST_ROWS), :].astype(BF16)
        return carry

    lax.fori_loop(0, dst_ref.shape[0] // CAST_ROWS, body, 0)


def _moe_kernel(blk_e_ref, n_used_ref, x_ref, wgu_ref, bgu_ref, wdn_ref, bdn_ref, o_ref, wgu_s, wdn_s):
    i = pl.program_id(0)
    expert_changed = (i == 0) | (blk_e_ref[i] != blk_e_ref[jnp.maximum(i - 1, 0)])

    @pl.when(expert_changed)
    def _():
        _cast_weight(wgu_ref, wgu_s)
        _cast_weight(wdn_ref, wdn_s)

    @pl.when(i < n_used_ref[0])
    def _():
        h = jnp.dot(x_ref[...], wgu_s[...], preferred_element_type=F32) + bgu_ref[0, 0]
        gate = jnp.minimum(h[:, :D_FF], SWIGLU_LIMIT)
        up = jnp.clip(h[:, D_FF:], -SWIGLU_LIMIT, SWIGLU_LIMIT)
        glu = gate * jax.nn.sigmoid(gate * SWIGLU_ALPHA)
        act = ((up + 1.0) * glu).astype(BF16)
        o_ref[...] = jnp.dot(act, wdn_s[...], preferred_element_type=F32) + bdn_ref[0, 0]

    @pl.when(i >= n_used_ref[0])
    def _():
        o_ref[...] = jnp.zeros_like(o_ref)


def moe_experts(xb, blk_e, n_used, layer, w_gu, b_gu, w_dn, b_dn):
    n_rows = xb.shape[0]
    n_blocks = n_rows // MOE_ROWS
    depth = w_gu.shape[0]
    grid_spec = pltpu.PrefetchScalarGridSpec(
        num_scalar_prefetch=2,
        grid=(n_blocks,),
        in_specs=[
            pl.BlockSpec((MOE_ROWS, D_MODEL), lambda i, be, nu: (i, 0)),
            pl.BlockSpec((1, 1, D_MODEL, 2 * D_FF), lambda i, be, nu: (layer, be[i], 0, 0)),
            pl.BlockSpec((1, 1, 1, 2 * D_FF), lambda i, be, nu: (layer, be[i], 0, 0)),
            pl.BlockSpec((1, 1, D_FF, D_MODEL), lambda i, be, nu: (layer, be[i], 0, 0)),
            pl.BlockSpec((1, 1, 1, D_MODEL), lambda i, be, nu: (layer, be[i], 0, 0)),
        ],
        out_specs=pl.BlockSpec((MOE_ROWS, D_MODEL), lambda i, be, nu: (i, 0)),
        scratch_shapes=[pltpu.VMEM((D_MODEL, 2 * D_FF), BF16), pltpu.VMEM((D_FF, D_MODEL), BF16)],
    )
    return pl.pallas_call(
        _moe_kernel,
        grid_spec=grid_spec,
        out_shape=jax.ShapeDtypeStruct((n_rows, D_MODEL), F32),
        compiler_params=_cparams("arbitrary"),
        name="moe_experts",
    )(blk_e, n_used, xb, w_gu, b_gu.reshape(depth, N_EXPERTS, 1, 2 * D_FF), w_dn,
      b_dn.reshape(depth, N_EXPERTS, 1, D_MODEL))


def route(top_i, pos, counts):
    n_tok = top_i.shape[0]
    n_slots = n_tok * TOP_K
    padded = ((counts + MOE_ROWS - 1) // MOE_ROWS) * MOE_ROWS
    pad_end = jnp.cumsum(padded)
    pad_start = pad_end - padded
    start = jnp.cumsum(counts) - counts
    experts = jnp.arange(N_EXPERTS, dtype=jnp.int32)
    slot_start = jnp.sum(jnp.where(top_i[:, :, None] == experts, pad_start, 0), axis=-1)
    dest = (slot_start + pos).astype(jnp.int32).T
    n_blocks = -(-n_slots // MOE_ROWS) + N_EXPERTS
    blk_first = jnp.arange(n_blocks, dtype=jnp.int32) * MOE_ROWS
    blk_e = jnp.minimum(jnp.sum((pad_end[None, :] <= blk_first[:, None]).astype(jnp.int32), axis=1), N_EXPERTS - 1)
    order = jnp.argsort(top_i.reshape(-1)).astype(jnp.int32)
    within = jnp.arange(MOE_ROWS, dtype=jnp.int32)[None, :] + (blk_first - pad_start[blk_e])[:, None]
    src = jnp.clip(start[blk_e][:, None] + within, 0, n_slots - 1).reshape(-1)
    valid = (within < counts[blk_e][:, None]).reshape(-1)
    row_tok = jnp.where(valid, order[src] // TOP_K, 0).astype(jnp.int32)
    n_used = (pad_end[-1:] // MOE_ROWS).astype(jnp.int32)
    return dest, row_tok, blk_e.astype(jnp.int32), n_used


def _combine_ln_kernel(x_ref, y0_ref, y1_ref, y2_ref, y3_ref, gate_ref, g_ref, b_ref, o_ref):
    gates = gate_ref[...]
    moe = (y0_ref[...] * gates[:, 0:1] + y1_ref[...] * gates[:, 1:2]
           + y2_ref[...] * gates[:, 2:3] + y3_ref[...] * gates[:, 3:4])
    o_ref[...] = _layer_norm(DEEPNORM_ALPHA * x_ref[...] + moe, g_ref[...], b_ref[...])


def combine_ln(x2d, ys, gates, ln_g, ln_b):
    n = x2d.shape[0]
    tm = min(ROW_TILE, n)
    rows = pl.BlockSpec((tm, D_MODEL), lambda i: (i, 0))
    return pl.pallas_call(
        _combine_ln_kernel,
        grid=(n // tm,),
        in_specs=[rows] * (1 + TOP_K) + [pl.BlockSpec((tm, TOP_K), lambda i: (i, 0)),
                                         _full((1, D_MODEL)), _full((1, D_MODEL))],
        out_specs=rows,
        out_shape=jax.ShapeDtypeStruct((n, D_MODEL), F32),
        compiler_params=_cparams("parallel"),
        name="combine_ln",
    )(x2d, *ys, gates, ln_g.reshape(1, D_MODEL), ln_b.reshape(1, D_MODEL))


def _block_diag(w):
    h, d, _ = w.shape
    return jnp.einsum('hij,hg->higj', w.astype(F32), jnp.eye(h, dtype=F32)).reshape(h * d, h * d)


def _layer(x, state, p, attn_bias):
    bn, seq, _ = x.shape
    n = bn * seq
    conv_buf, h0, s0, k_cache, v_cache = state
    x2d = x.reshape(n, D_MODEL)
    u = in_proj(x2d, p['w_in']).reshape(bn, seq, IN_COLS)

    y_a, new_buf, h_last = rglru(u, conv_buf, h0, p['conv_w'], p['conv_b'], p['w_r'], p['b_r'], p['w_i'], p['b_i'],
                                 p['lam'])
    q_col, k_col, v_col = OFF_Q // LANE, OFF_K // LANE, OFF_V // LANE
    if k_cache is None:
        qb = min(BAND_PAST, seq)
        y_b = attention(u, q_col, u, k_col, u, v_col, u, k_col, v_col, attn_bias, qb=qb, pb=qb,
                        prev_is_same_array=True)
    else:
        y_b = attention(u, q_col, k_cache, 0, v_cache, 0, u, k_col, v_col, attn_bias, qb=seq,
                        pb=k_cache.shape[1], prev_is_same_array=False)
    y_c, s_last = s5(u, s0, p['bmat'], p['scan_tables'], p['cmat'], p['d_c'], p['w_glu'], p['b_glu'])

    x1, x1h, top_i, gates, pos, counts = out_proj(
        y_a.reshape(n, D_A), y_b.reshape(n, D_B), y_c.reshape(n, D_C), x2d, p['g_a'], p['g_b'], p['g_c'],
        p['w_out'], p['ln1_g'], p['ln1_b'], p['w_router'], p['b_router'])
    dest, row_tok, blk_e, n_used = route(top_i, pos, counts.reshape(N_EXPERTS))
    yb = moe_experts(x1h[row_tok], blk_e, n_used, p['layer'], p['w_gu'], p['b_gu'], p['w_dn'], p['b_dn'])
    x2 = combine_ln(x1, [yb[dest[k]] for k in range(TOP_K)], gates, p['ln2_g'], p['ln2_b'])
    x2 = x2.reshape(bn, seq, D_MODEL)

    k_rows = u[:, :, OFF_K:OFF_V]
    v_rows = u[:, :, OFF_V:OFF_C]
    if k_cache is None:
        keep = min(BAND_PAST, seq)
        k_rows = k_rows[:, seq - keep:]
        v_rows = v_rows[:, seq - keep:]
    k_rows = k_rows.reshape(bn, -1, B_HEADS, B_HEAD_DIM)
    v_rows = v_rows.reshape(bn, -1, B_HEADS, B_HEAD_DIM)
    s_re = s_last[:, 0, :C_LANES].reshape(bn, C_GROUPS, C_STATE)
    s_im = s_last[:, 0, C_LANES:].reshape(bn, C_GROUPS, C_STATE)
    return x2, (new_buf, h_last.reshape(bn, D_A), k_rows, v_rows, s_re, s_im)


def kernel(x_prompt, x_sample, cache_conv_a, state_h_a, cache_k_b, cache_v_b, state_s_re_c, state_s_im_c, w_in, conv_w_a, conv_b_a, w_r_a, b_r_a, w_i_a, b_i_a, lambda_a, rel_bias_b, a_re_c, a_im_c, log_dt_c, b_re_c, b_im_c, c_re_c, c_im_c, d_c, w_glu_c, b_glu_c, g_norm_a, g_norm_b, g_norm_c, w_out, ln1_g, ln1_b, w_router, b_router, w_gu, b_gu, w_dn, b_dn, ln2_g, ln2_b):
    bp, seq_p, _ = x_prompt.shape
    bs, seq_s, _ = x_sample.shape
    kv_rows = cache_k_b.shape[2]
    yp, ys = x_prompt, x_sample
    p_states = [[] for _ in range(6)]
    s_states = [[] for _ in range(6)]
    for l in range(DEPTH):
        bmat, scan_tables, cmat = s5_params(a_re_c[l], a_im_c[l], log_dt_c[l], b_re_c[l], b_im_c[l], c_re_c[l],
                                           c_im_c[l])
        p = dict(
            w_in=w_in[l].astype(BF16), conv_w=conv_w_a[l], conv_b=conv_b_a[l],
            w_r=_block_diag(w_r_a[l]).astype(BF16), b_r=b_r_a[l], w_i=_block_diag(w_i_a[l]).astype(BF16),
            b_i=b_i_a[l], lam=lambda_a[l],
            bmat=bmat, scan_tables=scan_tables, cmat=cmat, d_c=d_c[l], w_glu=w_glu_c[l].astype(BF16), b_glu=b_glu_c[l],
            g_a=g_norm_a[l], g_b=g_norm_b[l], g_c=g_norm_c[l], w_out=w_out[l].astype(BF16),
            ln1_g=ln1_g[l], ln1_b=ln1_b[l], w_router=w_router[l], b_router=b_router[l],
            layer=l, w_gu=w_gu, b_gu=b_gu, w_dn=w_dn, b_dn=b_dn,
            ln2_g=ln2_g[l], ln2_b=ln2_b[l],
        )
        zero_state = (jnp.zeros((bp, CONV_W - 1, D_A), F32), jnp.zeros((bp, D_A), F32),
                      jnp.zeros((bp, 1, 2 * C_LANES), F32), None, None)
        yp, st = _layer(yp, zero_state, p, prompt_bias(rel_bias_b[l], min(BAND_PAST, seq_p)))
        for lst, s in zip(p_states, st):
            lst.append(s)
        s0 = jnp.concatenate([state_s_re_c[l].reshape(bs, 1, C_LANES), state_s_im_c[l].reshape(bs, 1, C_LANES)],
                             axis=-1)
        sample_state = (cache_conv_a[l], state_h_a[l], s0,
                        cache_k_b[l].reshape(bs, kv_rows, D_B), cache_v_b[l].reshape(bs, kv_rows, D_B))
        ys, st = _layer(ys, sample_state, p, sample_bias(rel_bias_b[l], seq_s, kv_rows))
        for lst, s in zip(s_states, st):
            lst.append(s)
    return (yp, ys) + tuple(jnp.stack(s) for s in p_states) + tuple(jnp.stack(s) for s in s_states)
```

```python
import functools
import math

import jax
import jax.numpy as jnp
import numpy as np
from jax import lax
from jax.experimental import pallas as pl
from jax.experimental.pallas import tpu as pltpu

F32 = jnp.float32
BF16 = jnp.bfloat16

D_MODEL = 1024
DEPTH = 2
CHUNK = 64
PREV_CHUNKS = 8
BAND_PAST = PREV_CHUNKS * CHUNK
D_A = D_MODEL // 4
D_B = D_MODEL // 2
D_C = D_MODEL // 4
A_HEADS = 4
A_HEAD_DIM = D_A // A_HEADS
CONV_W = 4
RG_C = 8.0
B_HEADS = 8
B_HEAD_DIM = D_B // B_HEADS
REL_CLIP = 128
C_GW = 16
C_GROUPS = D_C // C_GW
C_STATE = 64
C_LANES = C_GROUPS * C_STATE
OFF_GA = D_A
OFF_Q = 2 * D_A
OFF_K = OFF_Q + D_B
OFF_V = OFF_K + D_B
OFF_C = OFF_V + D_B
IN_COLS = OFF_C + D_C
N_EXPERTS = 32
TOP_K = 4
D_FF = D_MODEL
SWIGLU_LIMIT = 7.0
SWIGLU_ALPHA = 1.702
DEEPNORM_ALPHA = (2 * DEPTH) ** 0.25
LN_EPS = 1e-5
RMS_EPS = 1e-6
NEG_INF = -1e30

LANE = 128
SUBLANES = 8
HEADS_PER_SLAB = LANE // B_HEAD_DIM
VMEM_LIMIT = 56 * 1024 * 1024

ROW_TILE = 512
MOE_ROWS = 512
ATTN_SUB_ROWS = 256


def _cparams(*sem):
    return pltpu.CompilerParams(dimension_semantics=sem, vmem_limit_bytes=VMEM_LIMIT)


def _full(shape):
    return pl.BlockSpec(shape, lambda *_: (0,) * len(shape))


def _in_proj_kernel(x_ref, w_ref, o_ref):
    o_ref[...] = jnp.dot(x_ref[...].astype(BF16), w_ref[...], preferred_element_type=F32)


def in_proj(x2d, w_bf16):
    n = x2d.shape[0]
    tm = min(ROW_TILE, n)
    return pl.pallas_call(
        _in_proj_kernel,
        grid=(n // tm,),
        in_specs=[pl.BlockSpec((tm, D_MODEL), lambda i: (i, 0)), _full((D_MODEL, IN_COLS))],
        out_specs=pl.BlockSpec((tm, IN_COLS), lambda i: (i, 0)),
        out_shape=jax.ShapeDtypeStruct((n, IN_COLS), F32),
        compiler_params=_cparams("parallel"),
        name="in_proj",
    )(x2d, w_bf16)


def _rglru_kernel(xa_ref, ga_ref, buf_ref, h0_ref, cw_ref, cb_ref, wr_ref, br_ref, wi_ref, bi_ref, lam_ref,
                  y_ref, nbuf_ref, hl_ref, xp_ref, hc_ref, as_ref, hs_ref, *, t):
    pad = SUBLANES
    hist = CONV_W - 1

    @pl.when(pl.program_id(1) == 0)
    def _():
        xp_ref[0:pad, :] = jnp.zeros((pad, D_A), F32)
        xp_ref[pad - hist:pad, :] = buf_ref[0]
        hc_ref[...] = h0_ref[0]

    xa = xa_ref[0]
    xp_ref[pad:pad + t, :] = xa
    xc = cb_ref[...] + xa * cw_ref[hist:hist + 1, :]
    for j in range(hist):
        xc = xc + xp_ref[pad - hist + j:pad - hist + j + t, :] * cw_ref[j:j + 1, :]
    tail = xp_ref[pad + t - hist:pad + t, :]
    nbuf_ref[0] = tail
    xp_ref[pad - hist:pad, :] = tail

    xch = xc.astype(BF16)
    r = jax.nn.sigmoid(jnp.dot(xch, wr_ref[...], preferred_element_type=F32) + br_ref[...])
    i = jax.nn.sigmoid(jnp.dot(xch, wi_ref[...], preferred_element_type=F32) + bi_ref[...])
    lam = lam_ref[...]
    softplus_neg_lam = jnp.maximum(-lam, 0.0) + jnp.log(1.0 + jnp.exp(-jnp.abs(lam)))
    log_a = (-RG_C) * r * softplus_neg_lam
    a = jnp.exp(log_a)
    b = jnp.sqrt(1.0 - jnp.exp(2.0 * log_a)) * (i * xc)

    row_in_group = lax.broadcasted_iota(jnp.int32, (t, D_A), 0) & (SUBLANES - 1)
    acc_a, acc_b = a, b
    sh = 1
    while sh < SUBLANES:
        m = row_in_group >= sh
        acc_b = jnp.where(m, acc_a * pltpu.roll(acc_b, sh, 0) + acc_b, acc_b)
        acc_a = jnp.where(m, acc_a * pltpu.roll(acc_a, sh, 0), acc_a)
        sh *= 2
    as_ref[...] = acc_a
    hs_ref[...] = acc_b

    def group(g, carry):
        rows = pl.ds(pl.multiple_of(g * SUBLANES, SUBLANES), SUBLANES)
        h = hs_ref[rows, :] + as_ref[rows, :] * carry
        hs_ref[rows, :] = h
        return h[SUBLANES - 1:SUBLANES, :]

    h_last = lax.fori_loop(0, t // SUBLANES, group, hc_ref[...], unroll=8)
    hc_ref[...] = h_last
    hl_ref[0] = h_last
    y_ref[0] = hs_ref[...] * jax.nn.gelu(ga_ref[0])


def rglru(u, conv_buf, h0, cw, cb, wr_bd, br, wi_bd, bi, lam):
    bn, seq, _ = u.shape
    t = min(ROW_TILE, seq)
    vec = _full((1, D_A))
    return pl.pallas_call(
        functools.partial(_rglru_kernel, t=t),
        grid=(bn, seq // t),
        in_specs=[
            pl.BlockSpec((1, t, D_A), lambda b, s: (b, s, 0)),
            pl.BlockSpec((1, t, D_A), lambda b, s: (b, s, OFF_GA // D_A)),
            pl.BlockSpec((1, CONV_W - 1, D_A), lambda b, s: (b, 0, 0)),
            pl.BlockSpec((1, 1, D_A), lambda b, s: (b, 0, 0)),
            _full((CONV_W, D_A)), vec, _full((D_A, D_A)), vec, _full((D_A, D_A)), vec, vec,
        ],
        out_specs=[
            pl.BlockSpec((1, t, D_A), lambda b, s: (b, s, 0)),
            pl.BlockSpec((1, CONV_W - 1, D_A), lambda b, s: (b, 0, 0)),
            pl.BlockSpec((1, 1, D_A), lambda b, s: (b, 0, 0)),
        ],
        out_shape=[
            jax.ShapeDtypeStruct((bn, seq, D_A), F32),
            jax.ShapeDtypeStruct((bn, CONV_W - 1, D_A), F32),
            jax.ShapeDtypeStruct((bn, 1, D_A), F32),
        ],
        scratch_shapes=[pltpu.VMEM((t + SUBLANES, D_A), F32), pltpu.VMEM((1, D_A), F32),
                        pltpu.VMEM((t, D_A), F32), pltpu.VMEM((t, D_A), F32)],
        compiler_params=_cparams("parallel", "arbitrary"),
        name="rglru",
    )(u, u, conv_buf, h0.reshape(bn, 1, D_A), cw, cb.reshape(1, D_A), wr_bd, br.reshape(1, D_A),
      wi_bd, bi.reshape(1, D_A), lam.reshape(1, D_A))


def _attn_kernel(q_ref, kp_ref, kc_ref, vp_ref, vc_ref, bias_ref, o_ref, *, pb, sq, mask_first_prev):
    scale = B_HEAD_DIM ** -0.5
    qb = q_ref.shape[1]
    q_all = q_ref[0].astype(BF16)
    kp_all = kp_ref[0].astype(BF16)
    kc_all = kc_ref[0].astype(BF16)
    vp_all = vp_ref[0].astype(BF16)
    vc_all = vc_ref[0].astype(BF16)
    lane = lax.broadcasted_iota(jnp.int32, (1, LANE), 1)
    contract_last = (((1,), (1,)), ((), ()))
    for r in range(qb // sq):
        lo = max(pb + r * sq - BAND_PAST, 0)
        hi = (r + 1) * sq
        q = q_all[r * sq:hi, :]
        kp, vp = kp_all[lo:pb, :], vp_all[lo:pb, :]
        kc, vc = kc_all[0:hi, :], vc_all[0:hi, :]
        out = None
        for hh in range(HEADS_PER_SLAB):
            in_head = (lane // B_HEAD_DIM) == hh
            qh = jnp.where(in_head, q, jnp.zeros_like(q))
            sp = (lax.dot_general(qh, kp, contract_last, preferred_element_type=F32) * scale
                  + bias_ref[hh, r * sq:hi, lo:pb])
            sc = (lax.dot_general(qh, kc, contract_last, preferred_element_type=F32) * scale
                  + bias_ref[hh, r * sq:hi, pb:pb + hi])
            if mask_first_prev:
                sp = jnp.where(pl.program_id(2) == 0, NEG_INF, sp)
            m = jnp.maximum(jnp.max(sp, axis=-1, keepdims=True), jnp.max(sc, axis=-1, keepdims=True))
            ep = jnp.exp(sp - m)
            ec = jnp.exp(sc - m)
            denom = jnp.sum(ep, axis=-1, keepdims=True) + jnp.sum(ec, axis=-1, keepdims=True)
            o = (jnp.dot(ep.astype(BF16), vp, preferred_element_type=F32)
                 + jnp.dot(ec.astype(BF16), vc, preferred_element_type=F32)) / denom
            out = o if out is None else jnp.where(in_head, o, out)
        o_ref[0, r * sq:hi, :] = out


def attention(q_arr, q_col, kprev_arr, kprev_col, vprev_arr, vprev_col, kv_arr, k_col, v_col, bias, *, qb, pb,
              prev_is_same_array):
    bn, seq, _ = q_arr.shape
    n_slabs = B_HEADS // HEADS_PER_SLAB
    if prev_is_same_array:
        prev_map = lambda col: (lambda hp, b, s: (b, jnp.maximum(s - 1, 0), col + hp))
    else:
        prev_map = lambda col: (lambda hp, b, s: (b, 0, col + hp))
    cur_map = lambda col: (lambda hp, b, s: (b, s, col + hp))
    return pl.pallas_call(
        functools.partial(_attn_kernel, pb=pb, sq=min(qb, ATTN_SUB_ROWS), mask_first_prev=prev_is_same_array),
        grid=(n_slabs, bn, seq // qb),
        in_specs=[
            pl.BlockSpec((1, qb, LANE), cur_map(q_col)),
            pl.BlockSpec((1, pb, LANE), prev_map(kprev_col)),
            pl.BlockSpec((1, qb, LANE), cur_map(k_col)),
            pl.BlockSpec((1, pb, LANE), prev_map(vprev_col)),
            pl.BlockSpec((1, qb, LANE), cur_map(v_col)),
            pl.BlockSpec((HEADS_PER_SLAB, qb, pb + qb), lambda hp, b, s: (hp, 0, 0)),
        ],
        out_specs=pl.BlockSpec((1, qb, LANE), lambda hp, b, s: (b, s, hp)),
        out_shape=jax.ShapeDtypeStruct((bn, seq, D_B), F32),
        compiler_params=_cparams("arbitrary", "arbitrary", "arbitrary"),
        name="attention",
    )(q_arr, kprev_arr, kv_arr, vprev_arr, kv_arr, bias)


def _bias_kernel(v_ref, o_ref, *, band_mask):
    qb, width = o_ref.shape[1], o_ref.shape[2]
    period = v_ref.shape[2]
    table = pltpu.roll(jnp.broadcast_to(v_ref[0], (qb, period)), 0, 1, stride=1, stride_axis=0)[:, :width]
    if band_mask:
        i = lax.broadcasted_iota(jnp.int32, (qb, width), 0)
        j = lax.broadcasted_iota(jnp.int32, (qb, width), 1)
        dc = i // CHUNK - j // CHUNK + PREV_CHUNKS
        table = jnp.where((dc >= 0) & (dc <= PREV_CHUNKS), table, NEG_INF)
    o_ref[0] = table


def bias_table(rel_bias, qb, pb, band_mask):
    n_heads = rel_bias.shape[0]
    period = pb + 2 * qb
    m = np.arange(period)
    d = np.where(m < pb + qb, m, m - period)
    idx = np.clip(pb - d, -REL_CLIP, REL_CLIP) + REL_CLIP
    v = rel_bias.astype(F32)[:, idx].reshape(n_heads, 1, period)
    return pl.pallas_call(
        functools.partial(_bias_kernel, band_mask=band_mask),
        grid=(n_heads,),
        in_specs=[pl.BlockSpec((1, 1, period), lambda h: (h, 0, 0))],
        out_specs=pl.BlockSpec((1, qb, pb + qb), lambda h: (h, 0, 0)),
        out_shape=jax.ShapeDtypeStruct((n_heads, qb, pb + qb), F32),
        compiler_params=_cparams("parallel"),
        name="bias_table",
    )(v)


def _s5_kernel(u_ref, s0_ref, bmat_ref, ar_ref, ai_ref, cmat_ref, d_ref, wg_ref, bg_ref,
               y_ref, sl_ref, sc_ref, xs_ref, *, steps):
    @pl.when(pl.program_id(1) == 0)
    def _():
        sc_ref[...] = s0_ref[0]

    u = u_ref[0]
    xs_ref[...] = jnp.dot(u.astype(BF16), bmat_ref[...], preferred_element_type=F32)
    a_re = ar_ref[...]
    a_im = ai_ref[...]

    def step(t, s):
        rows = pl.ds(pl.multiple_of(t * SUBLANES, SUBLANES), SUBLANES)
        s = xs_ref[rows, :] + a_re * s + a_im * pltpu.roll(s, C_LANES, 1)
        xs_ref[rows, :] = s
        return s

    s_last = lax.fori_loop(0, steps, step, sc_ref[...], unroll=4)
    sc_ref[...] = s_last
    sl_ref[0] = s_last
    y = jnp.dot(xs_ref[...].astype(BF16), cmat_ref[...], preferred_element_type=F32) + d_ref[...] * u
    y = jax.nn.gelu(y)
    gate = jax.nn.sigmoid(jnp.dot(y.astype(BF16), wg_ref[...], preferred_element_type=F32) + bg_ref[...])
    y_ref[0] = y * gate


S5_STEPS = 64


def s5(u, s0, bmat, a_tables, cmat, d_skip, w_glu, b_glu):
    bn, seq, _ = u.shape
    groups = bn // SUBLANES
    steps = min(S5_STEPS, seq)
    rows = steps * SUBLANES
    uc = u[:, :, OFF_C:].reshape(groups, SUBLANES, seq, D_C).transpose(0, 2, 1, 3).reshape(groups, seq * SUBLANES, D_C)
    y_tm, s_last = pl.pallas_call(
        functools.partial(_s5_kernel, steps=steps),
        grid=(groups, seq // steps),
        in_specs=[
            pl.BlockSpec((1, rows, D_C), lambda g, s: (g, s, 0)),
            pl.BlockSpec((1, SUBLANES, 2 * C_LANES), lambda g, s: (g, 0, 0)),
            _full((D_C, 2 * C_LANES)), _full((SUBLANES, 2 * C_LANES)), _full((SUBLANES, 2 * C_LANES)),
            _full((2 * C_LANES, D_C)), _full((1, D_C)), _full((D_C, D_C)), _full((1, D_C)),
        ],
        out_specs=[
            pl.BlockSpec((1, rows, D_C), lambda g, s: (g, s, 0)),
            pl.BlockSpec((1, SUBLANES, 2 * C_LANES), lambda g, s: (g, 0, 0)),
        ],
        out_shape=[
            jax.ShapeDtypeStruct((groups, seq * SUBLANES, D_C), F32),
            jax.ShapeDtypeStruct((groups, SUBLANES, 2 * C_LANES), F32),
        ],
        scratch_shapes=[pltpu.VMEM((SUBLANES, 2 * C_LANES), F32), pltpu.VMEM((rows, 2 * C_LANES), F32)],
        compiler_params=_cparams("parallel", "arbitrary"),
        name="s5",
    )(uc, s0.reshape(groups, SUBLANES, 2 * C_LANES), bmat, *a_tables, cmat, d_skip.reshape(1, D_C), w_glu,
      b_glu.reshape(1, D_C))
    y = y_tm.reshape(groups, seq, SUBLANES, D_C).transpose(0, 2, 1, 3).reshape(bn, seq, D_C)
    return y, s_last.reshape(bn, 1, 2 * C_LANES)


def s5_params(a_re, a_im, log_dt, b_re, b_im, c_re, c_im):
    lam = lax.complex(a_re.astype(F32), a_im.astype(F32))
    dt = jnp.exp(log_dt.astype(F32))[:, None]
    a_bar = jnp.exp(lam * dt)
    b_bar = ((a_bar - 1.0) / lam)[:, :, None] * lax.complex(b_re.astype(F32), b_im.astype(F32))
    eye = jnp.eye(C_GROUPS, dtype=F32)

    def block_in(m):
        return jnp.einsum('gpi,gh->gihp', m, eye).reshape(D_C, C_LANES)

    def block_out(m):
        return jnp.einsum('gip,gh->gphi', m, eye).reshape(C_LANES, D_C)

    bmat = jnp.concatenate([block_in(jnp.real(b_bar)), block_in(jnp.imag(b_bar))], axis=1)
    cmat = jnp.concatenate([block_out(c_re.astype(F32)), -block_out(c_im.astype(F32))], axis=0)
    re = jnp.real(a_bar).reshape(1, C_LANES)
    im = jnp.imag(a_bar).reshape(1, C_LANES)
    a_re2 = jnp.broadcast_to(jnp.concatenate([re, re], axis=1), (SUBLANES, 2 * C_LANES))
    a_im2 = jnp.broadcast_to(jnp.concatenate([-im, im], axis=1), (SUBLANES, 2 * C_LANES))
    return bmat.astype(BF16), (a_re2, a_im2), cmat.astype(BF16)


def _rms(x, g):
    return x * lax.rsqrt(jnp.mean(jnp.square(x), axis=-1, keepdims=True) + RMS_EPS) * g


def _layer_norm(x, g, b):
    mu = jnp.mean(x, axis=-1, keepdims=True)
    xc = x - mu
    var = jnp.mean(jnp.square(xc), axis=-1, keepdims=True)
    return xc * lax.rsqrt(var + LN_EPS) * g + b


def _out_proj_kernel(ya_ref, yb_ref, yc_ref, x_ref, ga_ref, gb_ref, gc_ref, w_ref, lg_ref, lb_ref,
                     wrh_ref, wrl_ref, br_ref,
                     x1_ref, x1h_ref, topi_ref, gate_ref, pos_ref, cnt_ref, seen_ref):
    @pl.when(pl.program_id(0) == 0)
    def _():
        seen_ref[...] = jnp.zeros_like(seen_ref)

    m = jnp.dot(_rms(ya_ref[...], ga_ref[...]).astype(BF16), w_ref[0:D_A, :], preferred_element_type=F32)
    m += jnp.dot(_rms(yb_ref[...], gb_ref[...]).astype(BF16), w_ref[D_A:D_A + D_B, :], preferred_element_type=F32)
    m += jnp.dot(_rms(yc_ref[...], gc_ref[...]).astype(BF16), w_ref[D_A + D_B:, :], preferred_element_type=F32)
    x1 = _layer_norm(DEEPNORM_ALPHA * x_ref[...] + m, lg_ref[...], lb_ref[...])
    x1_ref[...] = x1
    x1h = x1.astype(BF16)
    x1h_ref[...] = x1h

    x1l = (x1 - x1h.astype(F32)).astype(BF16)
    logits = (jnp.dot(x1h, wrh_ref[...], preferred_element_type=F32)
              + jnp.dot(x1l, wrh_ref[...], preferred_element_type=F32)
              + jnp.dot(x1h, wrl_ref[...], preferred_element_type=F32)) + br_ref[...]

    tm = logits.shape[0]
    col = lax.broadcasted_iota(jnp.int32, (tm, N_EXPERTS), 1).astype(F32)
    work = logits
    sels, vals, idxs = [], [], []
    for _ in range(TOP_K):
        top = jnp.max(work, axis=1, keepdims=True)
        idx = jnp.min(jnp.where(work == top, col, float(N_EXPERTS)), axis=1, keepdims=True)
        sel = col == idx
        work = jnp.where(sel, -jnp.inf, work)
        sels.append(sel)
        vals.append(top)
        idxs.append(idx)
    exps = [jnp.exp(v - vals[0]) for v in vals]
    total = functools.reduce(lambda a, b: a + b, exps)

    chosen = functools.reduce(lambda a, b: a + b, [s.astype(F32) for s in sels])
    ri = lax.broadcasted_iota(jnp.int32, (tm, tm), 0)
    ci = lax.broadcasted_iota(jnp.int32, (tm, tm), 1)
    earlier = jnp.where(ci < ri, 1.0, 0.0).astype(BF16)
    before = jnp.dot(earlier, chosen.astype(BF16), preferred_element_type=F32) + seen_ref[...]
    seen = seen_ref[...] + jnp.sum(chosen, axis=0, keepdims=True)
    seen_ref[...] = seen
    cnt_ref[...] = seen.astype(jnp.int32)

    slot = lax.broadcasted_iota(jnp.int32, (tm, TOP_K), 1)
    top_i = jnp.zeros((tm, TOP_K), F32)
    gates = jnp.zeros((tm, TOP_K), F32)
    pos = jnp.zeros((tm, TOP_K), F32)
    for k in range(TOP_K):
        rank = jnp.sum(jnp.where(sels[k], before, 0.0), axis=1, keepdims=True)
        top_i = jnp.where(slot == k, idxs[k], top_i)
        gates = jnp.where(slot == k, exps[k] / total, gates)
        pos = jnp.where(slot == k, rank, pos)
    topi_ref[...] = top_i.astype(jnp.int32)
    gate_ref[...] = gates
    pos_ref[...] = pos.astype(jnp.int32)


def out_proj(ya, yb, yc, x2d, g_a, g_b, g_c, w_out_bf16, ln_g, ln_b, w_router, b_router):
    n = x2d.shape[0]
    tm = min(ROW_TILE, n)
    rows = lambda w: pl.BlockSpec((tm, w), lambda i: (i, 0))
    wr = w_router.astype(F32)
    wr_hi = wr.astype(BF16)
    wr_lo = (wr - wr_hi.astype(F32)).astype(BF16)
    return pl.pallas_call(
        _out_proj_kernel,
        grid=(n // tm,),
        in_specs=[rows(D_A), rows(D_B), rows(D_C), rows(D_MODEL),
                  _full((1, D_A)), _full((1, D_B)), _full((1, D_C)), _full((D_MODEL, D_MODEL)),
                  _full((1, D_MODEL)), _full((1, D_MODEL)),
                  _full((D_MODEL, N_EXPERTS)), _full((D_MODEL, N_EXPERTS)), _full((1, N_EXPERTS))],
        out_specs=[rows(D_MODEL), rows(D_MODEL), rows(TOP_K), rows(TOP_K), rows(TOP_K), _full((1, N_EXPERTS))],
        out_shape=[jax.ShapeDtypeStruct((n, D_MODEL), F32), jax.ShapeDtypeStruct((n, D_MODEL), BF16),
                   jax.ShapeDtypeStruct((n, TOP_K), jnp.int32), jax.ShapeDtypeStruct((n, TOP_K), F32),
                   jax.ShapeDtypeStruct((n, TOP_K), jnp.int32), jax.ShapeDtypeStruct((1, N_EXPERTS), jnp.int32)],
        scratch_shapes=[pltpu.VMEM((1, N_EXPERTS), F32)],
        compiler_params=_cparams("arbitrary"),
        name="out_proj",
    )(ya, yb, yc, x2d, g_a.reshape(1, D_A), g_b.reshape(1, D_B), g_c.reshape(1, D_C), w_out_bf16,
      ln_g.reshape(1, D_MODEL), ln_b.reshape(1, D_MODEL), wr_hi, wr_lo, b_router.reshape(1, N_EXPERTS))


CAST_ROWS = 128


def _cast_weight(src_ref, dst_ref):
    def body(c, carry):
        r = pl.multiple_of(c * CAST_ROWS, CAST_ROWS)
        dst_ref[pl.ds(r, CAST_ROWS), :] = src_ref[0, 0, pl.ds(r, CAST_ROWS), :].astype(BF16)
        return carry

    lax.fori_loop(0, dst_ref.shape[0] // CAST_ROWS, body, 0)


def _moe_kernel(blk_e_ref, n_used_ref, x_ref, wgu_ref, bgu_ref, wdn_ref, bdn_ref, o_ref, wgu_s, wdn_s):
    i = pl.program_id(0)
    expert_changed = (i == 0) | (blk_e_ref[i] != blk_e_ref[jnp.maximum(i - 1, 0)])

    @pl.when(expert_changed)
    def _():
        _cast_weight(wgu_ref, wgu_s)
        _cast_weight(wdn_ref, wdn_s)

    @pl.when(i < n_used_ref[0])
    def _():
        h = jnp.dot(x_ref[...], wgu_s[...], preferred_element_type=F32) + bgu_ref[0, 0]
        gate = jnp.minimum(h[:, :D_FF], SWIGLU_LIMIT)
        up = jnp.clip(h[:, D_FF:], -SWIGLU_LIMIT, SWIGLU_LIMIT)
        glu = gate * jax.nn.sigmoid(gate * SWIGLU_ALPHA)
        act = ((up + 1.0) * glu).astype(BF16)
        o_ref[...] = jnp.dot(act, wdn_s[...], preferred_element_type=F32) + bdn_ref[0, 0]

    @pl.when(i >= n_used_ref[0])
    def _():
        o_ref[...] = jnp.zeros_like(o_ref)


def moe_experts(xb, blk_e, n_used, layer, w_gu, b_gu, w_dn, b_dn):
    n_rows = xb.shape[0]
    n_blocks = n_rows // MOE_ROWS
    depth = w_gu.shape[0]
    grid_spec = pltpu.PrefetchScalarGridSpec(
        num_scalar_prefetch=2,
        grid=(n_blocks,),
        in_specs=[
            pl.BlockSpec((MOE_ROWS, D_MODEL), lambda i, be, nu: (i, 0)),
            pl.BlockSpec((1, 1, D_MODEL, 2 * D_FF), lambda i, be, nu: (layer, be[i], 0, 0)),
            pl.BlockSpec((1, 1, 1, 2 * D_FF), lambda i, be, nu: (layer, be[i], 0, 0)),
            pl.BlockSpec((1, 1, D_FF, D_MODEL), lambda i, be, nu: (layer, be[i], 0, 0)),
            pl.BlockSpec((1, 1, 1, D_MODEL), lambda i, be, nu: (layer, be[i], 0, 0)),
        ],
        out_specs=pl.BlockSpec((MOE_ROWS, D_MODEL), lambda i, be, nu: (i, 0)),
        scratch_shapes=[pltpu.VMEM((D_MODEL, 2 * D_FF), BF16), pltpu.VMEM((D_FF, D_MODEL), BF16)],
    )
    return pl.pallas_call(
        _moe_kernel,
        grid_spec=grid_spec,
        out_shape=jax.ShapeDtypeStruct((n_rows, D_MODEL), F32),
        compiler_params=_cparams("arbitrary"),
        name="moe_experts",
    )(blk_e, n_used, xb, w_gu, b_gu.reshape(depth, N_EXPERTS, 1, 2 * D_FF), w_dn,
      b_dn.reshape(depth, N_EXPERTS, 1, D_MODEL))


def route(top_i, pos, counts):
    n_tok = top_i.shape[0]
    n_slots = n_tok * TOP_K
    padded = ((counts + MOE_ROWS - 1) // MOE_ROWS) * MOE_ROWS
    pad_end = jnp.cumsum(padded)
    pad_start = pad_end - padded
    start = jnp.cumsum(counts) - counts
    experts = jnp.arange(N_EXPERTS, dtype=jnp.int32)
    slot_start = jnp.sum(jnp.where(top_i[:, :, None] == experts, pad_start, 0), axis=-1)
    dest = (slot_start + pos).astype(jnp.int32).T
    n_blocks = -(-n_slots // MOE_ROWS) + N_EXPERTS
    blk_first = jnp.arange(n_blocks, dtype=jnp.int32) * MOE_ROWS
    blk_e = jnp.minimum(jnp.sum((pad_end[None, :] <= blk_first[:, None]).astype(jnp.int32), axis=1), N_EXPERTS - 1)
    order = jnp.argsort(top_i.reshape(-1)).astype(jnp.int32)
    within = jnp.arange(MOE_ROWS, dtype=jnp.int32)[None, :] + (blk_first - pad_start[blk_e])[:, None]
    src = jnp.clip(start[blk_e][:, None] + within, 0, n_slots - 1).reshape(-1)
    valid = (within < counts[blk_e][:, None]).reshape(-1)
    row_tok = jnp.where(valid, order[src] // TOP_K, 0).astype(jnp.int32)
    n_used = (pad_end[-1:] // MOE_ROWS).astype(jnp.int32)
    return dest, row_tok, blk_e.astype(jnp.int32), n_used


COMBINE_ROWS = 256


def _combine_ln_kernel(dcur_ref, dnext_ref, x_ref, gate_ref, g_ref, b_ref, yb_ref, o_ref, buf_ref, sem_ref, *,
                       n_tiles):
    i = pl.program_id(0)
    tm = x_ref.shape[0]
    n_rows = TOP_K * tm
    slot = i % 2

    def row_copy(row, r, s):
        return pltpu.make_async_copy(yb_ref.at[pl.ds(row, 1), :], buf_ref.at[s, pl.ds(r, 1), :], sem_ref.at[s])

    def start_gather(d_ref, s):
        def body(r, carry):
            row_copy(d_ref[0, 0, r], r, s).start()
            return carry

        lax.fori_loop(0, n_rows, body, 0, unroll=8)

    @pl.when(i == 0)
    def _():
        start_gather(dcur_ref, 0)

    @pl.when(i + 1 < n_tiles)
    def _():
        start_gather(dnext_ref, 1 - slot)

    pltpu.make_async_copy(yb_ref.at[pl.ds(0, n_rows), :], buf_ref.at[slot], sem_ref.at[slot]).wait()
    gates = gate_ref[...]
    moe = buf_ref[slot, 0:tm, :] * gates[:, 0:1]
    for k in range(1, TOP_K):
        moe = moe + buf_ref[slot, k * tm:(k + 1) * tm, :] * gates[:, k:k + 1]
    o_ref[...] = _layer_norm(DEEPNORM_ALPHA * x_ref[...] + moe, g_ref[...], b_ref[...])


def combine_ln(x2d, yb, dest, gates, ln_g, ln_b):
    n = x2d.shape[0]
    tm = min(COMBINE_ROWS, n)
    n_tiles = n // tm
    rows = pl.BlockSpec((tm, D_MODEL), lambda i: (i, 0))
    dest_tiles = dest.reshape(TOP_K, n_tiles, tm).transpose(1, 0, 2).reshape(n_tiles, 1, TOP_K * tm)
    dest_spec = lambda index_map: pl.BlockSpec((1, 1, TOP_K * tm), index_map, memory_space=pltpu.SMEM)
    return pl.pallas_call(
        functools.partial(_combine_ln_kernel, n_tiles=n_tiles),
        grid=(n_tiles,),
        in_specs=[dest_spec(lambda i: (i, 0, 0)), dest_spec(lambda i: (jnp.minimum(i + 1, n_tiles - 1), 0, 0)),
                  rows, pl.BlockSpec((tm, TOP_K), lambda i: (i, 0)), _full((1, D_MODEL)), _full((1, D_MODEL)),
                  pl.BlockSpec(memory_space=pl.ANY)],
        out_specs=rows,
        out_shape=jax.ShapeDtypeStruct((n, D_MODEL), F32),
        scratch_shapes=[pltpu.VMEM((2, TOP_K * tm, D_MODEL), F32), pltpu.SemaphoreType.DMA((2,))],
        compiler_params=_cparams("arbitrary"),
        name="combine_ln",
    )(dest_tiles, dest_tiles, x2d, gates, ln_g.reshape(1, D_MODEL), ln_b.reshape(1, D_MODEL), yb)


def _block_diag(w):
    h, d, _ = w.shape
    return jnp.einsum('hij,hg->higj', w.astype(F32), jnp.eye(h, dtype=F32)).reshape(h * d, h * d)


def _mixer(x, state, p, attn_bias):
    bn, seq, _ = x.shape
    n = bn * seq
    conv_buf, h0, s0, k_cache, v_cache = state
    x2d = x.reshape(n, D_MODEL)
    u = in_proj(x2d, p['w_in']).reshape(bn, seq, IN_COLS)

    y_a, new_buf, h_last = rglru(u, conv_buf, h0, p['conv_w'], p['conv_b'], p['w_r'], p['b_r'], p['w_i'], p['b_i'],
                                 p['lam'])
    q_col, k_col, v_col = OFF_Q // LANE, OFF_K // LANE, OFF_V // LANE
    if k_cache is None:
        qb = min(BAND_PAST, seq)
        y_b = attention(u, q_col, u, k_col, u, v_col, u, k_col, v_col, attn_bias, qb=qb, pb=qb,
                        prev_is_same_array=True)
    else:
        y_b = attention(u, q_col, k_cache, 0, v_cache, 0, u, k_col, v_col, attn_bias, qb=seq,
                        pb=k_cache.shape[1], prev_is_same_array=False)
    y_c, s_last = s5(u, s0, p['bmat'], p['scan_tables'], p['cmat'], p['d_c'], p['w_glu'], p['b_glu'])

    x1, x1h, top_i, gates, pos, counts = out_proj(
        y_a.reshape(n, D_A), y_b.reshape(n, D_B), y_c.reshape(n, D_C), x2d, p['g_a'], p['g_b'], p['g_c'],
        p['w_out'], p['ln1_g'], p['ln1_b'], p['w_router'], p['b_router'])

    k_rows = u[:, :, OFF_K:OFF_V]
    v_rows = u[:, :, OFF_V:OFF_C]
    if k_cache is None:
        keep = min(BAND_PAST, seq)
        k_rows = k_rows[:, seq - keep:]
        v_rows = v_rows[:, seq - keep:]
    k_rows = k_rows.reshape(bn, -1, B_HEADS, B_HEAD_DIM)
    v_rows = v_rows.reshape(bn, -1, B_HEADS, B_HEAD_DIM)
    s_re = s_last[:, 0, :C_LANES].reshape(bn, C_GROUPS, C_STATE)
    s_im = s_last[:, 0, C_LANES:].reshape(bn, C_GROUPS, C_STATE)
    routed = dict(x1=x1, x1h=x1h, top_i=top_i, gates=gates, pos=pos, counts=counts.reshape(N_EXPERTS))
    return routed, (new_buf, h_last.reshape(bn, D_A), k_rows, v_rows, s_re, s_im)


def _moe(groups, p):
    experts = jnp.arange(N_EXPERTS, dtype=jnp.int32)
    seen = jnp.zeros((N_EXPERTS,), jnp.int32)
    pos_all = []
    for g in groups:
        pos_all.append(g['pos'] + jnp.sum(jnp.where(g['top_i'][:, :, None] == experts, seen, 0), axis=-1))
        seen = seen + g['counts']
    top_i = jnp.concatenate([g['top_i'] for g in groups], axis=0)
    dest, row_tok, blk_e, n_used = route(top_i, jnp.concatenate(pos_all, axis=0), seen)
    x1h = jnp.concatenate([g['x1h'] for g in groups], axis=0)
    yb = moe_experts(x1h[row_tok], blk_e, n_used, p['layer'], p['w_gu'], p['b_gu'], p['w_dn'], p['b_dn'])
    outs, first = [], 0
    for g in groups:
        n = g['x1'].shape[0]
        outs.append(combine_ln(g['x1'], yb, dest[:, first:first + n], g['gates'], p['ln2_g'], p['ln2_b']))
        first += n
    return outs


def kernel(x_prompt, x_sample, cache_conv_a, state_h_a, cache_k_b, cache_v_b, state_s_re_c, state_s_im_c, w_in, conv_w_a, conv_b_a, w_r_a, b_r_a, w_i_a, b_i_a, lambda_a, rel_bias_b, a_re_c, a_im_c, log_dt_c, b_re_c, b_im_c, c_re_c, c_im_c, d_c, w_glu_c, b_glu_c, g_norm_a, g_norm_b, g_norm_c, w_out, ln1_g, ln1_b, w_router, b_router, w_gu, b_gu, w_dn, b_dn, ln2_g, ln2_b):
    bp, seq_p, _ = x_prompt.shape
    bs, seq_s, _ = x_sample.shape
    kv_rows = cache_k_b.shape[2]
    yp, ys = x_prompt, x_sample
    p_states = [[] for _ in range(6)]
    s_states = [[] for _ in range(6)]
    for l in range(DEPTH):
        bmat, scan_tables, cmat = s5_params(a_re_c[l], a_im_c[l], log_dt_c[l], b_re_c[l], b_im_c[l], c_re_c[l],
                                           c_im_c[l])
        p = dict(
            w_in=w_in[l].astype(BF16), conv_w=conv_w_a[l], conv_b=conv_b_a[l],
            w_r=_block_diag(w_r_a[l]).astype(BF16), b_r=b_r_a[l], w_i=_block_diag(w_i_a[l]).astype(BF16),
            b_i=b_i_a[l], lam=lambda_a[l],
            bmat=bmat, scan_tables=scan_tables, cmat=cmat, d_c=d_c[l], w_glu=w_glu_c[l].astype(BF16), b_glu=b_glu_c[l],
            g_a=g_norm_a[l], g_b=g_norm_b[l], g_c=g_norm_c[l], w_out=w_out[l].astype(BF16),
            ln1_g=ln1_g[l], ln1_b=ln1_b[l], w_router=w_router[l], b_router=b_router[l],
            layer=l, w_gu=w_gu, b_gu=b_gu, w_dn=w_dn, b_dn=b_dn,
            ln2_g=ln2_g[l], ln2_b=ln2_b[l],
        )
        zero_state = (jnp.zeros((bp, CONV_W - 1, D_A), F32), jnp.zeros((bp, D_A), F32),
                      jnp.zeros((bp, 1, 2 * C_LANES), F32), None, None)
        qb = min(BAND_PAST, seq_p)
        routed_p, st = _mixer(yp, zero_state, p, bias_table(rel_bias_b[l], qb, qb, band_mask=True))
        for lst, s in zip(p_states, st):
            lst.append(s)
        s0 = jnp.concatenate([state_s_re_c[l].reshape(bs, 1, C_LANES), state_s_im_c[l].reshape(bs, 1, C_LANES)],
                             axis=-1)
        sample_state = (cache_conv_a[l], state_h_a[l], s0,
                        cache_k_b[l].reshape(bs, kv_rows, D_B), cache_v_b[l].reshape(bs, kv_rows, D_B))
        routed_s, st = _mixer(ys, sample_state, p, bias_table(rel_bias_b[l], seq_s, kv_rows, band_mask=False))
        for lst, s in zip(s_states, st):
            lst.append(s)
        yp, ys = _moe([routed_p, routed_s], p)
        yp = yp.reshape(bp, seq_p, D_MODEL)
        ys = ys.reshape(bs, seq_s, D_MODEL)
    return (yp, ys) + tuple(jnp.stack(s) for s in p_states) + tuple(jnp.stack(s) for s in s_states)
```

```python
import functools
import math

import jax
import jax.numpy as jnp
import numpy as np
from jax import lax
from jax.experimental import pallas as pl
from jax.experimental.pallas import tpu as pltpu

F32 = jnp.float32
BF16 = jnp.bfloat16

D_MODEL = 1024
DEPTH = 2
CHUNK = 64
PREV_CHUNKS = 8
BAND_PAST = PREV_CHUNKS * CHUNK
D_A = D_MODEL // 4
D_B = D_MODEL // 2
D_C = D_MODEL // 4
A_HEADS = 4
A_HEAD_DIM = D_A // A_HEADS
CONV_W = 4
RG_C = 8.0
B_HEADS = 8
B_HEAD_DIM = D_B // B_HEADS
REL_CLIP = 128
C_GW = 16
C_GROUPS = D_C // C_GW
C_STATE = 64
C_LANES = C_GROUPS * C_STATE
OFF_GA = D_A
OFF_Q = 2 * D_A
OFF_K = OFF_Q + D_B
OFF_V = OFF_K + D_B
OFF_C = OFF_V + D_B
IN_COLS = OFF_C + D_C
N_EXPERTS = 32
TOP_K = 4
D_FF = D_MODEL
SWIGLU_LIMIT = 7.0
SWIGLU_ALPHA = 1.702
DEEPNORM_ALPHA = (2 * DEPTH) ** 0.25
LN_EPS = 1e-5
RMS_EPS = 1e-6
NEG_INF = -1e30

LANE = 128
SUBLANES = 8
HEADS_PER_SLAB = LANE // B_HEAD_DIM
VMEM_LIMIT = 56 * 1024 * 1024

ROW_TILE = 512
MOE_ROWS = 512
MOE_CHUNKS = 4
ATTN_SUB_ROWS = 256


def _cparams(*sem):
    return pltpu.CompilerParams(dimension_semantics=sem, vmem_limit_bytes=VMEM_LIMIT)


def _full(shape):
    return pl.BlockSpec(shape, lambda *_: (0,) * len(shape))


def _in_proj_kernel(x_ref, w_ref, o_ref):
    o_ref[...] = jnp.dot(x_ref[...].astype(BF16), w_ref[...], preferred_element_type=F32)


def in_proj(x2d, w_bf16):
    n = x2d.shape[0]
    tm = min(ROW_TILE, n)
    return pl.pallas_call(
        _in_proj_kernel,
        grid=(n // tm,),
        in_specs=[pl.BlockSpec((tm, D_MODEL), lambda i: (i, 0)), _full((D_MODEL, IN_COLS))],
        out_specs=pl.BlockSpec((tm, IN_COLS), lambda i: (i, 0)),
        out_shape=jax.ShapeDtypeStruct((n, IN_COLS), F32),
        compiler_params=_cparams("parallel"),
        name="in_proj",
    )(x2d, w_bf16)


def _rglru_kernel(xa_ref, ga_ref, buf_ref, h0_ref, cw_ref, cb_ref, wr_ref, br_ref, wi_ref, bi_ref, lam_ref,
                  y_ref, nbuf_ref, hl_ref, xp_ref, hc_ref, as_ref, hs_ref, *, t):
    pad = SUBLANES
    hist = CONV_W - 1

    @pl.when(pl.program_id(1) == 0)
    def _():
        xp_ref[0:pad, :] = jnp.zeros((pad, D_A), F32)
        xp_ref[pad - hist:pad, :] = buf_ref[0]
        hc_ref[...] = h0_ref[0]

    xa = xa_ref[0]
    xp_ref[pad:pad + t, :] = xa
    xc = cb_ref[...] + xa * cw_ref[hist:hist + 1, :]
    for j in range(hist):
        xc = xc + xp_ref[pad - hist + j:pad - hist + j + t, :] * cw_ref[j:j + 1, :]
    tail = xp_ref[pad + t - hist:pad + t, :]
    nbuf_ref[0] = tail
    xp_ref[pad - hist:pad, :] = tail

    xch = xc.astype(BF16)
    r = jax.nn.sigmoid(jnp.dot(xch, wr_ref[...], preferred_element_type=F32) + br_ref[...])
    i = jax.nn.sigmoid(jnp.dot(xch, wi_ref[...], preferred_element_type=F32) + bi_ref[...])
    lam = lam_ref[...]
    softplus_neg_lam = jnp.maximum(-lam, 0.0) + jnp.log(1.0 + jnp.exp(-jnp.abs(lam)))
    log_a = (-RG_C) * r * softplus_neg_lam
    a = jnp.exp(log_a)
    b = jnp.sqrt(1.0 - jnp.exp(2.0 * log_a)) * (i * xc)

    row_in_group = lax.broadcasted_iota(jnp.int32, (t, D_A), 0) & (SUBLANES - 1)
    acc_a, acc_b = a, b
    sh = 1
    while sh < SUBLANES:
        m = row_in_group >= sh
        acc_b = jnp.where(m, acc_a * pltpu.roll(acc_b, sh, 0) + acc_b, acc_b)
        acc_a = jnp.where(m, acc_a * pltpu.roll(acc_a, sh, 0), acc_a)
        sh *= 2
    as_ref[...] = acc_a
    hs_ref[...] = acc_b

    def group(g, carry):
        rows = pl.ds(pl.multiple_of(g * SUBLANES, SUBLANES), SUBLANES)
        h = hs_ref[rows, :] + as_ref[rows, :] * carry
        hs_ref[rows, :] = h
        return h[SUBLANES - 1:SUBLANES, :]

    h_last = lax.fori_loop(0, t // SUBLANES, group, hc_ref[...], unroll=8)
    hc_ref[...] = h_last
    hl_ref[0] = h_last
    y_ref[0] = hs_ref[...] * jax.nn.gelu(ga_ref[0])


def rglru(u, conv_buf, h0, cw, cb, wr_bd, br, wi_bd, bi, lam):
    bn, seq, _ = u.shape
    t = min(ROW_TILE, seq)
    vec = _full((1, D_A))
    return pl.pallas_call(
        functools.partial(_rglru_kernel, t=t),
        grid=(bn, seq // t),
        in_specs=[
            pl.BlockSpec((1, t, D_A), lambda b, s: (b, s, 0)),
            pl.BlockSpec((1, t, D_A), lambda b, s: (b, s, OFF_GA // D_A)),
            pl.BlockSpec((1, CONV_W - 1, D_A), lambda b, s: (b, 0, 0)),
            pl.BlockSpec((1, 1, D_A), lambda b, s: (b, 0, 0)),
            _full((CONV_W, D_A)), vec, _full((D_A, D_A)), vec, _full((D_A, D_A)), vec, vec,
        ],
        out_specs=[
            pl.BlockSpec((1, t, D_A), lambda b, s: (b, s, 0)),
            pl.BlockSpec((1, CONV_W - 1, D_A), lambda b, s: (b, 0, 0)),
            pl.BlockSpec((1, 1, D_A), lambda b, s: (b, 0, 0)),
        ],
        out_shape=[
            jax.ShapeDtypeStruct((bn, seq, D_A), F32),
            jax.ShapeDtypeStruct((bn, CONV_W - 1, D_A), F32),
            jax.ShapeDtypeStruct((bn, 1, D_A), F32),
        ],
        scratch_shapes=[pltpu.VMEM((t + SUBLANES, D_A), F32), pltpu.VMEM((1, D_A), F32),
                        pltpu.VMEM((t, D_A), F32), pltpu.VMEM((t, D_A), F32)],
        compiler_params=_cparams("parallel", "arbitrary"),
        name="rglru",
    )(u, u, conv_buf, h0.reshape(bn, 1, D_A), cw, cb.reshape(1, D_A), wr_bd, br.reshape(1, D_A),
      wi_bd, bi.reshape(1, D_A), lam.reshape(1, D_A))


def _attn_kernel(q_ref, kp_ref, kc_ref, vp_ref, vc_ref, bias_ref, o_ref, *, pb, sq, mask_first_prev):
    scale = B_HEAD_DIM ** -0.5
    qb = q_ref.shape[1]
    q_all = q_ref[0].astype(BF16)
    kp_all = kp_ref[0].astype(BF16)
    kc_all = kc_ref[0].astype(BF16)
    vp_all = vp_ref[0].astype(BF16)
    vc_all = vc_ref[0].astype(BF16)
    lane = lax.broadcasted_iota(jnp.int32, (1, LANE), 1)
    contract_last = (((1,), (1,)), ((), ()))
    for r in range(qb // sq):
        lo = max(pb + r * sq - BAND_PAST, 0)
        hi = (r + 1) * sq
        q = q_all[r * sq:hi, :]
        kp, vp = kp_all[lo:pb, :], vp_all[lo:pb, :]
        kc, vc = kc_all[0:hi, :], vc_all[0:hi, :]
        out = None
        for hh in range(HEADS_PER_SLAB):
            in_head = (lane // B_HEAD_DIM) == hh
            qh = jnp.where(in_head, q, jnp.zeros_like(q))
            sp = (lax.dot_general(qh, kp, contract_last, preferred_element_type=F32) * scale
                  + bias_ref[hh, r * sq:hi, lo:pb])
            sc = (lax.dot_general(qh, kc, contract_last, preferred_element_type=F32) * scale
                  + bias_ref[hh, r * sq:hi, pb:pb + hi])
            if mask_first_prev:
                sp = jnp.where(pl.program_id(2) == 0, NEG_INF, sp)
            m = jnp.maximum(jnp.max(sp, axis=-1, keepdims=True), jnp.max(sc, axis=-1, keepdims=True))
            ep = jnp.exp(sp - m)
            ec = jnp.exp(sc - m)
            denom = jnp.sum(ep, axis=-1, keepdims=True) + jnp.sum(ec, axis=-1, keepdims=True)
            o = (jnp.dot(ep.astype(BF16), vp, preferred_element_type=F32)
                 + jnp.dot(ec.astype(BF16), vc, preferred_element_type=F32)) / denom
            out = o if out is None else jnp.where(in_head, o, out)
        o_ref[0, r * sq:hi, :] = out


def attention(q_arr, q_col, kprev_arr, kprev_col, vprev_arr, vprev_col, kv_arr, k_col, v_col, bias, *, qb, pb,
              prev_is_same_array):
    bn, seq, _ = q_arr.shape
    n_slabs = B_HEADS // HEADS_PER_SLAB
    if prev_is_same_array:
        prev_map = lambda col: (lambda hp, b, s: (b, jnp.maximum(s - 1, 0), col + hp))
    else:
        prev_map = lambda col: (lambda hp, b, s: (b, 0, col + hp))
    cur_map = lambda col: (lambda hp, b, s: (b, s, col + hp))
    return pl.pallas_call(
        functools.partial(_attn_kernel, pb=pb, sq=min(qb, ATTN_SUB_ROWS), mask_first_prev=prev_is_same_array),
        grid=(n_slabs, bn, seq // qb),
        in_specs=[
            pl.BlockSpec((1, qb, LANE), cur_map(q_col)),
            pl.BlockSpec((1, pb, LANE), prev_map(kprev_col)),
            pl.BlockSpec((1, qb, LANE), cur_map(k_col)),
            pl.BlockSpec((1, pb, LANE), prev_map(vprev_col)),
            pl.BlockSpec((1, qb, LANE), cur_map(v_col)),
            pl.BlockSpec((HEADS_PER_SLAB, qb, pb + qb), lambda hp, b, s: (hp, 0, 0)),
        ],
        out_specs=pl.BlockSpec((1, qb, LANE), lambda hp, b, s: (b, s, hp)),
        out_shape=jax.ShapeDtypeStruct((bn, seq, D_B), F32),
        compiler_params=_cparams("arbitrary", "arbitrary", "arbitrary"),
        name="attention",
    )(q_arr, kprev_arr, kv_arr, vprev_arr, kv_arr, bias)


def _bias_kernel(v_ref, o_ref, *, band_mask):
    qb, width = o_ref.shape[1], o_ref.shape[2]
    period = v_ref.shape[2]
    table = pltpu.roll(jnp.broadcast_to(v_ref[0], (qb, period)), 0, 1, stride=1, stride_axis=0)[:, :width]
    if band_mask:
        i = lax.broadcasted_iota(jnp.int32, (qb, width), 0)
        j = lax.broadcasted_iota(jnp.int32, (qb, width), 1)
        dc = i // CHUNK - j // CHUNK + PREV_CHUNKS
        table = jnp.where((dc >= 0) & (dc <= PREV_CHUNKS), table, NEG_INF)
    o_ref[0] = table


def bias_table(rel_bias, qb, pb, band_mask):
    n_heads = rel_bias.shape[0]
    period = pb + 2 * qb
    m = np.arange(period)
    d = np.where(m < pb + qb, m, m - period)
    idx = np.clip(pb - d, -REL_CLIP, REL_CLIP) + REL_CLIP
    v = rel_bias.astype(F32)[:, idx].reshape(n_heads, 1, period)
    return pl.pallas_call(
        functools.partial(_bias_kernel, band_mask=band_mask),
        grid=(n_heads,),
        in_specs=[pl.BlockSpec((1, 1, period), lambda h: (h, 0, 0))],
        out_specs=pl.BlockSpec((1, qb, pb + qb), lambda h: (h, 0, 0)),
        out_shape=jax.ShapeDtypeStruct((n_heads, qb, pb + qb), F32),
        compiler_params=_cparams("parallel"),
        name="bias_table",
    )(v)


def _s5_kernel(u_ref, s0_ref, bmat_ref, ar_ref, ai_ref, cmat_ref, d_ref, wg_ref, bg_ref,
               y_ref, sl_ref, sc_ref, xs_ref, *, steps):
    @pl.when(pl.program_id(1) == 0)
    def _():
        sc_ref[...] = s0_ref[0]

    u = u_ref[0]
    xs_ref[...] = jnp.dot(u.astype(BF16), bmat_ref[...], preferred_element_type=F32)
    a_re = ar_ref[...]
    a_im = ai_ref[...]

    def step(t, s):
        rows = pl.ds(pl.multiple_of(t * SUBLANES, SUBLANES), SUBLANES)
        s = xs_ref[rows, :] + a_re * s + a_im * pltpu.roll(s, C_LANES, 1)
        xs_ref[rows, :] = s
        return s

    s_last = lax.fori_loop(0, steps, step, sc_ref[...], unroll=4)
    sc_ref[...] = s_last
    sl_ref[0] = s_last
    y = jnp.dot(xs_ref[...].astype(BF16), cmat_ref[...], preferred_element_type=F32) + d_ref[...] * u
    y = jax.nn.gelu(y)
    gate = jax.nn.sigmoid(jnp.dot(y.astype(BF16), wg_ref[...], preferred_element_type=F32) + bg_ref[...])
    y_ref[0] = y * gate


S5_STEPS = 64


def s5(u, s0, bmat, a_tables, cmat, d_skip, w_glu, b_glu):
    bn, seq, _ = u.shape
    groups = bn // SUBLANES
    steps = min(S5_STEPS, seq)
    rows = steps * SUBLANES
    uc = u[:, :, OFF_C:].reshape(groups, SUBLANES, seq, D_C).transpose(0, 2, 1, 3).reshape(groups, seq * SUBLANES, D_C)
    y_tm, s_last = pl.pallas_call(
        functools.partial(_s5_kernel, steps=steps),
        grid=(groups, seq // steps),
        in_specs=[
            pl.BlockSpec((1, rows, D_C), lambda g, s: (g, s, 0)),
            pl.BlockSpec((1, SUBLANES, 2 * C_LANES), lambda g, s: (g, 0, 0)),
            _full((D_C, 2 * C_LANES)), _full((SUBLANES, 2 * C_LANES)), _full((SUBLANES, 2 * C_LANES)),
            _full((2 * C_LANES, D_C)), _full((1, D_C)), _full((D_C, D_C)), _full((1, D_C)),
        ],
        out_specs=[
            pl.BlockSpec((1, rows, D_C), lambda g, s: (g, s, 0)),
            pl.BlockSpec((1, SUBLANES, 2 * C_LANES), lambda g, s: (g, 0, 0)),
        ],
        out_shape=[
            jax.ShapeDtypeStruct((groups, seq * SUBLANES, D_C), F32),
            jax.ShapeDtypeStruct((groups, SUBLANES, 2 * C_LANES), F32),
        ],
        scratch_shapes=[pltpu.VMEM((SUBLANES, 2 * C_LANES), F32), pltpu.VMEM((rows, 2 * C_LANES), F32)],
        compiler_params=_cparams("parallel", "arbitrary"),
        name="s5",
    )(uc, s0.reshape(groups, SUBLANES, 2 * C_LANES), bmat, *a_tables, cmat, d_skip.reshape(1, D_C), w_glu,
      b_glu.reshape(1, D_C))
    y = y_tm.reshape(groups, seq, SUBLANES, D_C).transpose(0, 2, 1, 3).reshape(bn, seq, D_C)
    return y, s_last.reshape(bn, 1, 2 * C_LANES)


def s5_params(a_re, a_im, log_dt, b_re, b_im, c_re, c_im):
    lam = lax.complex(a_re.astype(F32), a_im.astype(F32))
    dt = jnp.exp(log_dt.astype(F32))[:, None]
    a_bar = jnp.exp(lam * dt)
    b_bar = ((a_bar - 1.0) / lam)[:, :, None] * lax.complex(b_re.astype(F32), b_im.astype(F32))
    eye = jnp.eye(C_GROUPS, dtype=F32)

    def block_in(m):
        return jnp.einsum('gpi,gh->gihp', m, eye).reshape(D_C, C_LANES)

    def block_out(m):
        return jnp.einsum('gip,gh->gphi', m, eye).reshape(C_LANES, D_C)

    bmat = jnp.concatenate([block_in(jnp.real(b_bar)), block_in(jnp.imag(b_bar))], axis=1)
    cmat = jnp.concatenate([block_out(c_re.astype(F32)), -block_out(c_im.astype(F32))], axis=0)
    re = jnp.real(a_bar).reshape(1, C_LANES)
    im = jnp.imag(a_bar).reshape(1, C_LANES)
    a_re2 = jnp.broadcast_to(jnp.concatenate([re, re], axis=1), (SUBLANES, 2 * C_LANES))
    a_im2 = jnp.broadcast_to(jnp.concatenate([-im, im], axis=1), (SUBLANES, 2 * C_LANES))
    return bmat.astype(BF16), (a_re2, a_im2), cmat.astype(BF16)


def _rms(x, g):
    return x * lax.rsqrt(jnp.mean(jnp.square(x), axis=-1, keepdims=True) + RMS_EPS) * g


def _layer_norm(x, g, b):
    mu = jnp.mean(x, axis=-1, keepdims=True)
    xc = x - mu
    var = jnp.mean(jnp.square(xc), axis=-1, keepdims=True)
    return xc * lax.rsqrt(var + LN_EPS) * g + b


def _out_proj_kernel(ya_ref, yb_ref, yc_ref, x_ref, ga_ref, gb_ref, gc_ref, w_ref, lg_ref, lb_ref,
                     wrh_ref, wrl_ref, br_ref,
                     x1_ref, topi_ref, gate_ref, pos_ref, cnt_ref, seen_ref):
    @pl.when(pl.program_id(0) == 0)
    def _():
        seen_ref[...] = jnp.zeros_like(seen_ref)

    m = jnp.dot(_rms(ya_ref[...], ga_ref[...]).astype(BF16), w_ref[0:D_A, :], preferred_element_type=F32)
    m += jnp.dot(_rms(yb_ref[...], gb_ref[...]).astype(BF16), w_ref[D_A:D_A + D_B, :], preferred_element_type=F32)
    m += jnp.dot(_rms(yc_ref[...], gc_ref[...]).astype(BF16), w_ref[D_A + D_B:, :], preferred_element_type=F32)
    x1 = _layer_norm(DEEPNORM_ALPHA * x_ref[...] + m, lg_ref[...], lb_ref[...])
    x1_ref[...] = x1
    x1h = x1.astype(BF16)

    x1l = (x1 - x1h.astype(F32)).astype(BF16)
    logits = (jnp.dot(x1h, wrh_ref[...], preferred_element_type=F32)
              + jnp.dot(x1l, wrh_ref[...], preferred_element_type=F32)
              + jnp.dot(x1h, wrl_ref[...], preferred_element_type=F32)) + br_ref[...]

    tm = logits.shape[0]
    col = lax.broadcasted_iota(jnp.int32, (tm, N_EXPERTS), 1).astype(F32)
    work = logits
    sels, vals, idxs = [], [], []
    for _ in range(TOP_K):
        top = jnp.max(work, axis=1, keepdims=True)
        idx = jnp.min(jnp.where(work == top, col, float(N_EXPERTS)), axis=1, keepdims=True)
        sel = col == idx
        work = jnp.where(sel, -jnp.inf, work)
        sels.append(sel)
        vals.append(top)
        idxs.append(idx)
    exps = [jnp.exp(v - vals[0]) for v in vals]
    total = functools.reduce(lambda a, b: a + b, exps)

    chosen = functools.reduce(lambda a, b: a + b, [s.astype(F32) for s in sels])
    ri = lax.broadcasted_iota(jnp.int32, (tm, tm), 0)
    ci = lax.broadcasted_iota(jnp.int32, (tm, tm), 1)
    earlier = jnp.where(ci < ri, 1.0, 0.0).astype(BF16)
    before = jnp.dot(earlier, chosen.astype(BF16), preferred_element_type=F32) + seen_ref[...]
    seen = seen_ref[...] + jnp.sum(chosen, axis=0, keepdims=True)
    seen_ref[...] = seen
    cnt_ref[...] = seen.astype(jnp.int32)

    slot = lax.broadcasted_iota(jnp.int32, (tm, TOP_K), 1)
    top_i = jnp.zeros((tm, TOP_K), F32)
    gates = jnp.zeros((tm, TOP_K), F32)
    pos = jnp.zeros((tm, TOP_K), F32)
    for k in range(TOP_K):
        rank = jnp.sum(jnp.where(sels[k], before, 0.0), axis=1, keepdims=True)
        top_i = jnp.where(slot == k, idxs[k], top_i)
        gates = jnp.where(slot == k, exps[k] / total, gates)
        pos = jnp.where(slot == k, rank, pos)
    topi_ref[...] = top_i.astype(jnp.int32)
    gate_ref[...] = gates
    pos_ref[...] = pos.astype(jnp.int32)


def out_proj(ya, yb, yc, x2d, g_a, g_b, g_c, w_out_bf16, ln_g, ln_b, w_router, b_router):
    n = x2d.shape[0]
    tm = min(ROW_TILE, n)
    rows = lambda w: pl.BlockSpec((tm, w), lambda i: (i, 0))
    wr = w_router.astype(F32)
    wr_hi = wr.astype(BF16)
    wr_lo = (wr - wr_hi.astype(F32)).astype(BF16)
    return pl.pallas_call(
        _out_proj_kernel,
        grid=(n // tm,),
        in_specs=[rows(D_A), rows(D_B), rows(D_C), rows(D_MODEL),
                  _full((1, D_A)), _full((1, D_B)), _full((1, D_C)), _full((D_MODEL, D_MODEL)),
                  _full((1, D_MODEL)), _full((1, D_MODEL)),
                  _full((D_MODEL, N_EXPERTS)), _full((D_MODEL, N_EXPERTS)), _full((1, N_EXPERTS))],
        out_specs=[rows(D_MODEL), rows(TOP_K), rows(TOP_K), rows(TOP_K), _full((1, N_EXPERTS))],
        out_shape=[jax.ShapeDtypeStruct((n, D_MODEL), F32),
                   jax.ShapeDtypeStruct((n, TOP_K), jnp.int32), jax.ShapeDtypeStruct((n, TOP_K), F32),
                   jax.ShapeDtypeStruct((n, TOP_K), jnp.int32), jax.ShapeDtypeStruct((1, N_EXPERTS), jnp.int32)],
        scratch_shapes=[pltpu.VMEM((1, N_EXPERTS), F32)],
        compiler_params=_cparams("arbitrary"),
        name="out_proj",
    )(ya, yb, yc, x2d, g_a.reshape(1, D_A), g_b.reshape(1, D_B), g_c.reshape(1, D_C), w_out_bf16,
      ln_g.reshape(1, D_MODEL), ln_b.reshape(1, D_MODEL), wr_hi, wr_lo, b_router.reshape(1, N_EXPERTS))


CAST_ROWS = 128


def _cast_weight(src_ref, dst_ref):
    def body(c, carry):
        r = pl.multiple_of(c * CAST_ROWS, CAST_ROWS)
        dst_ref[pl.ds(r, CAST_ROWS), :] = src_ref[0, 0, pl.ds(r, CAST_ROWS), :].astype(BF16)
        return carry

    lax.fori_loop(0, dst_ref.shape[0] // CAST_ROWS, body, 0)


def _moe_kernel(blk_e_ref, n_used_ref, tok_ref, tok_next_ref, x_ref, wgu_ref, bgu_ref, wdn_ref, bdn_ref, o_ref,
                wgu_s, wdn_s, xbuf_ref, xh_ref, sem_ref):
    i = pl.program_id(0)
    n_used = n_used_ref[0]
    slot = i % 2
    expert_changed = (i == 0) | (blk_e_ref[i] != blk_e_ref[jnp.maximum(i - 1, 0)])

    def row_copy(tok, r, s):
        return pltpu.make_async_copy(x_ref.at[pl.ds(tok, 1), :], xbuf_ref.at[s, pl.ds(r, 1), :], sem_ref.at[s])

    @pl.when(i == 0)
    def _():
        def body(r, carry):
            row_copy(tok_ref[0, 0, r], r, 0).start()
            return carry

        lax.fori_loop(0, MOE_ROWS, body, 0, unroll=8)

    @pl.when(expert_changed)
    def _():
        _cast_weight(wgu_ref, wgu_s)
        _cast_weight(wdn_ref, wdn_s)

    @pl.when(i <= n_used)
    def _():
        pltpu.make_async_copy(x_ref.at[pl.ds(0, MOE_ROWS), :], xbuf_ref.at[slot], sem_ref.at[slot]).wait()

    @pl.when(i < n_used)
    def _():
        xh_ref[...] = xbuf_ref[slot].astype(BF16)
        cw = D_FF // MOE_CHUNKS
        per_chunk = MOE_ROWS // MOE_CHUNKS
        acc = None
        for c in range(MOE_CHUNKS):
            for r in range(c * per_chunk, (c + 1) * per_chunk):
                row_copy(tok_next_ref[0, 0, r], r, 1 - slot).start()
            words = pltpu.bitcast(xbuf_ref[slot, 0:SUBLANES, 0:cw], jnp.uint32)
            zero = ((words >> 16) >> 16).astype(F32)[0:1, :]
            xh = xh_ref[...]
            hg = (jnp.dot(xh, wgu_s[:, c * cw:(c + 1) * cw], preferred_element_type=F32)
                  + (bgu_ref[0, 0, :, c * cw:(c + 1) * cw] + zero))
            hu = (jnp.dot(xh, wgu_s[:, D_FF + c * cw:D_FF + (c + 1) * cw], preferred_element_type=F32)
                  + bgu_ref[0, 0, :, D_FF + c * cw:D_FF + (c + 1) * cw])
            gate = jnp.minimum(hg, SWIGLU_LIMIT)
            up = jnp.clip(hu, -SWIGLU_LIMIT, SWIGLU_LIMIT)
            glu = gate * jax.nn.sigmoid(gate * SWIGLU_ALPHA)
            act = ((up + 1.0) * glu).astype(BF16)
            part = jnp.dot(act, wdn_s[c * cw:(c + 1) * cw, :], preferred_element_type=F32)
            acc = part if acc is None else acc + part
        o_ref[...] = acc + bdn_ref[0, 0]

    @pl.when(i >= n_used)
    def _():
        o_ref[...] = jnp.zeros_like(o_ref)


def moe_experts(x, row_tok, blk_e, n_used, layer, w_gu, b_gu, w_dn, b_dn):
    n_rows = row_tok.shape[0]
    n_blocks = n_rows // MOE_ROWS
    depth = w_gu.shape[0]
    tok_blocks = row_tok.reshape(n_blocks, 1, MOE_ROWS)
    tok_spec = lambda index_map: pl.BlockSpec((1, 1, MOE_ROWS), index_map, memory_space=pltpu.SMEM)
    grid_spec = pltpu.PrefetchScalarGridSpec(
        num_scalar_prefetch=2,
        grid=(n_blocks,),
        in_specs=[
            tok_spec(lambda i, be, nu: (i, 0, 0)),
            tok_spec(lambda i, be, nu: (jnp.minimum(i + 1, n_blocks - 1), 0, 0)),
            pl.BlockSpec(memory_space=pl.ANY),
            pl.BlockSpec((1, 1, D_MODEL, 2 * D_FF), lambda i, be, nu: (layer, be[i], 0, 0)),
            pl.BlockSpec((1, 1, 1, 2 * D_FF), lambda i, be, nu: (layer, be[i], 0, 0)),
            pl.BlockSpec((1, 1, D_FF, D_MODEL), lambda i, be, nu: (layer, be[i], 0, 0)),
            pl.BlockSpec((1, 1, 1, D_MODEL), lambda i, be, nu: (layer, be[i], 0, 0)),
        ],
        out_specs=pl.BlockSpec((MOE_ROWS, D_MODEL), lambda i, be, nu: (i, 0)),
        scratch_shapes=[pltpu.VMEM((D_MODEL, 2 * D_FF), BF16), pltpu.VMEM((D_FF, D_MODEL), BF16),
                        pltpu.VMEM((2, MOE_ROWS, D_MODEL), F32), pltpu.VMEM((MOE_ROWS, D_MODEL), BF16),
                        pltpu.SemaphoreType.DMA((2,))],
    )
    return pl.pallas_call(
        _moe_kernel,
        grid_spec=grid_spec,
        out_shape=jax.ShapeDtypeStruct((n_rows, D_MODEL), F32),
        compiler_params=_cparams("arbitrary"),
        name="moe_experts",
    )(blk_e, n_used, tok_blocks, tok_blocks, x, w_gu, b_gu.reshape(depth, N_EXPERTS, 1, 2 * D_FF), w_dn,
      b_dn.reshape(depth, N_EXPERTS, 1, D_MODEL))


def route(top_i, pos, counts):
    n_tok = top_i.shape[0]
    n_slots = n_tok * TOP_K
    padded = ((counts + MOE_ROWS - 1) // MOE_ROWS) * MOE_ROWS
    pad_end = jnp.cumsum(padded)
    pad_start = pad_end - padded
    start = jnp.cumsum(counts) - counts
    experts = jnp.arange(N_EXPERTS, dtype=jnp.int32)
    slot_start = jnp.sum(jnp.where(top_i[:, :, None] == experts, pad_start, 0), axis=-1)
    dest = (slot_start + pos).astype(jnp.int32).T
    n_blocks = -(-n_slots // MOE_ROWS) + N_EXPERTS
    blk_first = jnp.arange(n_blocks, dtype=jnp.int32) * MOE_ROWS
    blk_e = jnp.minimum(jnp.sum((pad_end[None, :] <= blk_first[:, None]).astype(jnp.int32), axis=1), N_EXPERTS - 1)
    order = jnp.argsort(top_i.reshape(-1)).astype(jnp.int32)
    within = jnp.arange(MOE_ROWS, dtype=jnp.int32)[None, :] + (blk_first - pad_start[blk_e])[:, None]
    src = jnp.clip(start[blk_e][:, None] + within, 0, n_slots - 1).reshape(-1)
    valid = (within < counts[blk_e][:, None]).reshape(-1)
    row_tok = jnp.where(valid, order[src] // TOP_K, 0).astype(jnp.int32)
    n_used = (pad_end[-1:] // MOE_ROWS).astype(jnp.int32)
    return dest, row_tok, blk_e.astype(jnp.int32), n_used


COMBINE_ROWS = 256


def _combine_ln_kernel(dcur_ref, dnext_ref, x_ref, gate_ref, g_ref, b_ref, yb_ref, o_ref, buf_ref, sem_ref, *,
                       n_tiles):
    i = pl.program_id(0)
    tm = x_ref.shape[0]
    n_rows = TOP_K * tm
    slot = i % 2

    def row_copy(row, r, s):
        return pltpu.make_async_copy(yb_ref.at[pl.ds(row, 1), :], buf_ref.at[s, pl.ds(r, 1), :], sem_ref.at[s])

    def start_gather(d_ref, s):
        def body(r, carry):
            row_copy(d_ref[0, 0, r], r, s).start()
            return carry

        lax.fori_loop(0, n_rows, body, 0, unroll=8)

    @pl.when(i == 0)
    def _():
        start_gather(dcur_ref, 0)

    @pl.when(i + 1 < n_tiles)
    def _():
        for r in range(n_rows):
            row_copy(dnext_ref[0, 0, r], r, 1 - slot).start()

    pltpu.make_async_copy(yb_ref.at[pl.ds(0, n_rows), :], buf_ref.at[slot], sem_ref.at[slot]).wait()
    gates = gate_ref[...]
    moe = buf_ref[slot, 0:tm, :] * gates[:, 0:1]
    for k in range(1, TOP_K):
        moe = moe + buf_ref[slot, k * tm:(k + 1) * tm, :] * gates[:, k:k + 1]
    o_ref[...] = _layer_norm(DEEPNORM_ALPHA * x_ref[...] + moe, g_ref[...], b_ref[...])


def combine_ln(x2d, yb, dest, gates, ln_g, ln_b):
    n = x2d.shape[0]
    tm = min(COMBINE_ROWS, n)
    n_tiles = n // tm
    rows = pl.BlockSpec((tm, D_MODEL), lambda i: (i, 0))
    dest_tiles = dest.reshape(TOP_K, n_tiles, tm).transpose(1, 0, 2).reshape(n_tiles, 1, TOP_K * tm)
    dest_spec = lambda index_map: pl.BlockSpec((1, 1, TOP_K * tm), index_map, memory_space=pltpu.SMEM)
    return pl.pallas_call(
        functools.partial(_combine_ln_kernel, n_tiles=n_tiles),
        grid=(n_tiles,),
        in_specs=[dest_spec(lambda i: (i, 0, 0)), dest_spec(lambda i: (jnp.minimum(i + 1, n_tiles - 1), 0, 0)),
                  rows, pl.BlockSpec((tm, TOP_K), lambda i: (i, 0)), _full((1, D_MODEL)), _full((1, D_MODEL)),
                  pl.BlockSpec(memory_space=pl.ANY)],
        out_specs=rows,
        out_shape=jax.ShapeDtypeStruct((n, D_MODEL), F32),
        scratch_shapes=[pltpu.VMEM((2, TOP_K * tm, D_MODEL), F32), pltpu.SemaphoreType.DMA((2,))],
        compiler_params=_cparams("arbitrary"),
        name="combine_ln",
    )(dest_tiles, dest_tiles, x2d, gates, ln_g.reshape(1, D_MODEL), ln_b.reshape(1, D_MODEL), yb)


def _block_diag(w):
    h, d, _ = w.shape
    return jnp.einsum('hij,hg->higj', w.astype(F32), jnp.eye(h, dtype=F32)).reshape(h * d, h * d)


def _mixer(x, state, p, attn_bias):
    bn, seq, _ = x.shape
    n = bn * seq
    conv_buf, h0, s0, k_cache, v_cache = state
    x2d = x.reshape(n, D_MODEL)
    u = in_proj(x2d, p['w_in']).reshape(bn, seq, IN_COLS)

    y_a, new_buf, h_last = rglru(u, conv_buf, h0, p['conv_w'], p['conv_b'], p['w_r'], p['b_r'], p['w_i'], p['b_i'],
                                 p['lam'])
    q_col, k_col, v_col = OFF_Q // LANE, OFF_K // LANE, OFF_V // LANE
    if k_cache is None:
        qb = min(BAND_PAST, seq)
        y_b = attention(u, q_col, u, k_col, u, v_col, u, k_col, v_col, attn_bias, qb=qb, pb=qb,
                        prev_is_same_array=True)
    else:
        y_b = attention(u, q_col, k_cache, 0, v_cache, 0, u, k_col, v_col, attn_bias, qb=seq,
                        pb=k_cache.shape[1], prev_is_same_array=False)
    y_c, s_last = s5(u, s0, p['bmat'], p['scan_tables'], p['cmat'], p['d_c'], p['w_glu'], p['b_glu'])

    x1, top_i, gates, pos, counts = out_proj(
        y_a.reshape(n, D_A), y_b.reshape(n, D_B), y_c.reshape(n, D_C), x2d, p['g_a'], p['g_b'], p['g_c'],
        p['w_out'], p['ln1_g'], p['ln1_b'], p['w_router'], p['b_router'])

    k_rows = u[:, :, OFF_K:OFF_V]
    v_rows = u[:, :, OFF_V:OFF_C]
    if k_cache is None:
        keep = min(BAND_PAST, seq)
        k_rows = k_rows[:, seq - keep:]
        v_rows = v_rows[:, seq - keep:]
    k_rows = k_rows.reshape(bn, -1, B_HEADS, B_HEAD_DIM)
    v_rows = v_rows.reshape(bn, -1, B_HEADS, B_HEAD_DIM)
    s_re = s_last[:, 0, :C_LANES].reshape(bn, C_GROUPS, C_STATE)
    s_im = s_last[:, 0, C_LANES:].reshape(bn, C_GROUPS, C_STATE)
    routed = dict(x1=x1, top_i=top_i, gates=gates, pos=pos, counts=counts.reshape(N_EXPERTS))
    return routed, (new_buf, h_last.reshape(bn, D_A), k_rows, v_rows, s_re, s_im)


def _moe(groups, p):
    experts = jnp.arange(N_EXPERTS, dtype=jnp.int32)
    seen = jnp.zeros((N_EXPERTS,), jnp.int32)
    pos_all = []
    for g in groups:
        pos_all.append(g['pos'] + jnp.sum(jnp.where(g['top_i'][:, :, None] == experts, seen, 0), axis=-1))
        seen = seen + g['counts']
    top_i = jnp.concatenate([g['top_i'] for g in groups], axis=0)
    dest, row_tok, blk_e, n_used = route(top_i, jnp.concatenate(pos_all, axis=0), seen)
    x1 = jnp.concatenate([g['x1'] for g in groups], axis=0)
    yb = moe_experts(x1, row_tok, blk_e, n_used, p['layer'], p['w_gu'], p['b_gu'], p['w_dn'], p['b_dn'])
    outs, first = [], 0
    for g in groups:
        n = g['x1'].shape[0]
        outs.append(combine_ln(g['x1'], yb, dest[:, first:first + n], g['gates'], p['ln2_g'], p['ln2_b']))
        first += n
    return outs


def kernel(x_prompt, x_sample, cache_conv_a, state_h_a, cache_k_b, cache_v_b, state_s_re_c, state_s_im_c, w_in, conv_w_a, conv_b_a, w_r_a, b_r_a, w_i_a, b_i_a, lambda_a, rel_bias_b, a_re_c, a_im_c, log_dt_c, b_re_c, b_im_c, c_re_c, c_im_c, d_c, w_glu_c, b_glu_c, g_norm_a, g_norm_b, g_norm_c, w_out, ln1_g, ln1_b, w_router, b_router, w_gu, b_gu, w_dn, b_dn, ln2_g, ln2_b):
    bp, seq_p, _ = x_prompt.shape
    bs, seq_s, _ = x_sample.shape
    kv_rows = cache_k_b.shape[2]
    yp, ys = x_prompt, x_sample
    p_states = [[] for _ in range(6)]
    s_states = [[] for _ in range(6)]
    for l in range(DEPTH):
        bmat, scan_tables, cmat = s5_params(a_re_c[l], a_im_c[l], log_dt_c[l], b_re_c[l], b_im_c[l], c_re_c[l],
                                           c_im_c[l])
        p = dict(
            w_in=w_in[l].astype(BF16), conv_w=conv_w_a[l], conv_b=conv_b_a[l],
            w_r=_block_diag(w_r_a[l]).astype(BF16), b_r=b_r_a[l], w_i=_block_diag(w_i_a[l]).astype(BF16),
            b_i=b_i_a[l], lam=lambda_a[l],
            bmat=bmat, scan_tables=scan_tables, cmat=cmat, d_c=d_c[l], w_glu=w_glu_c[l].astype(BF16), b_glu=b_glu_c[l],
            g_a=g_norm_a[l], g_b=g_norm_b[l], g_c=g_norm_c[l], w_out=w_out[l].astype(BF16),
            ln1_g=ln1_g[l], ln1_b=ln1_b[l], w_router=w_router[l], b_router=b_router[l],
            layer=l, w_gu=w_gu, b_gu=b_gu, w_dn=w_dn, b_dn=b_dn,
            ln2_g=ln2_g[l], ln2_b=ln2_b[l],
        )
        zero_state = (jnp.zeros((bp, CONV_W - 1, D_A), F32), jnp.zeros((bp, D_A), F32),
                      jnp.zeros((bp, 1, 2 * C_LANES), F32), None, None)
        qb = min(BAND_PAST, seq_p)
        routed_p, st = _mixer(yp, zero_state, p, bias_table(rel_bias_b[l], qb, qb, band_mask=True))
        for lst, s in zip(p_states, st):
            lst.append(s)
        s0 = jnp.concatenate([state_s_re_c[l].reshape(bs, 1, C_LANES), state_s_im_c[l].reshape(bs, 1, C_LANES)],
                             axis=-1)
        sample_state = (cache_conv_a[l], state_h_a[l], s0,
                        cache_k_b[l].reshape(bs, kv_rows, D_B), cache_v_b[l].reshape(bs, kv_rows, D_B))
        routed_s, st = _mixer(ys, sample_state, p, bias_table(rel_bias_b[l], seq_s, kv_rows, band_mask=False))
        for lst, s in zip(s_states, st):
            lst.append(s)
        yp, ys = _moe([routed_p, routed_s], p)
        yp = yp.reshape(bp, seq_p, D_MODEL)
        ys = ys.reshape(bs, seq_s, D_MODEL)
    return (yp, ys) + tuple(jnp.stack(s) for s in p_states) + tuple(jnp.stack(s) for s in s_states)
```

```python
import functools
import math

import jax
import jax.numpy as jnp
import numpy as np
from jax import lax
from jax.experimental import pallas as pl
from jax.experimental.pallas import tpu as pltpu

F32 = jnp.float32
BF16 = jnp.bfloat16

D_MODEL = 1024
DEPTH = 2
CHUNK = 64
PREV_CHUNKS = 8
BAND_PAST = PREV_CHUNKS * CHUNK
D_A = D_MODEL // 4
D_B = D_MODEL // 2
D_C = D_MODEL // 4
A_HEADS = 4
A_HEAD_DIM = D_A // A_HEADS
CONV_W = 4
RG_C = 8.0
B_HEADS = 8
B_HEAD_DIM = D_B // B_HEADS
REL_CLIP = 128
C_GW = 16
C_GROUPS = D_C // C_GW
C_STATE = 64
C_LANES = C_GROUPS * C_STATE
OFF_GA = D_A
OFF_Q = 2 * D_A
OFF_K = OFF_Q + D_B
OFF_V = OFF_K + D_B
OFF_C = OFF_V + D_B
IN_COLS = OFF_C + D_C
N_EXPERTS = 32
TOP_K = 4
D_FF = D_MODEL
SWIGLU_LIMIT = 7.0
SWIGLU_ALPHA = 1.702
DEEPNORM_ALPHA = (2 * DEPTH) ** 0.25
LN_EPS = 1e-5
RMS_EPS = 1e-6
NEG_INF = -1e30

LANE = 128
SUBLANES = 8
HEADS_PER_SLAB = LANE // B_HEAD_DIM
VMEM_LIMIT = 56 * 1024 * 1024

ROW_TILE = 512
MOE_ROWS = 512
MOE_CHUNKS = 4
MOE_START_CHUNKS = 2
ATTN_SUB_ROWS = 256


def _cparams(*sem):
    return pltpu.CompilerParams(dimension_semantics=sem, vmem_limit_bytes=VMEM_LIMIT)


def _full(shape):
    return pl.BlockSpec(shape, lambda *_: (0,) * len(shape))


def _in_proj_kernel(x_ref, w_ref, o_ref):
    o_ref[...] = jnp.dot(x_ref[...].astype(BF16), w_ref[...], preferred_element_type=F32)


def in_proj(x2d, w_bf16):
    n = x2d.shape[0]
    tm = min(ROW_TILE, n)
    return pl.pallas_call(
        _in_proj_kernel,
        grid=(n // tm,),
        in_specs=[pl.BlockSpec((tm, D_MODEL), lambda i: (i, 0)), _full((D_MODEL, IN_COLS))],
        out_specs=pl.BlockSpec((tm, IN_COLS), lambda i: (i, 0)),
        out_shape=jax.ShapeDtypeStruct((n, IN_COLS), F32),
        compiler_params=_cparams("parallel"),
        name="in_proj",
    )(x2d, w_bf16)


def _rglru_kernel(xa_ref, ga_ref, buf_ref, h0_ref, cw_ref, cb_ref, wr_ref, br_ref, wi_ref, bi_ref, lam_ref,
                  y_ref, nbuf_ref, hl_ref, xp_ref, hc_ref, as_ref, hs_ref, *, t):
    pad = SUBLANES
    hist = CONV_W - 1

    @pl.when(pl.program_id(1) == 0)
    def _():
        xp_ref[0:pad, :] = jnp.zeros((pad, D_A), F32)
        xp_ref[pad - hist:pad, :] = buf_ref[0]
        hc_ref[...] = h0_ref[0]

    xa = xa_ref[0]
    xp_ref[pad:pad + t, :] = xa
    xc = cb_ref[...] + xa * cw_ref[hist:hist + 1, :]
    for j in range(hist):
        xc = xc + xp_ref[pad - hist + j:pad - hist + j + t, :] * cw_ref[j:j + 1, :]
    tail = xp_ref[pad + t - hist:pad + t, :]
    nbuf_ref[0] = tail
    xp_ref[pad - hist:pad, :] = tail

    xch = xc.astype(BF16)
    r = jax.nn.sigmoid(jnp.dot(xch, wr_ref[...], preferred_element_type=F32) + br_ref[...])
    i = jax.nn.sigmoid(jnp.dot(xch, wi_ref[...], preferred_element_type=F32) + bi_ref[...])
    lam = lam_ref[...]
    softplus_neg_lam = jnp.maximum(-lam, 0.0) + jnp.log(1.0 + jnp.exp(-jnp.abs(lam)))
    log_a = (-RG_C) * r * softplus_neg_lam
    a = jnp.exp(log_a)
    b = jnp.sqrt(1.0 - jnp.exp(2.0 * log_a)) * (i * xc)

    row_in_group = lax.broadcasted_iota(jnp.int32, (t, D_A), 0) & (SUBLANES - 1)
    acc_a, acc_b = a, b
    sh = 1
    while sh < SUBLANES:
        m = row_in_group >= sh
        acc_b = jnp.where(m, acc_a * pltpu.roll(acc_b, sh, 0) + acc_b, acc_b)
        acc_a = jnp.where(m, acc_a * pltpu.roll(acc_a, sh, 0), acc_a)
        sh *= 2
    as_ref[...] = acc_a
    hs_ref[...] = acc_b

    def group(g, carry):
        rows = pl.ds(pl.multiple_of(g * SUBLANES, SUBLANES), SUBLANES)
        h = hs_ref[rows, :] + as_ref[rows, :] * carry
        hs_ref[rows, :] = h
        return h[SUBLANES - 1:SUBLANES, :]

    h_last = lax.fori_loop(0, t // SUBLANES, group, hc_ref[...], unroll=8)
    hc_ref[...] = h_last
    hl_ref[0] = h_last
    y_ref[0] = hs_ref[...] * jax.nn.gelu(ga_ref[0])


def rglru(u, conv_buf, h0, cw, cb, wr_bd, br, wi_bd, bi, lam):
    bn, seq, _ = u.shape
    t = min(ROW_TILE, seq)
    vec = _full((1, D_A))
    return pl.pallas_call(
        functools.partial(_rglru_kernel, t=t),
        grid=(bn, seq // t),
        in_specs=[
            pl.BlockSpec((1, t, D_A), lambda b, s: (b, s, 0)),
            pl.BlockSpec((1, t, D_A), lambda b, s: (b, s, OFF_GA // D_A)),
            pl.BlockSpec((1, CONV_W - 1, D_A), lambda b, s: (b, 0, 0)),
            pl.BlockSpec((1, 1, D_A), lambda b, s: (b, 0, 0)),
            _full((CONV_W, D_A)), vec, _full((D_A, D_A)), vec, _full((D_A, D_A)), vec, vec,
        ],
        out_specs=[
            pl.BlockSpec((1, t, D_A), lambda b, s: (b, s, 0)),
            pl.BlockSpec((1, CONV_W - 1, D_A), lambda b, s: (b, 0, 0)),
            pl.BlockSpec((1, 1, D_A), lambda b, s: (b, 0, 0)),
        ],
        out_shape=[
            jax.ShapeDtypeStruct((bn, seq, D_A), F32),
            jax.ShapeDtypeStruct((bn, CONV_W - 1, D_A), F32),
            jax.ShapeDtypeStruct((bn, 1, D_A), F32),
        ],
        scratch_shapes=[pltpu.VMEM((t + SUBLANES, D_A), F32), pltpu.VMEM((1, D_A), F32),
                        pltpu.VMEM((t, D_A), F32), pltpu.VMEM((t, D_A), F32)],
        compiler_params=_cparams("parallel", "arbitrary"),
        name="rglru",
    )(u, u, conv_buf, h0.reshape(bn, 1, D_A), cw, cb.reshape(1, D_A), wr_bd, br.reshape(1, D_A),
      wi_bd, bi.reshape(1, D_A), lam.reshape(1, D_A))


def _attn_kernel(q_ref, kp_ref, kc_ref, vp_ref, vc_ref, bias_ref, o_ref, *, pb, sq, mask_first_prev):
    scale = B_HEAD_DIM ** -0.5
    qb = q_ref.shape[1]
    q_all = (q_ref[0] * scale).astype(BF16)
    kp_all = kp_ref[0].astype(BF16)
    kc_all = kc_ref[0].astype(BF16)
    vp_all = vp_ref[0].astype(BF16)
    vc_all = vc_ref[0].astype(BF16)
    lane = lax.broadcasted_iota(jnp.int32, (1, LANE), 1)
    contract_last = (((1,), (1,)), ((), ()))
    for r in range(qb // sq):
        lo = max(pb + r * sq - BAND_PAST, 0)
        hi = (r + 1) * sq
        q = q_all[r * sq:hi, :]
        kp, vp = kp_all[lo:pb, :], vp_all[lo:pb, :]
        kc, vc = kc_all[0:hi, :], vc_all[0:hi, :]
        out = None
        for hh in range(HEADS_PER_SLAB):
            in_head = (lane // B_HEAD_DIM) == hh
            qh = jnp.where(in_head, q, jnp.zeros_like(q))
            sp = (lax.dot_general(qh, kp, contract_last, preferred_element_type=F32)
                  + bias_ref[hh, r * sq:hi, lo:pb])
            sc = (lax.dot_general(qh, kc, contract_last, preferred_element_type=F32)
                  + bias_ref[hh, r * sq:hi, pb:pb + hi])
            if mask_first_prev:
                sp = jnp.where(pl.program_id(2) == 0, NEG_INF, sp)
            m = jnp.maximum(jnp.max(sp, axis=-1, keepdims=True), jnp.max(sc, axis=-1, keepdims=True))
            ep = jnp.exp(sp - m)
            ec = jnp.exp(sc - m)
            denom = jnp.sum(ep, axis=-1, keepdims=True) + jnp.sum(ec, axis=-1, keepdims=True)
            o = (jnp.dot(ep.astype(BF16), vp, preferred_element_type=F32)
                 + jnp.dot(ec.astype(BF16), vc, preferred_element_type=F32)) / denom
            out = o if out is None else jnp.where(in_head, o, out)
        o_ref[0, r * sq:hi, :] = out


def attention(q_arr, q_col, kprev_arr, kprev_col, vprev_arr, vprev_col, kv_arr, k_col, v_col, bias, *, qb, pb,
              prev_is_same_array):
    bn, seq, _ = q_arr.shape
    n_slabs = B_HEADS // HEADS_PER_SLAB
    if prev_is_same_array:
        prev_map = lambda col: (lambda hp, b, s: (b, jnp.maximum(s - 1, 0), col + hp))
    else:
        prev_map = lambda col: (lambda hp, b, s: (b, 0, col + hp))
    cur_map = lambda col: (lambda hp, b, s: (b, s, col + hp))
    return pl.pallas_call(
        functools.partial(_attn_kernel, pb=pb, sq=min(qb, ATTN_SUB_ROWS), mask_first_prev=prev_is_same_array),
        grid=(n_slabs, bn, seq // qb),
        in_specs=[
            pl.BlockSpec((1, qb, LANE), cur_map(q_col)),
            pl.BlockSpec((1, pb, LANE), prev_map(kprev_col)),
            pl.BlockSpec((1, qb, LANE), cur_map(k_col)),
            pl.BlockSpec((1, pb, LANE), prev_map(vprev_col)),
            pl.BlockSpec((1, qb, LANE), cur_map(v_col)),
            pl.BlockSpec((HEADS_PER_SLAB, qb, pb + qb), lambda hp, b, s: (hp, 0, 0)),
        ],
        out_specs=pl.BlockSpec((1, qb, LANE), lambda hp, b, s: (b, s, hp)),
        out_shape=jax.ShapeDtypeStruct((bn, seq, D_B), F32),
        compiler_params=_cparams("arbitrary", "arbitrary", "arbitrary"),
        name="attention",
    )(q_arr, kprev_arr, kv_arr, vprev_arr, kv_arr, bias)


def _bias_kernel(v_ref, o_ref, *, band_mask):
    qb, width = o_ref.shape[1], o_ref.shape[2]
    period = v_ref.shape[2]
    table = pltpu.roll(jnp.broadcast_to(v_ref[0], (qb, period)), 0, 1, stride=1, stride_axis=0)[:, :width]
    if band_mask:
        i = lax.broadcasted_iota(jnp.int32, (qb, width), 0)
        j = lax.broadcasted_iota(jnp.int32, (qb, width), 1)
        dc = i // CHUNK - j // CHUNK + PREV_CHUNKS
        table = jnp.where((dc >= 0) & (dc <= PREV_CHUNKS), table, NEG_INF)
    o_ref[0] = table


def bias_table(rel_bias, qb, pb, band_mask):
    n_heads = rel_bias.shape[0]
    period = pb + 2 * qb
    m = np.arange(period)
    d = np.where(m < pb + qb, m, m - period)
    idx = np.clip(pb - d, -REL_CLIP, REL_CLIP) + REL_CLIP
    v = rel_bias.astype(F32)[:, idx].reshape(n_heads, 1, period)
    return pl.pallas_call(
        functools.partial(_bias_kernel, band_mask=band_mask),
        grid=(n_heads,),
        in_specs=[pl.BlockSpec((1, 1, period), lambda h: (h, 0, 0))],
        out_specs=pl.BlockSpec((1, qb, pb + qb), lambda h: (h, 0, 0)),
        out_shape=jax.ShapeDtypeStruct((n_heads, qb, pb + qb), F32),
        compiler_params=_cparams("parallel"),
        name="bias_table",
    )(v)


def _s5_kernel(u_ref, s0_ref, bmat_ref, ar_ref, ai_ref, cmat_ref, d_ref, wg_ref, bg_ref,
               y_ref, sl_ref, sc_ref, xs_ref, *, steps):
    @pl.when(pl.program_id(1) == 0)
    def _():
        sc_ref[...] = s0_ref[0]

    u = u_ref[0]
    xs_ref[...] = jnp.dot(u.astype(BF16), bmat_ref[...], preferred_element_type=F32)
    a_re = ar_ref[...]
    a_im = ai_ref[...]

    def step(t, s):
        rows = pl.ds(pl.multiple_of(t * SUBLANES, SUBLANES), SUBLANES)
        s = xs_ref[rows, :] + a_re * s + a_im * pltpu.roll(s, C_LANES, 1)
        xs_ref[rows, :] = s
        return s

    s_last = lax.fori_loop(0, steps, step, sc_ref[...], unroll=4)
    sc_ref[...] = s_last
    sl_ref[0] = s_last
    y = jnp.dot(xs_ref[...].astype(BF16), cmat_ref[...], preferred_element_type=F32) + d_ref[...] * u
    y = jax.nn.gelu(y)
    gate = jax.nn.sigmoid(jnp.dot(y.astype(BF16), wg_ref[...], preferred_element_type=F32) + bg_ref[...])
    y_ref[0] = y * gate


S5_STEPS = 64


def s5(u, s0, bmat, a_tables, cmat, d_skip, w_glu, b_glu):
    bn, seq, _ = u.shape
    groups = bn // SUBLANES
    steps = min(S5_STEPS, seq)
    rows = steps * SUBLANES
    uc = u[:, :, OFF_C:].reshape(groups, SUBLANES, seq, D_C).transpose(0, 2, 1, 3).reshape(groups, seq * SUBLANES, D_C)
    y_tm, s_last = pl.pallas_call(
        functools.partial(_s5_kernel, steps=steps),
        grid=(groups, seq // steps),
        in_specs=[
            pl.BlockSpec((1, rows, D_C), lambda g, s: (g, s, 0)),
            pl.BlockSpec((1, SUBLANES, 2 * C_LANES), lambda g, s: (g, 0, 0)),
            _full((D_C, 2 * C_LANES)), _full((SUBLANES, 2 * C_LANES)), _full((SUBLANES, 2 * C_LANES)),
            _full((2 * C_LANES, D_C)), _full((1, D_C)), _full((D_C, D_C)), _full((1, D_C)),
        ],
        out_specs=[
            pl.BlockSpec((1, rows, D_C), lambda g, s: (g, s, 0)),
            pl.BlockSpec((1, SUBLANES, 2 * C_LANES), lambda g, s: (g, 0, 0)),
        ],
        out_shape=[
            jax.ShapeDtypeStruct((groups, seq * SUBLANES, D_C), F32),
            jax.ShapeDtypeStruct((groups, SUBLANES, 2 * C_LANES), F32),
        ],
        scratch_shapes=[pltpu.VMEM((SUBLANES, 2 * C_LANES), F32), pltpu.VMEM((rows, 2 * C_LANES), F32)],
        compiler_params=_cparams("parallel", "arbitrary"),
        name="s5",
    )(uc, s0.reshape(groups, SUBLANES, 2 * C_LANES), bmat, *a_tables, cmat, d_skip.reshape(1, D_C), w_glu,
      b_glu.reshape(1, D_C))
    y = y_tm.reshape(groups, seq, SUBLANES, D_C).transpose(0, 2, 1, 3).reshape(bn, seq, D_C)
    return y, s_last.reshape(bn, 1, 2 * C_LANES)


def s5_params(a_re, a_im, log_dt, b_re, b_im, c_re, c_im):
    lam = lax.complex(a_re.astype(F32), a_im.astype(F32))
    dt = jnp.exp(log_dt.astype(F32))[:, None]
    a_bar = jnp.exp(lam * dt)
    b_bar = ((a_bar - 1.0) / lam)[:, :, None] * lax.complex(b_re.astype(F32), b_im.astype(F32))
    eye = jnp.eye(C_GROUPS, dtype=F32)

    def block_in(m):
        return jnp.einsum('gpi,gh->gihp', m, eye).reshape(D_C, C_LANES)

    def block_out(m):
        return jnp.einsum('gip,gh->gphi', m, eye).reshape(C_LANES, D_C)

    bmat = jnp.concatenate([block_in(jnp.real(b_bar)), block_in(jnp.imag(b_bar))], axis=1)
    cmat = jnp.concatenate([block_out(c_re.astype(F32)), -block_out(c_im.astype(F32))], axis=0)
    re = jnp.real(a_bar).reshape(1, C_LANES)
    im = jnp.imag(a_bar).reshape(1, C_LANES)
    a_re2 = jnp.broadcast_to(jnp.concatenate([re, re], axis=1), (SUBLANES, 2 * C_LANES))
    a_im2 = jnp.broadcast_to(jnp.concatenate([-im, im], axis=1), (SUBLANES, 2 * C_LANES))
    return bmat.astype(BF16), (a_re2, a_im2), cmat.astype(BF16)


def _rms(x, g):
    return x * lax.rsqrt(jnp.mean(jnp.square(x), axis=-1, keepdims=True) + RMS_EPS) * g


def _layer_norm(x, g, b):
    mu = jnp.mean(x, axis=-1, keepdims=True)
    xc = x - mu
    var = jnp.mean(jnp.square(xc), axis=-1, keepdims=True)
    return xc * lax.rsqrt(var + LN_EPS) * g + b


def _out_proj_kernel(ya_ref, yb_ref, yc_ref, x_ref, ga_ref, gb_ref, gc_ref, w_ref, lg_ref, lb_ref,
                     wrh_ref, wrl_ref, br_ref,
                     x1_ref, topi_ref, gate_ref, pos_ref, cnt_ref, seen_ref):
    @pl.when(pl.program_id(0) == 0)
    def _():
        seen_ref[...] = jnp.zeros_like(seen_ref)

    m = jnp.dot(_rms(ya_ref[...], ga_ref[...]).astype(BF16), w_ref[0:D_A, :], preferred_element_type=F32)
    m += jnp.dot(_rms(yb_ref[...], gb_ref[...]).astype(BF16), w_ref[D_A:D_A + D_B, :], preferred_element_type=F32)
    m += jnp.dot(_rms(yc_ref[...], gc_ref[...]).astype(BF16), w_ref[D_A + D_B:, :], preferred_element_type=F32)
    x1 = _layer_norm(DEEPNORM_ALPHA * x_ref[...] + m, lg_ref[...], lb_ref[...])
    x1_ref[...] = x1
    x1h = x1.astype(BF16)

    x1l = (x1 - x1h.astype(F32)).astype(BF16)
    logits = (jnp.dot(x1h, wrh_ref[...], preferred_element_type=F32)
              + jnp.dot(x1l, wrh_ref[...], preferred_element_type=F32)
              + jnp.dot(x1h, wrl_ref[...], preferred_element_type=F32)) + br_ref[...]

    tm = logits.shape[0]
    col = lax.broadcasted_iota(jnp.int32, (tm, N_EXPERTS), 1).astype(F32)
    work = logits
    sels, vals, idxs = [], [], []
    for _ in range(TOP_K):
        top = jnp.max(work, axis=1, keepdims=True)
        idx = jnp.min(jnp.where(work == top, col, float(N_EXPERTS)), axis=1, keepdims=True)
        sel = col == idx
        work = jnp.where(sel, -jnp.inf, work)
        sels.append(sel)
        vals.append(top)
        idxs.append(idx)
    exps = [jnp.exp(v - vals[0]) for v in vals]
    total = functools.reduce(lambda a, b: a + b, exps)

    chosen = functools.reduce(lambda a, b: a + b, [s.astype(F32) for s in sels])
    ri = lax.broadcasted_iota(jnp.int32, (tm, tm), 0)
    ci = lax.broadcasted_iota(jnp.int32, (tm, tm), 1)
    earlier = jnp.where(ci < ri, 1.0, 0.0).astype(BF16)
    before = jnp.dot(earlier, chosen.astype(BF16), preferred_element_type=F32) + seen_ref[...]
    seen = seen_ref[...] + jnp.sum(chosen, axis=0, keepdims=True)
    seen_ref[...] = seen
    cnt_ref[...] = seen.astype(jnp.int32)

    slot = lax.broadcasted_iota(jnp.int32, (tm, TOP_K), 1)
    top_i = jnp.zeros((tm, TOP_K), F32)
    gates = jnp.zeros((tm, TOP_K), F32)
    pos = jnp.zeros((tm, TOP_K), F32)
    for k in range(TOP_K):
        rank = jnp.sum(jnp.where(sels[k], before, 0.0), axis=1, keepdims=True)
        top_i = jnp.where(slot == k, idxs[k], top_i)
        gates = jnp.where(slot == k, exps[k] / total, gates)
        pos = jnp.where(slot == k, rank, pos)
    topi_ref[...] = top_i.astype(jnp.int32)
    gate_ref[...] = gates
    pos_ref[...] = pos.astype(jnp.int32)


def out_proj(ya, yb, yc, x2d, g_a, g_b, g_c, w_out_bf16, ln_g, ln_b, w_router, b_router):
    n = x2d.shape[0]
    tm = min(ROW_TILE, n)
    rows = lambda w: pl.BlockSpec((tm, w), lambda i: (i, 0))
    wr = w_router.astype(F32)
    wr_hi = wr.astype(BF16)
    wr_lo = (wr - wr_hi.astype(F32)).astype(BF16)
    return pl.pallas_call(
        _out_proj_kernel,
        grid=(n // tm,),
        in_specs=[rows(D_A), rows(D_B), rows(D_C), rows(D_MODEL),
                  _full((1, D_A)), _full((1, D_B)), _full((1, D_C)), _full((D_MODEL, D_MODEL)),
                  _full((1, D_MODEL)), _full((1, D_MODEL)),
                  _full((D_MODEL, N_EXPERTS)), _full((D_MODEL, N_EXPERTS)), _full((1, N_EXPERTS))],
        out_specs=[rows(D_MODEL), rows(TOP_K), rows(TOP_K), rows(TOP_K), _full((1, N_EXPERTS))],
        out_shape=[jax.ShapeDtypeStruct((n, D_MODEL), F32),
                   jax.ShapeDtypeStruct((n, TOP_K), jnp.int32), jax.ShapeDtypeStruct((n, TOP_K), F32),
                   jax.ShapeDtypeStruct((n, TOP_K), jnp.int32), jax.ShapeDtypeStruct((1, N_EXPERTS), jnp.int32)],
        scratch_shapes=[pltpu.VMEM((1, N_EXPERTS), F32)],
        compiler_params=_cparams("arbitrary"),
        name="out_proj",
    )(ya, yb, yc, x2d, g_a.reshape(1, D_A), g_b.reshape(1, D_B), g_c.reshape(1, D_C), w_out_bf16,
      ln_g.reshape(1, D_MODEL), ln_b.reshape(1, D_MODEL), wr_hi, wr_lo, b_router.reshape(1, N_EXPERTS))


CAST_ROWS = 128


def _cast_weight(src_ref, dst_ref):
    def body(c, carry):
        r = pl.multiple_of(c * CAST_ROWS, CAST_ROWS)
        dst_ref[pl.ds(r, CAST_ROWS), :] = src_ref[0, 0, pl.ds(r, CAST_ROWS), :].astype(BF16)
        return carry

    lax.fori_loop(0, dst_ref.shape[0] // CAST_ROWS, body, 0)


def _moe_kernel(blk_e_ref, n_used_ref, tok_ref, tok_next_ref, x_ref, wgu_ref, bgu_ref, wdn_ref, bdn_ref, o_ref,
                wgu_s, wdn_s, xbuf_ref, xh_ref, sem_ref):
    i = pl.program_id(0)
    n_used = n_used_ref[0]
    slot = i % 2
    expert_changed = (i == 0) | (blk_e_ref[i] != blk_e_ref[jnp.maximum(i - 1, 0)])

    def row_copy(tok, r, s):
        return pltpu.make_async_copy(x_ref.at[pl.ds(tok, 1), :], xbuf_ref.at[s, pl.ds(r, 1), :], sem_ref.at[s])

    @pl.when(i == 0)
    def _():
        def body(r, carry):
            row_copy(tok_ref[0, 0, r], r, 0).start()
            return carry

        lax.fori_loop(0, MOE_ROWS, body, 0, unroll=8)

    @pl.when(expert_changed)
    def _():
        _cast_weight(wgu_ref, wgu_s)
        _cast_weight(wdn_ref, wdn_s)

    @pl.when(i <= n_used)
    def _():
        pltpu.make_async_copy(x_ref.at[pl.ds(0, MOE_ROWS), :], xbuf_ref.at[slot], sem_ref.at[slot]).wait()

    @pl.when(i < n_used)
    def _():
        xh_ref[...] = xbuf_ref[slot].astype(BF16)
        cw = D_FF // MOE_CHUNKS
        per_chunk = MOE_ROWS // MOE_START_CHUNKS
        acc = None
        for c in range(MOE_CHUNKS):
            for r in range(min(c, MOE_START_CHUNKS) * per_chunk, min(c + 1, MOE_START_CHUNKS) * per_chunk):
                row_copy(tok_next_ref[0, 0, r], r, 1 - slot).start()
            words = pltpu.bitcast(xbuf_ref[slot, 0:SUBLANES, 0:cw], jnp.uint32)
            zero = ((words >> 16) >> 16).astype(F32)[0:1, :]
            xh = xh_ref[...]
            hg = (jnp.dot(xh, wgu_s[:, c * cw:(c + 1) * cw], preferred_element_type=F32)
                  + (bgu_ref[0, 0, :, c * cw:(c + 1) * cw] + zero))
            hu = (jnp.dot(xh, wgu_s[:, D_FF + c * cw:D_FF + (c + 1) * cw], preferred_element_type=F32)
                  + bgu_ref[0, 0, :, D_FF + c * cw:D_FF + (c + 1) * cw])
            gate = jnp.minimum(hg, SWIGLU_LIMIT)
            up = jnp.clip(hu, -SWIGLU_LIMIT, SWIGLU_LIMIT)
            glu = gate * jax.nn.sigmoid(gate * SWIGLU_ALPHA)
            act = ((up + 1.0) * glu).astype(BF16)
            part = jnp.dot(act, wdn_s[c * cw:(c + 1) * cw, :], preferred_element_type=F32)
            acc = part if acc is None else acc + part
        o_ref[...] = acc + bdn_ref[0, 0]

    @pl.when(i >= n_used)
    def _():
        o_ref[...] = jnp.zeros_like(o_ref)


def moe_experts(x, row_tok, blk_e, n_used, layer, w_gu, b_gu, w_dn, b_dn):
    n_rows = row_tok.shape[0]
    n_blocks = n_rows // MOE_ROWS
    depth = w_gu.shape[0]
    tok_blocks = row_tok.reshape(n_blocks, 1, MOE_ROWS)
    tok_spec = lambda index_map: pl.BlockSpec((1, 1, MOE_ROWS), index_map, memory_space=pltpu.SMEM)
    grid_spec = pltpu.PrefetchScalarGridSpec(
        num_scalar_prefetch=2,
        grid=(n_blocks,),
        in_specs=[
            tok_spec(lambda i, be, nu: (i, 0, 0)),
            tok_spec(lambda i, be, nu: (jnp.minimum(i + 1, n_blocks - 1), 0, 0)),
            pl.BlockSpec(memory_space=pl.ANY),
            pl.BlockSpec((1, 1, D_MODEL, 2 * D_FF), lambda i, be, nu: (layer, be[i], 0, 0)),
            pl.BlockSpec((1, 1, 1, 2 * D_FF), lambda i, be, nu: (layer, be[i], 0, 0)),
            pl.BlockSpec((1, 1, D_FF, D_MODEL), lambda i, be, nu: (layer, be[i], 0, 0)),
            pl.BlockSpec((1, 1, 1, D_MODEL), lambda i, be, nu: (layer, be[i], 0, 0)),
        ],
        out_specs=pl.BlockSpec((MOE_ROWS, D_MODEL), lambda i, be, nu: (i, 0)),
        scratch_shapes=[pltpu.VMEM((D_MODEL, 2 * D_FF), BF16), pltpu.VMEM((D_FF, D_MODEL), BF16),
                        pltpu.VMEM((2, MOE_ROWS, D_MODEL), F32), pltpu.VMEM((MOE_ROWS, D_MODEL), BF16),
                        pltpu.SemaphoreType.DMA((2,))],
    )
    return pl.pallas_call(
        _moe_kernel,
        grid_spec=grid_spec,
        out_shape=jax.ShapeDtypeStruct((n_rows, D_MODEL), F32),
        compiler_params=_cparams("arbitrary"),
        name="moe_experts",
    )(blk_e, n_used, tok_blocks, tok_blocks, x, w_gu, b_gu.reshape(depth, N_EXPERTS, 1, 2 * D_FF), w_dn,
      b_dn.reshape(depth, N_EXPERTS, 1, D_MODEL))


def route(top_i, pos, counts):
    n_tok = top_i.shape[0]
    n_slots = n_tok * TOP_K
    padded = ((counts + MOE_ROWS - 1) // MOE_ROWS) * MOE_ROWS
    pad_end = jnp.cumsum(padded)
    pad_start = pad_end - padded
    start = jnp.cumsum(counts) - counts
    experts = jnp.arange(N_EXPERTS, dtype=jnp.int32)
    slot_start = jnp.sum(jnp.where(top_i[:, :, None] == experts, pad_start, 0), axis=-1)
    dest = (slot_start + pos).astype(jnp.int32).T
    n_blocks = -(-n_slots // MOE_ROWS) + N_EXPERTS
    blk_first = jnp.arange(n_blocks, dtype=jnp.int32) * MOE_ROWS
    blk_e = jnp.minimum(jnp.sum((pad_end[None, :] <= blk_first[:, None]).astype(jnp.int32), axis=1), N_EXPERTS - 1)
    order = jnp.argsort(top_i.reshape(-1)).astype(jnp.int32)
    within = jnp.arange(MOE_ROWS, dtype=jnp.int32)[None, :] + (blk_first - pad_start[blk_e])[:, None]
    src = jnp.clip(start[blk_e][:, None] + within, 0, n_slots - 1).reshape(-1)
    valid = (within < counts[blk_e][:, None]).reshape(-1)
    row_tok = jnp.where(valid, order[src] // TOP_K, 0).astype(jnp.int32)
    n_used = (pad_end[-1:] // MOE_ROWS).astype(jnp.int32)
    return dest, row_tok, blk_e.astype(jnp.int32), n_used


COMBINE_ROWS = 256


def _combine_ln_kernel(dcur_ref, dnext_ref, x_ref, gate_ref, g_ref, b_ref, yb_ref, o_ref, buf_ref, sem_ref, *,
                       n_tiles):
    i = pl.program_id(0)
    tm = x_ref.shape[0]
    n_rows = TOP_K * tm
    slot = i % 2

    def row_copy(row, r, s):
        return pltpu.make_async_copy(yb_ref.at[pl.ds(row, 1), :], buf_ref.at[s, pl.ds(r, 1), :], sem_ref.at[s])

    def start_gather(d_ref, s):
        def body(r, carry):
            row_copy(d_ref[0, 0, r], r, s).start()
            return carry

        lax.fori_loop(0, n_rows, body, 0, unroll=8)

    @pl.when(i == 0)
    def _():
        start_gather(dcur_ref, 0)

    @pl.when(i + 1 < n_tiles)
    def _():
        for r in range(n_rows):
            row_copy(dnext_ref[0, 0, r], r, 1 - slot).start()

    pltpu.make_async_copy(yb_ref.at[pl.ds(0, n_rows), :], buf_ref.at[slot], sem_ref.at[slot]).wait()
    gates = gate_ref[...]
    moe = buf_ref[slot, 0:tm, :] * gates[:, 0:1]
    for k in range(1, TOP_K):
        moe = moe + buf_ref[slot, k * tm:(k + 1) * tm, :] * gates[:, k:k + 1]
    o_ref[...] = _layer_norm(DEEPNORM_ALPHA * x_ref[...] + moe, g_ref[...], b_ref[...])


def combine_ln(x2d, yb, dest, gates, ln_g, ln_b):
    n = x2d.shape[0]
    tm = min(COMBINE_ROWS, n)
    n_tiles = n // tm
    rows = pl.BlockSpec((tm, D_MODEL), lambda i: (i, 0))
    dest_tiles = dest.reshape(TOP_K, n_tiles, tm).transpose(1, 0, 2).reshape(n_tiles, 1, TOP_K * tm)
    dest_spec = lambda index_map: pl.BlockSpec((1, 1, TOP_K * tm), index_map, memory_space=pltpu.SMEM)
    return pl.pallas_call(
        functools.partial(_combine_ln_kernel, n_tiles=n_tiles),
        grid=(n_tiles,),
        in_specs=[dest_spec(lambda i: (i, 0, 0)), dest_spec(lambda i: (jnp.minimum(i + 1, n_tiles - 1), 0, 0)),
                  rows, pl.BlockSpec((tm, TOP_K), lambda i: (i, 0)), _full((1, D_MODEL)), _full((1, D_MODEL)),
                  pl.BlockSpec(memory_space=pl.ANY)],
        out_specs=rows,
        out_shape=jax.ShapeDtypeStruct((n, D_MODEL), F32),
        scratch_shapes=[pltpu.VMEM((2, TOP_K * tm, D_MODEL), F32), pltpu.SemaphoreType.DMA((2,))],
        compiler_params=_cparams("arbitrary"),
        name="combine_ln",
    )(dest_tiles, dest_tiles, x2d, gates, ln_g.reshape(1, D_MODEL), ln_b.reshape(1, D_MODEL), yb)


def _block_diag(w):
    h, d, _ = w.shape
    return jnp.einsum('hij,hg->higj', w.astype(F32), jnp.eye(h, dtype=F32)).reshape(h * d, h * d)


def _mixer(x, state, p, attn_bias):
    bn, seq, _ = x.shape
    n = bn * seq
    conv_buf, h0, s0, k_cache, v_cache = state
    x2d = x.reshape(n, D_MODEL)
    u = in_proj(x2d, p['w_in']).reshape(bn, seq, IN_COLS)

    y_a, new_buf, h_last = rglru(u, conv_buf, h0, p['conv_w'], p['conv_b'], p['w_r'], p['b_r'], p['w_i'], p['b_i'],
                                 p['lam'])
    q_col, k_col, v_col = OFF_Q // LANE, OFF_K // LANE, OFF_V // LANE
    if k_cache is None:
        qb = min(BAND_PAST, seq)
        y_b = attention(u, q_col, u, k_col, u, v_col, u, k_col, v_col, attn_bias, qb=qb, pb=qb,
                        prev_is_same_array=True)
    else:
        y_b = attention(u, q_col, k_cache, 0, v_cache, 0, u, k_col, v_col, attn_bias, qb=seq,
                        pb=k_cache.shape[1], prev_is_same_array=False)
    y_c, s_last = s5(u, s0, p['bmat'], p['scan_tables'], p['cmat'], p['d_c'], p['w_glu'], p['b_glu'])

    x1, top_i, gates, pos, counts = out_proj(
        y_a.reshape(n, D_A), y_b.reshape(n, D_B), y_c.reshape(n, D_C), x2d, p['g_a'], p['g_b'], p['g_c'],
        p['w_out'], p['ln1_g'], p['ln1_b'], p['w_router'], p['b_router'])

    k_rows = u[:, :, OFF_K:OFF_V]
    v_rows = u[:, :, OFF_V:OFF_C]
    if k_cache is None:
        keep = min(BAND_PAST, seq)
        k_rows = k_rows[:, seq - keep:]
        v_rows = v_rows[:, seq - keep:]
    k_rows = k_rows.reshape(bn, -1, B_HEADS, B_HEAD_DIM)
    v_rows = v_rows.reshape(bn, -1, B_HEADS, B_HEAD_DIM)
    s_re = s_last[:, 0, :C_LANES].reshape(bn, C_GROUPS, C_STATE)
    s_im = s_last[:, 0, C_LANES:].reshape(bn, C_GROUPS, C_STATE)
    routed = dict(x1=x1, top_i=top_i, gates=gates, pos=pos, counts=counts.reshape(N_EXPERTS))
    return routed, (new_buf, h_last.reshape(bn, D_A), k_rows, v_rows, s_re, s_im)


def _moe(groups, p):
    experts = jnp.arange(N_EXPERTS, dtype=jnp.int32)
    seen = jnp.zeros((N_EXPERTS,), jnp.int32)
    pos_all = []
    for g in groups:
        pos_all.append(g['pos'] + jnp.sum(jnp.where(g['top_i'][:, :, None] == experts, seen, 0), axis=-1))
        seen = seen + g['counts']
    top_i = jnp.concatenate([g['top_i'] for g in groups], axis=0)
    dest, row_tok, blk_e, n_used = route(top_i, jnp.concatenate(pos_all, axis=0), seen)
    x1 = jnp.concatenate([g['x1'] for g in groups], axis=0)
    yb = moe_experts(x1, row_tok, blk_e, n_used, p['layer'], p['w_gu'], p['b_gu'], p['w_dn'], p['b_dn'])
    outs, first = [], 0
    for g in groups:
        n = g['x1'].shape[0]
        outs.append(combine_ln(g['x1'], yb, dest[:, first:first + n], g['gates'], p['ln2_g'], p['ln2_b']))
        first += n
    return outs


def kernel(x_prompt, x_sample, cache_conv_a, state_h_a, cache_k_b, cache_v_b, state_s_re_c, state_s_im_c, w_in, conv_w_a, conv_b_a, w_r_a, b_r_a, w_i_a, b_i_a, lambda_a, rel_bias_b, a_re_c, a_im_c, log_dt_c, b_re_c, b_im_c, c_re_c, c_im_c, d_c, w_glu_c, b_glu_c, g_norm_a, g_norm_b, g_norm_c, w_out, ln1_g, ln1_b, w_router, b_router, w_gu, b_gu, w_dn, b_dn, ln2_g, ln2_b):
    bp, seq_p, _ = x_prompt.shape
    bs, seq_s, _ = x_sample.shape
    kv_rows = cache_k_b.shape[2]
    yp, ys = x_prompt, x_sample
    p_states = [[] for _ in range(6)]
    s_states = [[] for _ in range(6)]
    for l in range(DEPTH):
        bmat, scan_tables, cmat = s5_params(a_re_c[l], a_im_c[l], log_dt_c[l], b_re_c[l], b_im_c[l], c_re_c[l],
                                           c_im_c[l])
        p = dict(
            w_in=w_in[l].astype(BF16), conv_w=conv_w_a[l], conv_b=conv_b_a[l],
            w_r=_block_diag(w_r_a[l]).astype(BF16), b_r=b_r_a[l], w_i=_block_diag(w_i_a[l]).astype(BF16),
            b_i=b_i_a[l], lam=lambda_a[l],
            bmat=bmat, scan_tables=scan_tables, cmat=cmat, d_c=d_c[l], w_glu=w_glu_c[l].astype(BF16), b_glu=b_glu_c[l],
            g_a=g_norm_a[l], g_b=g_norm_b[l], g_c=g_norm_c[l], w_out=w_out[l].astype(BF16),
            ln1_g=ln1_g[l], ln1_b=ln1_b[l], w_router=w_router[l], b_router=b_router[l],
            layer=l, w_gu=w_gu, b_gu=b_gu, w_dn=w_dn, b_dn=b_dn,
            ln2_g=ln2_g[l], ln2_b=ln2_b[l],
        )
        zero_state = (jnp.zeros((bp, CONV_W - 1, D_A), F32), jnp.zeros((bp, D_A), F32),
                      jnp.zeros((bp, 1, 2 * C_LANES), F32), None, None)
        qb = min(BAND_PAST, seq_p)
        routed_p, st = _mixer(yp, zero_state, p, bias_table(rel_bias_b[l], qb, qb, band_mask=True))
        for lst, s in zip(p_states, st):
            lst.append(s)
        s0 = jnp.concatenate([state_s_re_c[l].reshape(bs, 1, C_LANES), state_s_im_c[l].reshape(bs, 1, C_LANES)],
                             axis=-1)
        sample_state = (cache_conv_a[l], state_h_a[l], s0,
                        cache_k_b[l].reshape(bs, kv_rows, D_B), cache_v_b[l].reshape(bs, kv_rows, D_B))
        routed_s, st = _mixer(ys, sample_state, p, bias_table(rel_bias_b[l], seq_s, kv_rows, band_mask=False))
        for lst, s in zip(s_states, st):
            lst.append(s)
        yp, ys = _moe([routed_p, routed_s], p)
        yp = yp.reshape(bp, seq_p, D_MODEL)
        ys = ys.reshape(bs, seq_s, D_MODEL)
    return (yp, ys) + tuple(jnp.stack(s) for s in p_states) + tuple(jnp.stack(s) for s in s_states)
```

```python
import functools
import math

import jax
import jax.numpy as jnp
import numpy as np
from jax import lax
from jax.experimental import pallas as pl
from jax.experimental.pallas import tpu as pltpu

F32 = jnp.float32
BF16 = jnp.bfloat16

D_MODEL = 1024
DEPTH = 2
CHUNK = 64
PREV_CHUNKS = 8
BAND_PAST = PREV_CHUNKS * CHUNK
D_A = D_MODEL // 4
D_B = D_MODEL // 2
D_C = D_MODEL // 4
A_HEADS = 4
A_HEAD_DIM = D_A // A_HEADS
CONV_W = 4
RG_C = 8.0
B_HEADS = 8
B_HEAD_DIM = D_B // B_HEADS
REL_CLIP = 128
C_GW = 16
C_GROUPS = D_C // C_GW
C_STATE = 64
C_LANES = C_GROUPS * C_STATE
OFF_GA = D_A
OFF_Q = 2 * D_A
OFF_K = OFF_Q + D_B
OFF_V = OFF_K + D_B
OFF_C = OFF_V + D_B
IN_COLS = OFF_C + D_C
N_EXPERTS = 32
TOP_K = 4
D_FF = D_MODEL
SWIGLU_LIMIT = 7.0
SWIGLU_ALPHA = 1.702
DEEPNORM_ALPHA = (2 * DEPTH) ** 0.25
LN_EPS = 1e-5
RMS_EPS = 1e-6
NEG_INF = -1e30

LANE = 128
SUBLANES = 8
HEADS_PER_SLAB = LANE // B_HEAD_DIM
VMEM_LIMIT = 56 * 1024 * 1024

ROW_TILE = 512
MOE_ROWS = 512
MOE_CHUNKS = 4
DMA_THREADS = 2
MOE_START_CHUNKS = 4
ATTN_SUB_ROWS = 256


def _cparams(*sem):
    return pltpu.CompilerParams(dimension_semantics=sem, vmem_limit_bytes=VMEM_LIMIT)


def _full(shape):
    return pl.BlockSpec(shape, lambda *_: (0,) * len(shape))


def _in_proj_kernel(x_ref, w_ref, o_ref):
    o_ref[...] = jnp.dot(x_ref[...].astype(BF16), w_ref[...], preferred_element_type=F32)


def in_proj(x2d, w_bf16):
    n = x2d.shape[0]
    tm = min(ROW_TILE, n)
    return pl.pallas_call(
        _in_proj_kernel,
        grid=(n // tm,),
        in_specs=[pl.BlockSpec((tm, D_MODEL), lambda i: (i, 0)), _full((D_MODEL, IN_COLS))],
        out_specs=pl.BlockSpec((tm, IN_COLS), lambda i: (i, 0)),
        out_shape=jax.ShapeDtypeStruct((n, IN_COLS), F32),
        compiler_params=_cparams("parallel"),
        name="in_proj",
    )(x2d, w_bf16)


def _rglru_kernel(xa_ref, ga_ref, buf_ref, h0_ref, cw_ref, cb_ref, wr_ref, br_ref, wi_ref, bi_ref, lam_ref,
                  y_ref, nbuf_ref, hl_ref, xp_ref, hc_ref, as_ref, hs_ref, *, t):
    pad = SUBLANES
    hist = CONV_W - 1

    @pl.when(pl.program_id(1) == 0)
    def _():
        xp_ref[0:pad, :] = jnp.zeros((pad, D_A), F32)
        xp_ref[pad - hist:pad, :] = buf_ref[0]
        hc_ref[...] = h0_ref[0]

    xa = xa_ref[0]
    xp_ref[pad:pad + t, :] = xa
    xc = cb_ref[...] + xa * cw_ref[hist:hist + 1, :]
    for j in range(hist):
        xc = xc + xp_ref[pad - hist + j:pad - hist + j + t, :] * cw_ref[j:j + 1, :]
    tail = xp_ref[pad + t - hist:pad + t, :]
    nbuf_ref[0] = tail
    xp_ref[pad - hist:pad, :] = tail

    xch = xc.astype(BF16)
    r = jax.nn.sigmoid(jnp.dot(xch, wr_ref[...], preferred_element_type=F32) + br_ref[...])
    i = jax.nn.sigmoid(jnp.dot(xch, wi_ref[...], preferred_element_type=F32) + bi_ref[...])
    lam = lam_ref[...]
    softplus_neg_lam = jnp.maximum(-lam, 0.0) + jnp.log(1.0 + jnp.exp(-jnp.abs(lam)))
    log_a = (-RG_C) * r * softplus_neg_lam
    a = jnp.exp(log_a)
    b = jnp.sqrt(1.0 - jnp.exp(2.0 * log_a)) * (i * xc)

    row_in_group = lax.broadcasted_iota(jnp.int32, (t, D_A), 0) & (SUBLANES - 1)
    acc_a, acc_b = a, b
    sh = 1
    while sh < SUBLANES:
        m = row_in_group >= sh
        acc_b = jnp.where(m, acc_a * pltpu.roll(acc_b, sh, 0) + acc_b, acc_b)
        acc_a = jnp.where(m, acc_a * pltpu.roll(acc_a, sh, 0), acc_a)
        sh *= 2
    as_ref[...] = acc_a
    hs_ref[...] = acc_b

    def group(g, carry):
        rows = pl.ds(pl.multiple_of(g * SUBLANES, SUBLANES), SUBLANES)
        h = hs_ref[rows, :] + as_ref[rows, :] * carry
        hs_ref[rows, :] = h
        return h[SUBLANES - 1:SUBLANES, :]

    h_last = lax.fori_loop(0, t // SUBLANES, group, hc_ref[...], unroll=8)
    hc_ref[...] = h_last
    hl_ref[0] = h_last
    y_ref[0] = hs_ref[...] * jax.nn.gelu(ga_ref[0])


def rglru(u, conv_buf, h0, cw, cb, wr_bd, br, wi_bd, bi, lam):
    bn, seq, _ = u.shape
    t = min(ROW_TILE, seq)
    vec = _full((1, D_A))
    return pl.pallas_call(
        functools.partial(_rglru_kernel, t=t),
        grid=(bn, seq // t),
        in_specs=[
            pl.BlockSpec((1, t, D_A), lambda b, s: (b, s, 0)),
            pl.BlockSpec((1, t, D_A), lambda b, s: (b, s, OFF_GA // D_A)),
            pl.BlockSpec((1, CONV_W - 1, D_A), lambda b, s: (b, 0, 0)),
            pl.BlockSpec((1, 1, D_A), lambda b, s: (b, 0, 0)),
            _full((CONV_W, D_A)), vec, _full((D_A, D_A)), vec, _full((D_A, D_A)), vec, vec,
        ],
        out_specs=[
            pl.BlockSpec((1, t, D_A), lambda b, s: (b, s, 0)),
            pl.BlockSpec((1, CONV_W - 1, D_A), lambda b, s: (b, 0, 0)),
            pl.BlockSpec((1, 1, D_A), lambda b, s: (b, 0, 0)),
        ],
        out_shape=[
            jax.ShapeDtypeStruct((bn, seq, D_A), F32),
            jax.ShapeDtypeStruct((bn, CONV_W - 1, D_A), F32),
            jax.ShapeDtypeStruct((bn, 1, D_A), F32),
        ],
        scratch_shapes=[pltpu.VMEM((t + SUBLANES, D_A), F32), pltpu.VMEM((1, D_A), F32),
                        pltpu.VMEM((t, D_A), F32), pltpu.VMEM((t, D_A), F32)],
        compiler_params=_cparams("parallel", "arbitrary"),
        name="rglru",
    )(u, u, conv_buf, h0.reshape(bn, 1, D_A), cw, cb.reshape(1, D_A), wr_bd, br.reshape(1, D_A),
      wi_bd, bi.reshape(1, D_A), lam.reshape(1, D_A))


def _attn_kernel(q_ref, kp_ref, kc_ref, vp_ref, vc_ref, bias_ref, o_ref, *, pb, sq, mask_first_prev):
    scale = B_HEAD_DIM ** -0.5
    qb = q_ref.shape[1]
    q_all = (q_ref[0] * scale).astype(BF16)
    kp_all = kp_ref[0].astype(BF16)
    kc_all = kc_ref[0].astype(BF16)
    vp_all = vp_ref[0].astype(BF16)
    vc_all = vc_ref[0].astype(BF16)
    lane = lax.broadcasted_iota(jnp.int32, (1, LANE), 1)
    contract_last = (((1,), (1,)), ((), ()))
    for r in range(qb // sq):
        lo = max(pb + r * sq - BAND_PAST, 0)
        hi = (r + 1) * sq
        q = q_all[r * sq:hi, :]
        kp, vp = kp_all[lo:pb, :], vp_all[lo:pb, :]
        kc, vc = kc_all[0:hi, :], vc_all[0:hi, :]
        out = None
        for hh in range(HEADS_PER_SLAB):
            in_head = (lane // B_HEAD_DIM) == hh
            qh = jnp.where(in_head, q, jnp.zeros_like(q))
            sp = (lax.dot_general(qh, kp, contract_last, preferred_element_type=F32)
                  + bias_ref[hh, r * sq:hi, lo:pb])
            sc = (lax.dot_general(qh, kc, contract_last, preferred_element_type=F32)
                  + bias_ref[hh, r * sq:hi, pb:pb + hi])
            if mask_first_prev:
                sp = jnp.where(pl.program_id(2) == 0, NEG_INF, sp)
            m = jnp.maximum(jnp.max(sp, axis=-1, keepdims=True), jnp.max(sc, axis=-1, keepdims=True))
            ep = jnp.exp(sp - m)
            ec = jnp.exp(sc - m)
            denom = jnp.sum(ep, axis=-1, keepdims=True) + jnp.sum(ec, axis=-1, keepdims=True)
            o = (jnp.dot(ep.astype(BF16), vp, preferred_element_type=F32)
                 + jnp.dot(ec.astype(BF16), vc, preferred_element_type=F32)) / denom
            out = o if out is None else jnp.where(in_head, o, out)
        o_ref[0, r * sq:hi, :] = out


def attention(q_arr, q_col, kprev_arr, kprev_col, vprev_arr, vprev_col, kv_arr, k_col, v_col, bias, *, qb, pb,
              prev_is_same_array):
    bn, seq, _ = q_arr.shape
    n_slabs = B_HEADS // HEADS_PER_SLAB
    if prev_is_same_array:
        prev_map = lambda col: (lambda hp, b, s: (b, jnp.maximum(s - 1, 0), col + hp))
    else:
        prev_map = lambda col: (lambda hp, b, s: (b, 0, col + hp))
    cur_map = lambda col: (lambda hp, b, s: (b, s, col + hp))
    return pl.pallas_call(
        functools.partial(_attn_kernel, pb=pb, sq=min(qb, ATTN_SUB_ROWS), mask_first_prev=prev_is_same_array),
        grid=(n_slabs, bn, seq // qb),
        in_specs=[
            pl.BlockSpec((1, qb, LANE), cur_map(q_col)),
            pl.BlockSpec((1, pb, LANE), prev_map(kprev_col)),
            pl.BlockSpec((1, qb, LANE), cur_map(k_col)),
            pl.BlockSpec((1, pb, LANE), prev_map(vprev_col)),
            pl.BlockSpec((1, qb, LANE), cur_map(v_col)),
            pl.BlockSpec((HEADS_PER_SLAB, qb, pb + qb), lambda hp, b, s: (hp, 0, 0)),
        ],
        out_specs=pl.BlockSpec((1, qb, LANE), lambda hp, b, s: (b, s, hp)),
        out_shape=jax.ShapeDtypeStruct((bn, seq, D_B), F32),
        compiler_params=_cparams("arbitrary", "arbitrary", "arbitrary"),
        name="attention",
    )(q_arr, kprev_arr, kv_arr, vprev_arr, kv_arr, bias)


def _bias_kernel(v_ref, o_ref, *, band_mask):
    qb, width = o_ref.shape[1], o_ref.shape[2]
    period = v_ref.shape[2]
    table = pltpu.roll(jnp.broadcast_to(v_ref[0], (qb, period)), 0, 1, stride=1, stride_axis=0)[:, :width]
    if band_mask:
        i = lax.broadcasted_iota(jnp.int32, (qb, width), 0)
        j = lax.broadcasted_iota(jnp.int32, (qb, width), 1)
        dc = i // CHUNK - j // CHUNK + PREV_CHUNKS
        table = jnp.where((dc >= 0) & (dc <= PREV_CHUNKS), table, NEG_INF)
    o_ref[0] = table


def bias_table(rel_bias, qb, pb, band_mask):
    n_heads = rel_bias.shape[0]
    period = pb + 2 * qb
    m = np.arange(period)
    d = np.where(m < pb + qb, m, m - period)
    idx = np.clip(pb - d, -REL_CLIP, REL_CLIP) + REL_CLIP
    v = rel_bias.astype(F32)[:, idx].reshape(n_heads, 1, period)
    return pl.pallas_call(
        functools.partial(_bias_kernel, band_mask=band_mask),
        grid=(n_heads,),
        in_specs=[pl.BlockSpec((1, 1, period), lambda h: (h, 0, 0))],
        out_specs=pl.BlockSpec((1, qb, pb + qb), lambda h: (h, 0, 0)),
        out_shape=jax.ShapeDtypeStruct((n_heads, qb, pb + qb), F32),
        compiler_params=_cparams("parallel"),
        name="bias_table",
    )(v)


def _s5_kernel(u_ref, s0_ref, bmat_ref, ar_ref, ai_ref, cmat_ref, d_ref, wg_ref, bg_ref,
               y_ref, sl_ref, sc_ref, xs_ref, *, steps):
    @pl.when(pl.program_id(1) == 0)
    def _():
        sc_ref[...] = s0_ref[0]

    u = u_ref[0]
    xs_ref[...] = jnp.dot(u.astype(BF16), bmat_ref[...], preferred_element_type=F32)
    a_re = ar_ref[...]
    a_im = ai_ref[...]

    def step(t, s):
        rows = pl.ds(pl.multiple_of(t * SUBLANES, SUBLANES), SUBLANES)
        s = xs_ref[rows, :] + a_re * s + a_im * pltpu.roll(s, C_LANES, 1)
        xs_ref[rows, :] = s
        return s

    s_last = lax.fori_loop(0, steps, step, sc_ref[...], unroll=4)
    sc_ref[...] = s_last
    sl_ref[0] = s_last
    y = jnp.dot(xs_ref[...].astype(BF16), cmat_ref[...], preferred_element_type=F32) + d_ref[...] * u
    y = jax.nn.gelu(y)
    gate = jax.nn.sigmoid(jnp.dot(y.astype(BF16), wg_ref[...], preferred_element_type=F32) + bg_ref[...])
    y_ref[0] = y * gate


S5_STEPS = 64


def s5(u, s0, bmat, a_tables, cmat, d_skip, w_glu, b_glu):
    bn, seq, _ = u.shape
    groups = bn // SUBLANES
    steps = min(S5_STEPS, seq)
    rows = steps * SUBLANES
    uc = u[:, :, OFF_C:].reshape(groups, SUBLANES, seq, D_C).transpose(0, 2, 1, 3).reshape(groups, seq * SUBLANES, D_C)
    y_tm, s_last = pl.pallas_call(
        functools.partial(_s5_kernel, steps=steps),
        grid=(groups, seq // steps),
        in_specs=[
            pl.BlockSpec((1, rows, D_C), lambda g, s: (g, s, 0)),
            pl.BlockSpec((1, SUBLANES, 2 * C_LANES), lambda g, s: (g, 0, 0)),
            _full((D_C, 2 * C_LANES)), _full((SUBLANES, 2 * C_LANES)), _full((SUBLANES, 2 * C_LANES)),
            _full((2 * C_LANES, D_C)), _full((1, D_C)), _full((D_C, D_C)), _full((1, D_C)),
        ],
        out_specs=[
            pl.BlockSpec((1, rows, D_C), lambda g, s: (g, s, 0)),
            pl.BlockSpec((1, SUBLANES, 2 * C_LANES), lambda g, s: (g, 0, 0)),
        ],
        out_shape=[
            jax.ShapeDtypeStruct((groups, seq * SUBLANES, D_C), F32),
            jax.ShapeDtypeStruct((groups, SUBLANES, 2 * C_LANES), F32),
        ],
        scratch_shapes=[pltpu.VMEM((SUBLANES, 2 * C_LANES), F32), pltpu.VMEM((rows, 2 * C_LANES), F32)],
        compiler_params=_cparams("parallel", "arbitrary"),
        name="s5",
    )(uc, s0.reshape(groups, SUBLANES, 2 * C_LANES), bmat, *a_tables, cmat, d_skip.reshape(1, D_C), w_glu,
      b_glu.reshape(1, D_C))
    y = y_tm.reshape(groups, seq, SUBLANES, D_C).transpose(0, 2, 1, 3).reshape(bn, seq, D_C)
    return y, s_last.reshape(bn, 1, 2 * C_LANES)


def s5_params(a_re, a_im, log_dt, b_re, b_im, c_re, c_im):
    lam = lax.complex(a_re.astype(F32), a_im.astype(F32))
    dt = jnp.exp(log_dt.astype(F32))[:, None]
    a_bar = jnp.exp(lam * dt)
    b_bar = ((a_bar - 1.0) / lam)[:, :, None] * lax.complex(b_re.astype(F32), b_im.astype(F32))
    eye = jnp.eye(C_GROUPS, dtype=F32)

    def block_in(m):
        return jnp.einsum('gpi,gh->gihp', m, eye).reshape(D_C, C_LANES)

    def block_out(m):
        return jnp.einsum('gip,gh->gphi', m, eye).reshape(C_LANES, D_C)

    bmat = jnp.concatenate([block_in(jnp.real(b_bar)), block_in(jnp.imag(b_bar))], axis=1)
    cmat = jnp.concatenate([block_out(c_re.astype(F32)), -block_out(c_im.astype(F32))], axis=0)
    re = jnp.real(a_bar).reshape(1, C_LANES)
    im = jnp.imag(a_bar).reshape(1, C_LANES)
    a_re2 = jnp.broadcast_to(jnp.concatenate([re, re], axis=1), (SUBLANES, 2 * C_LANES))
    a_im2 = jnp.broadcast_to(jnp.concatenate([-im, im], axis=1), (SUBLANES, 2 * C_LANES))
    return bmat.astype(BF16), (a_re2, a_im2), cmat.astype(BF16)


def _rms(x, g):
    return x * lax.rsqrt(jnp.mean(jnp.square(x), axis=-1, keepdims=True) + RMS_EPS) * g


def _layer_norm(x, g, b):
    mu = jnp.mean(x, axis=-1, keepdims=True)
    xc = x - mu
    var = jnp.mean(jnp.square(xc), axis=-1, keepdims=True)
    return xc * lax.rsqrt(var + LN_EPS) * g + b


def _out_proj_kernel(ya_ref, yb_ref, yc_ref, x_ref, ga_ref, gb_ref, gc_ref, w_ref, lg_ref, lb_ref,
                     wrh_ref, wrl_ref, br_ref,
                     x1_ref, topi_ref, gate_ref, pos_ref, cnt_ref, seen_ref):
    @pl.when(pl.program_id(0) == 0)
    def _():
        seen_ref[...] = jnp.zeros_like(seen_ref)

    m = jnp.dot(_rms(ya_ref[...], ga_ref[...]).astype(BF16), w_ref[0:D_A, :], preferred_element_type=F32)
    m += jnp.dot(_rms(yb_ref[...], gb_ref[...]).astype(BF16), w_ref[D_A:D_A + D_B, :], preferred_element_type=F32)
    m += jnp.dot(_rms(yc_ref[...], gc_ref[...]).astype(BF16), w_ref[D_A + D_B:, :], preferred_element_type=F32)
    x1 = _layer_norm(DEEPNORM_ALPHA * x_ref[...] + m, lg_ref[...], lb_ref[...])
    x1_ref[...] = x1
    x1h = x1.astype(BF16)

    x1l = (x1 - x1h.astype(F32)).astype(BF16)
    logits = (jnp.dot(x1h, wrh_ref[...], preferred_element_type=F32)
              + jnp.dot(x1l, wrh_ref[...], preferred_element_type=F32)
              + jnp.dot(x1h, wrl_ref[...], preferred_element_type=F32)) + br_ref[...]

    tm = logits.shape[0]
    col = lax.broadcasted_iota(jnp.int32, (tm, N_EXPERTS), 1).astype(F32)
    work = logits
    sels, vals, idxs = [], [], []
    for _ in range(TOP_K):
        top = jnp.max(work, axis=1, keepdims=True)
        idx = jnp.min(jnp.where(work == top, col, float(N_EXPERTS)), axis=1, keepdims=True)
        sel = col == idx
        work = jnp.where(sel, -jnp.inf, work)
        sels.append(sel)
        vals.append(top)
        idxs.append(idx)
    exps = [jnp.exp(v - vals[0]) for v in vals]
    total = functools.reduce(lambda a, b: a + b, exps)

    chosen = functools.reduce(lambda a, b: a + b, [s.astype(F32) for s in sels])
    ri = lax.broadcasted_iota(jnp.int32, (tm, tm), 0)
    ci = lax.broadcasted_iota(jnp.int32, (tm, tm), 1)
    earlier = jnp.where(ci < ri, 1.0, 0.0).astype(BF16)
    before = jnp.dot(earlier, chosen.astype(BF16), preferred_element_type=F32) + seen_ref[...]
    seen = seen_ref[...] + jnp.sum(chosen, axis=0, keepdims=True)
    seen_ref[...] = seen
    cnt_ref[...] = seen.astype(jnp.int32)

    slot = lax.broadcasted_iota(jnp.int32, (tm, TOP_K), 1)
    top_i = jnp.zeros((tm, TOP_K), F32)
    gates = jnp.zeros((tm, TOP_K), F32)
    pos = jnp.zeros((tm, TOP_K), F32)
    for k in range(TOP_K):
        rank = jnp.sum(jnp.where(sels[k], before, 0.0), axis=1, keepdims=True)
        top_i = jnp.where(slot == k, idxs[k], top_i)
        gates = jnp.where(slot == k, exps[k] / total, gates)
        pos = jnp.where(slot == k, rank, pos)
    topi_ref[...] = top_i.astype(jnp.int32)
    gate_ref[...] = gates
    pos_ref[...] = pos.astype(jnp.int32)


def out_proj(ya, yb, yc, x2d, g_a, g_b, g_c, w_out_bf16, ln_g, ln_b, w_router, b_router):
    n = x2d.shape[0]
    tm = min(ROW_TILE, n)
    rows = lambda w: pl.BlockSpec((tm, w), lambda i: (i, 0))
    wr = w_router.astype(F32)
    wr_hi = wr.astype(BF16)
    wr_lo = (wr - wr_hi.astype(F32)).astype(BF16)
    return pl.pallas_call(
        _out_proj_kernel,
        grid=(n // tm,),
        in_specs=[rows(D_A), rows(D_B), rows(D_C), rows(D_MODEL),
                  _full((1, D_A)), _full((1, D_B)), _full((1, D_C)), _full((D_MODEL, D_MODEL)),
                  _full((1, D_MODEL)), _full((1, D_MODEL)),
                  _full((D_MODEL, N_EXPERTS)), _full((D_MODEL, N_EXPERTS)), _full((1, N_EXPERTS))],
        out_specs=[rows(D_MODEL), rows(TOP_K), rows(TOP_K), rows(TOP_K), _full((1, N_EXPERTS))],
        out_shape=[jax.ShapeDtypeStruct((n, D_MODEL), F32),
                   jax.ShapeDtypeStruct((n, TOP_K), jnp.int32), jax.ShapeDtypeStruct((n, TOP_K), F32),
                   jax.ShapeDtypeStruct((n, TOP_K), jnp.int32), jax.ShapeDtypeStruct((1, N_EXPERTS), jnp.int32)],
        scratch_shapes=[pltpu.VMEM((1, N_EXPERTS), F32)],
        compiler_params=_cparams("arbitrary"),
        name="out_proj",
    )(ya, yb, yc, x2d, g_a.reshape(1, D_A), g_b.reshape(1, D_B), g_c.reshape(1, D_C), w_out_bf16,
      ln_g.reshape(1, D_MODEL), ln_b.reshape(1, D_MODEL), wr_hi, wr_lo, b_router.reshape(1, N_EXPERTS))


CAST_ROWS = 128


def _cast_weight(src_ref, dst_ref):
    def body(c, carry):
        r = pl.multiple_of(c * CAST_ROWS, CAST_ROWS)
        dst_ref[pl.ds(r, CAST_ROWS), :] = src_ref[0, 0, pl.ds(r, CAST_ROWS), :].astype(BF16)
        return carry

    lax.fori_loop(0, dst_ref.shape[0] // CAST_ROWS, body, 0)


def _moe_kernel(blk_e_ref, n_used_ref, tok_ref, tok_next_ref, x_ref, wgu_ref, bgu_ref, wdn_ref, bdn_ref, o_ref,
                wgu_s, wdn_s, xbuf_ref, xh_ref, sem_ref):
    i = pl.program_id(0)
    n_used = n_used_ref[0]
    slot = i % 2
    expert_changed = (i == 0) | (blk_e_ref[i] != blk_e_ref[jnp.maximum(i - 1, 0)])

    def row_copy(tok, r, s):
        return pltpu.make_async_copy(x_ref.at[pl.ds(tok, 1), :], xbuf_ref.at[s, pl.ds(r, 1), :], sem_ref.at[s])

    @pl.when(i == 0)
    def _():
        def body(r, carry):
            row_copy(tok_ref[0, 0, r], r, 0).start()
            return carry

        lax.fori_loop(0, MOE_ROWS, body, 0, unroll=8)

    @pl.when(expert_changed)
    def _():
        _cast_weight(wgu_ref, wgu_s)
        _cast_weight(wdn_ref, wdn_s)

    @pl.when(i <= n_used)
    def _():
        pltpu.make_async_copy(x_ref.at[pl.ds(0, MOE_ROWS), :], xbuf_ref.at[slot], sem_ref.at[slot]).wait()

    @pl.when(i < n_used)
    def _():
        xh_ref[...] = xbuf_ref[slot].astype(BF16)
        cw = D_FF // MOE_CHUNKS
        per_chunk = MOE_ROWS // MOE_START_CHUNKS
        acc = None
        for c in range(MOE_CHUNKS):
            for r in range(min(c, MOE_START_CHUNKS) * per_chunk, min(c + 1, MOE_START_CHUNKS) * per_chunk):
                row_copy(tok_next_ref[0, 0, r], r, 1 - slot).start(priority=r % DMA_THREADS)
            words = pltpu.bitcast(xbuf_ref[slot, 0:SUBLANES, 0:cw], jnp.uint32)
            zero = ((words >> 16) >> 16).astype(F32)[0:1, :]
            xh = xh_ref[...]
            hg = (jnp.dot(xh, wgu_s[:, c * cw:(c + 1) * cw], preferred_element_type=F32)
                  + (bgu_ref[0, 0, :, c * cw:(c + 1) * cw] + zero))
            hu = (jnp.dot(xh, wgu_s[:, D_FF + c * cw:D_FF + (c + 1) * cw], preferred_element_type=F32)
                  + bgu_ref[0, 0, :, D_FF + c * cw:D_FF + (c + 1) * cw])
            gate = jnp.minimum(hg, SWIGLU_LIMIT)
            up = jnp.clip(hu, -SWIGLU_LIMIT, SWIGLU_LIMIT)
            glu = gate * jax.nn.sigmoid(gate * SWIGLU_ALPHA)
            act = ((up + 1.0) * glu).astype(BF16)
            part = jnp.dot(act, wdn_s[c * cw:(c + 1) * cw, :], preferred_element_type=F32)
            acc = part if acc is None else acc + part
        o_ref[...] = acc + bdn_ref[0, 0]

    @pl.when(i >= n_used)
    def _():
        o_ref[...] = jnp.zeros_like(o_ref)


def moe_experts(x, row_tok, blk_e, n_used, layer, w_gu, b_gu, w_dn, b_dn):
    n_rows = row_tok.shape[0]
    n_blocks = n_rows // MOE_ROWS
    depth = w_gu.shape[0]
    tok_blocks = row_tok.reshape(n_blocks, 1, MOE_ROWS)
    tok_spec = lambda index_map: pl.BlockSpec((1, 1, MOE_ROWS), index_map, memory_space=pltpu.SMEM)
    grid_spec = pltpu.PrefetchScalarGridSpec(
        num_scalar_prefetch=2,
        grid=(n_blocks,),
        in_specs=[
            tok_spec(lambda i, be, nu: (i, 0, 0)),
            tok_spec(lambda i, be, nu: (jnp.minimum(i + 1, n_blocks - 1), 0, 0)),
            pl.BlockSpec(memory_space=pl.ANY),
            pl.BlockSpec((1, 1, D_MODEL, 2 * D_FF), lambda i, be, nu: (layer, be[i], 0, 0)),
            pl.BlockSpec((1, 1, 1, 2 * D_FF), lambda i, be, nu: (layer, be[i], 0, 0)),
            pl.BlockSpec((1, 1, D_FF, D_MODEL), lambda i, be, nu: (layer, be[i], 0, 0)),
            pl.BlockSpec((1, 1, 1, D_MODEL), lambda i, be, nu: (layer, be[i], 0, 0)),
        ],
        out_specs=pl.BlockSpec((MOE_ROWS, D_MODEL), lambda i, be, nu: (i, 0)),
        scratch_shapes=[pltpu.VMEM((D_MODEL, 2 * D_FF), BF16), pltpu.VMEM((D_FF, D_MODEL), BF16),
                        pltpu.VMEM((2, MOE_ROWS, D_MODEL), F32), pltpu.VMEM((MOE_ROWS, D_MODEL), BF16),
                        pltpu.SemaphoreType.DMA((2,))],
    )
    return pl.pallas_call(
        _moe_kernel,
        grid_spec=grid_spec,
        out_shape=jax.ShapeDtypeStruct((n_rows, D_MODEL), F32),
        compiler_params=_cparams("arbitrary"),
        name="moe_experts",
    )(blk_e, n_used, tok_blocks, tok_blocks, x, w_gu, b_gu.reshape(depth, N_EXPERTS, 1, 2 * D_FF), w_dn,
      b_dn.reshape(depth, N_EXPERTS, 1, D_MODEL))


def route(top_i, pos, counts):
    n_tok = top_i.shape[0]
    n_slots = n_tok * TOP_K
    padded = ((counts + MOE_ROWS - 1) // MOE_ROWS) * MOE_ROWS
    pad_end = jnp.cumsum(padded)
    pad_start = pad_end - padded
    start = jnp.cumsum(counts) - counts
    experts = jnp.arange(N_EXPERTS, dtype=jnp.int32)
    slot_start = jnp.sum(jnp.where(top_i[:, :, None] == experts, pad_start, 0), axis=-1)
    dest = (slot_start + pos).astype(jnp.int32).T
    n_blocks = -(-n_slots // MOE_ROWS) + N_EXPERTS
    blk_first = jnp.arange(n_blocks, dtype=jnp.int32) * MOE_ROWS
    blk_e = jnp.minimum(jnp.sum((pad_end[None, :] <= blk_first[:, None]).astype(jnp.int32), axis=1), N_EXPERTS - 1)
    order = jnp.argsort(top_i.reshape(-1)).astype(jnp.int32)
    within = jnp.arange(MOE_ROWS, dtype=jnp.int32)[None, :] + (blk_first - pad_start[blk_e])[:, None]
    src = jnp.clip(start[blk_e][:, None] + within, 0, n_slots - 1).reshape(-1)
    valid = (within < counts[blk_e][:, None]).reshape(-1)
    row_tok = jnp.where(valid, order[src] // TOP_K, 0).astype(jnp.int32)
    n_used = (pad_end[-1:] // MOE_ROWS).astype(jnp.int32)
    return dest, row_tok, blk_e.astype(jnp.int32), n_used


COMBINE_ROWS = 256


def _combine_ln_kernel(dcur_ref, dnext_ref, x_ref, gate_ref, g_ref, b_ref, yb_ref, o_ref, buf_ref, sem_ref, *,
                       n_tiles):
    i = pl.program_id(0)
    tm = x_ref.shape[0]
    n_rows = TOP_K * tm
    slot = i % 2

    def row_copy(row, r, s):
        return pltpu.make_async_copy(yb_ref.at[pl.ds(row, 1), :], buf_ref.at[s, pl.ds(r, 1), :], sem_ref.at[s])

    def start_gather(d_ref, s):
        def body(r, carry):
            row_copy(d_ref[0, 0, r], r, s).start()
            return carry

        lax.fori_loop(0, n_rows, body, 0, unroll=8)

    @pl.when(i == 0)
    def _():
        start_gather(dcur_ref, 0)

    @pl.when(i + 1 < n_tiles)
    def _():
        for r in range(n_rows):
            row_copy(dnext_ref[0, 0, r], r, 1 - slot).start(priority=r % DMA_THREADS)

    pltpu.make_async_copy(yb_ref.at[pl.ds(0, n_rows), :], buf_ref.at[slot], sem_ref.at[slot]).wait()
    gates = gate_ref[...]
    moe = buf_ref[slot, 0:tm, :] * gates[:, 0:1]
    for k in range(1, TOP_K):
        moe = moe + buf_ref[slot, k * tm:(k + 1) * tm, :] * gates[:, k:k + 1]
    o_ref[...] = _layer_norm(DEEPNORM_ALPHA * x_ref[...] + moe, g_ref[...], b_ref[...])


def combine_ln(x2d, yb, dest, gates, ln_g, ln_b):
    n = x2d.shape[0]
    tm = min(COMBINE_ROWS, n)
    n_tiles = n // tm
    rows = pl.BlockSpec((tm, D_MODEL), lambda i: (i, 0))
    dest_tiles = dest.reshape(TOP_K, n_tiles, tm).transpose(1, 0, 2).reshape(n_tiles, 1, TOP_K * tm)
    dest_spec = lambda index_map: pl.BlockSpec((1, 1, TOP_K * tm), index_map, memory_space=pltpu.SMEM)
    return pl.pallas_call(
        functools.partial(_combine_ln_kernel, n_tiles=n_tiles),
        grid=(n_tiles,),
        in_specs=[dest_spec(lambda i: (i, 0, 0)), dest_spec(lambda i: (jnp.minimum(i + 1, n_tiles - 1), 0, 0)),
                  rows, pl.BlockSpec((tm, TOP_K), lambda i: (i, 0)), _full((1, D_MODEL)), _full((1, D_MODEL)),
                  pl.BlockSpec(memory_space=pl.ANY)],
        out_specs=rows,
        out_shape=jax.ShapeDtypeStruct((n, D_MODEL), F32),
        scratch_shapes=[pltpu.VMEM((2, TOP_K * tm, D_MODEL), F32), pltpu.SemaphoreType.DMA((2,))],
        compiler_params=_cparams("arbitrary"),
        name="combine_ln",
    )(dest_tiles, dest_tiles, x2d, gates, ln_g.reshape(1, D_MODEL), ln_b.reshape(1, D_MODEL), yb)


def _block_diag(w):
    h, d, _ = w.shape
    return jnp.einsum('hij,hg->higj', w.astype(F32), jnp.eye(h, dtype=F32)).reshape(h * d, h * d)


def _mixer(x, state, p, attn_bias):
    bn, seq, _ = x.shape
    n = bn * seq
    conv_buf, h0, s0, k_cache, v_cache = state
    x2d = x.reshape(n, D_MODEL)
    u = in_proj(x2d, p['w_in']).reshape(bn, seq, IN_COLS)

    y_a, new_buf, h_last = rglru(u, conv_buf, h0, p['conv_w'], p['conv_b'], p['w_r'], p['b_r'], p['w_i'], p['b_i'],
                                 p['lam'])
    q_col, k_col, v_col = OFF_Q // LANE, OFF_K // LANE, OFF_V // LANE
    if k_cache is None:
        qb = min(BAND_PAST, seq)
        y_b = attention(u, q_col, u, k_col, u, v_col, u, k_col, v_col, attn_bias, qb=qb, pb=qb,
                        prev_is_same_array=True)
    else:
        y_b = attention(u, q_col, k_cache, 0, v_cache, 0, u, k_col, v_col, attn_bias, qb=seq,
                        pb=k_cache.shape[1], prev_is_same_array=False)
    y_c, s_last = s5(u, s0, p['bmat'], p['scan_tables'], p['cmat'], p['d_c'], p['w_glu'], p['b_glu'])

    x1, top_i, gates, pos, counts = out_proj(
        y_a.reshape(n, D_A), y_b.reshape(n, D_B), y_c.reshape(n, D_C), x2d, p['g_a'], p['g_b'], p['g_c'],
        p['w_out'], p['ln1_g'], p['ln1_b'], p['w_router'], p['b_router'])

    k_rows = u[:, :, OFF_K:OFF_V]
    v_rows = u[:, :, OFF_V:OFF_C]
    if k_cache is None:
        keep = min(BAND_PAST, seq)
        k_rows = k_rows[:, seq - keep:]
        v_rows = v_rows[:, seq - keep:]
    k_rows = k_rows.reshape(bn, -1, B_HEADS, B_HEAD_DIM)
    v_rows = v_rows.reshape(bn, -1, B_HEADS, B_HEAD_DIM)
    s_re = s_last[:, 0, :C_LANES].reshape(bn, C_GROUPS, C_STATE)
    s_im = s_last[:, 0, C_LANES:].reshape(bn, C_GROUPS, C_STATE)
    routed = dict(x1=x1, top_i=top_i, gates=gates, pos=pos, counts=counts.reshape(N_EXPERTS))
    return routed, (new_buf, h_last.reshape(bn, D_A), k_rows, v_rows, s_re, s_im)


def _moe(groups, p):
    experts = jnp.arange(N_EXPERTS, dtype=jnp.int32)
    seen = jnp.zeros((N_EXPERTS,), jnp.int32)
    pos_all = []
    for g in groups:
        pos_all.append(g['pos'] + jnp.sum(jnp.where(g['top_i'][:, :, None] == experts, seen, 0), axis=-1))
        seen = seen + g['counts']
    top_i = jnp.concatenate([g['top_i'] for g in groups], axis=0)
    dest, row_tok, blk_e, n_used = route(top_i, jnp.concatenate(pos_all, axis=0), seen)
    x1 = jnp.concatenate([g['x1'] for g in groups], axis=0)
    yb = moe_experts(x1, row_tok, blk_e, n_used, p['layer'], p['w_gu'], p['b_gu'], p['w_dn'], p['b_dn'])
    outs, first = [], 0
    for g in groups:
        n = g['x1'].shape[0]
        outs.append(combine_ln(g['x1'], yb, dest[:, first:first + n], g['gates'], p['ln2_g'], p['ln2_b']))
        first += n
    return outs


def kernel(x_prompt, x_sample, cache_conv_a, state_h_a, cache_k_b, cache_v_b, state_s_re_c, state_s_im_c, w_in, conv_w_a, conv_b_a, w_r_a, b_r_a, w_i_a, b_i_a, lambda_a, rel_bias_b, a_re_c, a_im_c, log_dt_c, b_re_c, b_im_c, c_re_c, c_im_c, d_c, w_glu_c, b_glu_c, g_norm_a, g_norm_b, g_norm_c, w_out, ln1_g, ln1_b, w_router, b_router, w_gu, b_gu, w_dn, b_dn, ln2_g, ln2_b):
    bp, seq_p, _ = x_prompt.shape
    bs, seq_s, _ = x_sample.shape
    kv_rows = cache_k_b.shape[2]
    yp, ys = x_prompt, x_sample
    p_states = [[] for _ in range(6)]
    s_states = [[] for _ in range(6)]
    for l in range(DEPTH):
        bmat, scan_tables, cmat = s5_params(a_re_c[l], a_im_c[l], log_dt_c[l], b_re_c[l], b_im_c[l], c_re_c[l],
                                           c_im_c[l])
        p = dict(
            w_in=w_in[l].astype(BF16), conv_w=conv_w_a[l], conv_b=conv_b_a[l],
            w_r=_block_diag(w_r_a[l]).astype(BF16), b_r=b_r_a[l], w_i=_block_diag(w_i_a[l]).astype(BF16),
            b_i=b_i_a[l], lam=lambda_a[l],
            bmat=bmat, scan_tables=scan_tables, cmat=cmat, d_c=d_c[l], w_glu=w_glu_c[l].astype(BF16), b_glu=b_glu_c[l],
            g_a=g_norm_a[l], g_b=g_norm_b[l], g_c=g_norm_c[l], w_out=w_out[l].astype(BF16),
            ln1_g=ln1_g[l], ln1_b=ln1_b[l], w_router=w_router[l], b_router=b_router[l],
            layer=l, w_gu=w_gu, b_gu=b_gu, w_dn=w_dn, b_dn=b_dn,
            ln2_g=ln2_g[l], ln2_b=ln2_b[l],
        )
        zero_state = (jnp.zeros((bp, CONV_W - 1, D_A), F32), jnp.zeros((bp, D_A), F32),
                      jnp.zeros((bp, 1, 2 * C_LANES), F32), None, None)
        qb = min(BAND_PAST, seq_p)
        routed_p, st = _mixer(yp, zero_state, p, bias_table(rel_bias_b[l], qb, qb, band_mask=True))
        for lst, s in zip(p_states, st):
            lst.append(s)
        s0 = jnp.concatenate([state_s_re_c[l].reshape(bs, 1, C_LANES), state_s_im_c[l].reshape(bs, 1, C_LANES)],
                             axis=-1)
        sample_state = (cache_conv_a[l], state_h_a[l], s0,
                        cache_k_b[l].reshape(bs, kv_rows, D_B), cache_v_b[l].reshape(bs, kv_rows, D_B))
        routed_s, st = _mixer(ys, sample_state, p, bias_table(rel_bias_b[l], seq_s, kv_rows, band_mask=False))
        for lst, s in zip(s_states, st):
            lst.append(s)
        yp, ys = _moe([routed_p, routed_s], p)
        yp = yp.reshape(bp, seq_p, D_MODEL)
        ys = ys.reshape(bs, seq_s, D_MODEL)
    return (yp, ys) + tuple(jnp.stack(s) for s in p_states) + tuple(jnp.stack(s) for s in s_states)
```

```python
import functools
import math

import jax
import jax.numpy as jnp
import numpy as np
from jax import lax
from jax.experimental import pallas as pl
from jax.experimental.pallas import tpu as pltpu

F32 = jnp.float32
BF16 = jnp.bfloat16

D_MODEL = 1024
DEPTH = 2
CHUNK = 64
PREV_CHUNKS = 8
BAND_PAST = PREV_CHUNKS * CHUNK
D_A = D_MODEL // 4
D_B = D_MODEL // 2
D_C = D_MODEL // 4
A_HEADS = 4
A_HEAD_DIM = D_A // A_HEADS
CONV_W = 4
RG_C = 8.0
B_HEADS = 8
B_HEAD_DIM = D_B // B_HEADS
REL_CLIP = 128
C_GW = 16
C_GROUPS = D_C // C_GW
C_STATE = 64
C_LANES = C_GROUPS * C_STATE
OFF_GA = D_A
OFF_Q = 2 * D_A
OFF_K = OFF_Q + D_B
OFF_V = OFF_K + D_B
OFF_C = OFF_V + D_B
IN_COLS = OFF_C + D_C
N_EXPERTS = 32
TOP_K = 4
D_FF = D_MODEL
SWIGLU_LIMIT = 7.0
SWIGLU_ALPHA = 1.702
DEEPNORM_ALPHA = (2 * DEPTH) ** 0.25
LN_EPS = 1e-5
RMS_EPS = 1e-6
NEG_INF = -1e30

LANE = 128
SUBLANES = 8
HEADS_PER_SLAB = LANE // B_HEAD_DIM
VMEM_LIMIT = 56 * 1024 * 1024

ROW_TILE = 512
MOE_ROWS = 512
MOE_CHUNKS = 4
DMA_THREADS = 2
MOE_START_CHUNKS = 4
ATTN_SUB_ROWS = 256


def _cparams(*sem):
    return pltpu.CompilerParams(dimension_semantics=sem, vmem_limit_bytes=VMEM_LIMIT)


def _full(shape):
    return pl.BlockSpec(shape, lambda *_: (0,) * len(shape))


def _in_proj_kernel(x_ref, w_ref, o_ref):
    o_ref[...] = jnp.dot(x_ref[...].astype(BF16), w_ref[...], preferred_element_type=F32)


def in_proj(x2d, w_bf16):
    n = x2d.shape[0]
    tm = min(ROW_TILE, n)
    return pl.pallas_call(
        _in_proj_kernel,
        grid=(n // tm,),
        in_specs=[pl.BlockSpec((tm, D_MODEL), lambda i: (i, 0)), _full((D_MODEL, IN_COLS))],
        out_specs=pl.BlockSpec((tm, IN_COLS), lambda i: (i, 0)),
        out_shape=jax.ShapeDtypeStruct((n, IN_COLS), F32),
        compiler_params=_cparams("parallel"),
        name="in_proj",
    )(x2d, w_bf16)


def _rglru_kernel(xa_ref, ga_ref, buf_ref, h0_ref, cw_ref, cb_ref, wr_ref, br_ref, wi_ref, bi_ref, lam_ref,
                  y_ref, nbuf_ref, hl_ref, xp_ref, hc_ref, as_ref, hs_ref, *, t):
    pad = SUBLANES
    hist = CONV_W - 1

    @pl.when(pl.program_id(1) == 0)
    def _():
        xp_ref[0:pad, :] = jnp.zeros((pad, D_A), F32)
        xp_ref[pad - hist:pad, :] = buf_ref[0]
        hc_ref[...] = h0_ref[0]

    xa = xa_ref[0]
    xp_ref[pad:pad + t, :] = xa
    xc = cb_ref[...] + xa * cw_ref[hist:hist + 1, :]
    for j in range(hist):
        xc = xc + xp_ref[pad - hist + j:pad - hist + j + t, :] * cw_ref[j:j + 1, :]
    tail = xp_ref[pad + t - hist:pad + t, :]
    nbuf_ref[0] = tail
    xp_ref[pad - hist:pad, :] = tail

    xch = xc.astype(BF16)
    r = jax.nn.sigmoid(jnp.dot(xch, wr_ref[...], preferred_element_type=F32) + br_ref[...])
    i = jax.nn.sigmoid(jnp.dot(xch, wi_ref[...], preferred_element_type=F32) + bi_ref[...])
    lam = lam_ref[...]
    softplus_neg_lam = jnp.maximum(-lam, 0.0) + jnp.log(1.0 + jnp.exp(-jnp.abs(lam)))
    log_a = (-RG_C) * r * softplus_neg_lam
    a = jnp.exp(log_a)
    b = jnp.sqrt(1.0 - jnp.exp(2.0 * log_a)) * (i * xc)

    row_in_group = lax.broadcasted_iota(jnp.int32, (t, D_A), 0) & (SUBLANES - 1)
    acc_a, acc_b = a, b
    sh = 1
    while sh < SUBLANES:
        m = row_in_group >= sh
        acc_b = jnp.where(m, acc_a * pltpu.roll(acc_b, sh, 0) + acc_b, acc_b)
        acc_a = jnp.where(m, acc_a * pltpu.roll(acc_a, sh, 0), acc_a)
        sh *= 2
    as_ref[...] = acc_a
    hs_ref[...] = acc_b

    def group(g, carry):
        rows = pl.ds(pl.multiple_of(g * SUBLANES, SUBLANES), SUBLANES)
        h = hs_ref[rows, :] + as_ref[rows, :] * carry
        hs_ref[rows, :] = h
        return h[SUBLANES - 1:SUBLANES, :]

    h_last = lax.fori_loop(0, t // SUBLANES, group, hc_ref[...], unroll=8)
    hc_ref[...] = h_last
    hl_ref[0] = h_last
    y_ref[0] = hs_ref[...] * jax.nn.gelu(ga_ref[0])


def rglru(u, conv_buf, h0, cw, cb, wr_bd, br, wi_bd, bi, lam):
    bn, seq, _ = u.shape
    t = min(ROW_TILE, seq)
    vec = _full((1, D_A))
    return pl.pallas_call(
        functools.partial(_rglru_kernel, t=t),
        grid=(bn, seq // t),
        in_specs=[
            pl.BlockSpec((1, t, D_A), lambda b, s: (b, s, 0)),
            pl.BlockSpec((1, t, D_A), lambda b, s: (b, s, OFF_GA // D_A)),
            pl.BlockSpec((1, CONV_W - 1, D_A), lambda b, s: (b, 0, 0)),
            pl.BlockSpec((1, 1, D_A), lambda b, s: (b, 0, 0)),
            _full((CONV_W, D_A)), vec, _full((D_A, D_A)), vec, _full((D_A, D_A)), vec, vec,
        ],
        out_specs=[
            pl.BlockSpec((1, t, D_A), lambda b, s: (b, s, 0)),
            pl.BlockSpec((1, CONV_W - 1, D_A), lambda b, s: (b, 0, 0)),
            pl.BlockSpec((1, 1, D_A), lambda b, s: (b, 0, 0)),
        ],
        out_shape=[
            jax.ShapeDtypeStruct((bn, seq, D_A), F32),
            jax.ShapeDtypeStruct((bn, CONV_W - 1, D_A), F32),
            jax.ShapeDtypeStruct((bn, 1, D_A), F32),
        ],
        scratch_shapes=[pltpu.VMEM((t + SUBLANES, D_A), F32), pltpu.VMEM((1, D_A), F32),
                        pltpu.VMEM((t, D_A), F32), pltpu.VMEM((t, D_A), F32)],
        compiler_params=_cparams("parallel", "arbitrary"),
        name="rglru",
    )(u, u, conv_buf, h0.reshape(bn, 1, D_A), cw, cb.reshape(1, D_A), wr_bd, br.reshape(1, D_A),
      wi_bd, bi.reshape(1, D_A), lam.reshape(1, D_A))


def _attn_kernel(q_ref, kp_ref, kc_ref, vp_ref, vc_ref, bias_ref, o_ref, *, pb, sq, mask_first_prev):
    scale = B_HEAD_DIM ** -0.5
    qb = q_ref.shape[1]
    q_all = (q_ref[0] * scale).astype(BF16)
    kp_all = kp_ref[0].astype(BF16)
    kc_all = kc_ref[0].astype(BF16)
    vp_all = vp_ref[0].astype(BF16)
    vc_all = vc_ref[0].astype(BF16)
    lane = lax.broadcasted_iota(jnp.int32, (1, LANE), 1)
    contract_last = (((1,), (1,)), ((), ()))
    for r in range(qb // sq):
        lo = max(pb + r * sq - BAND_PAST, 0)
        hi = (r + 1) * sq
        q = q_all[r * sq:hi, :]
        kp, vp = kp_all[lo:pb, :], vp_all[lo:pb, :]
        kc, vc = kc_all[0:hi, :], vc_all[0:hi, :]
        out = None
        for hh in range(HEADS_PER_SLAB):
            in_head = (lane // B_HEAD_DIM) == hh
            qh = jnp.where(in_head, q, jnp.zeros_like(q))
            sp = (lax.dot_general(qh, kp, contract_last, preferred_element_type=F32)
                  + bias_ref[hh, r * sq:hi, lo:pb])
            sc = (lax.dot_general(qh, kc, contract_last, preferred_element_type=F32)
                  + bias_ref[hh, r * sq:hi, pb:pb + hi])
            if mask_first_prev:
                sp = jnp.where(pl.program_id(2) == 0, NEG_INF, sp)
            m = jnp.maximum(jnp.max(sp, axis=-1, keepdims=True), jnp.max(sc, axis=-1, keepdims=True))
            ep = jnp.exp(sp - m)
            ec = jnp.exp(sc - m)
            denom = jnp.sum(ep, axis=-1, keepdims=True) + jnp.sum(ec, axis=-1, keepdims=True)
            o = (jnp.dot(ep.astype(BF16), vp, preferred_element_type=F32)
                 + jnp.dot(ec.astype(BF16), vc, preferred_element_type=F32)) / denom
            out = o if out is None else jnp.where(in_head, o, out)
        o_ref[0, r * sq:hi, :] = out


def attention(q_arr, q_col, kprev_arr, kprev_col, vprev_arr, vprev_col, kv_arr, k_col, v_col, bias, *, qb, pb,
              prev_is_same_array):
    bn, seq, _ = q_arr.shape
    n_slabs = B_HEADS // HEADS_PER_SLAB
    if prev_is_same_array:
        prev_map = lambda col: (lambda hp, b, s: (b, jnp.maximum(s - 1, 0), col + hp))
    else:
        prev_map = lambda col: (lambda hp, b, s: (b, 0, col + hp))
    cur_map = lambda col: (lambda hp, b, s: (b, s, col + hp))
    return pl.pallas_call(
        functools.partial(_attn_kernel, pb=pb, sq=min(qb, ATTN_SUB_ROWS), mask_first_prev=prev_is_same_array),
        grid=(n_slabs, bn, seq // qb),
        in_specs=[
            pl.BlockSpec((1, qb, LANE), cur_map(q_col)),
            pl.BlockSpec((1, pb, LANE), prev_map(kprev_col)),
            pl.BlockSpec((1, qb, LANE), cur_map(k_col)),
            pl.BlockSpec((1, pb, LANE), prev_map(vprev_col)),
            pl.BlockSpec((1, qb, LANE), cur_map(v_col)),
            pl.BlockSpec((HEADS_PER_SLAB, qb, pb + qb), lambda hp, b, s: (hp, 0, 0)),
        ],
        out_specs=pl.BlockSpec((1, qb, LANE), lambda hp, b, s: (b, s, hp)),
        out_shape=jax.ShapeDtypeStruct((bn, seq, D_B), F32),
        compiler_params=_cparams("arbitrary", "arbitrary", "arbitrary"),
        name="attention",
    )(q_arr, kprev_arr, kv_arr, vprev_arr, kv_arr, bias)


def _bias_kernel(v_ref, o_ref, *, band_mask):
    qb, width = o_ref.shape[1], o_ref.shape[2]
    period = v_ref.shape[2]
    table = pltpu.roll(jnp.broadcast_to(v_ref[0], (qb, period)), 0, 1, stride=1, stride_axis=0)[:, :width]
    if band_mask:
        i = lax.broadcasted_iota(jnp.int32, (qb, width), 0)
        j = lax.broadcasted_iota(jnp.int32, (qb, width), 1)
        dc = i // CHUNK - j // CHUNK + PREV_CHUNKS
        table = jnp.where((dc >= 0) & (dc <= PREV_CHUNKS), table, NEG_INF)
    o_ref[0] = table


def bias_table(rel_bias, qb, pb, band_mask):
    n_heads = rel_bias.shape[0]
    period = pb + 2 * qb
    m = np.arange(period)
    d = np.where(m < pb + qb, m, m - period)
    idx = np.clip(pb - d, -REL_CLIP, REL_CLIP) + REL_CLIP
    v = rel_bias.astype(F32)[:, idx].reshape(n_heads, 1, period)
    return pl.pallas_call(
        functools.partial(_bias_kernel, band_mask=band_mask),
        grid=(n_heads,),
        in_specs=[pl.BlockSpec((1, 1, period), lambda h: (h, 0, 0))],
        out_specs=pl.BlockSpec((1, qb, pb + qb), lambda h: (h, 0, 0)),
        out_shape=jax.ShapeDtypeStruct((n_heads, qb, pb + qb), F32),
        compiler_params=_cparams("parallel"),
        name="bias_table",
    )(v)


def _s5_kernel(u_ref, s0_ref, bmat_ref, ar_ref, ai_ref, cmat_ref, d_ref, wg_ref, bg_ref,
               y_ref, sl_ref, sc_ref, xs_ref, *, steps):
    @pl.when(pl.program_id(1) == 0)
    def _():
        sc_ref[...] = s0_ref[0]

    u = u_ref[0]
    xs_ref[...] = jnp.dot(u.astype(BF16), bmat_ref[...], preferred_element_type=F32)
    a_re = ar_ref[...]
    a_im = ai_ref[...]

    def step(t, s):
        rows = pl.ds(pl.multiple_of(t * SUBLANES, SUBLANES), SUBLANES)
        s = xs_ref[rows, :] + a_re * s + a_im * pltpu.roll(s, C_LANES, 1)
        xs_ref[rows, :] = s
        return s

    s_last = lax.fori_loop(0, steps, step, sc_ref[...], unroll=4)
    sc_ref[...] = s_last
    sl_ref[0] = s_last
    y = jnp.dot(xs_ref[...].astype(BF16), cmat_ref[...], preferred_element_type=F32) + d_ref[...] * u
    y = jax.nn.gelu(y)
    gate = jax.nn.sigmoid(jnp.dot(y.astype(BF16), wg_ref[...], preferred_element_type=F32) + bg_ref[...])
    y_ref[0] = y * gate


S5_STEPS = 64


def s5(u, s0, bmat, a_tables, cmat, d_skip, w_glu, b_glu):
    bn, seq, _ = u.shape
    groups = bn // SUBLANES
    steps = min(S5_STEPS, seq)
    rows = steps * SUBLANES
    uc = u[:, :, OFF_C:].reshape(groups, SUBLANES, seq, D_C).transpose(0, 2, 1, 3).reshape(groups, seq * SUBLANES, D_C)
    y_tm, s_last = pl.pallas_call(
        functools.partial(_s5_kernel, steps=steps),
        grid=(groups, seq // steps),
        in_specs=[
            pl.BlockSpec((1, rows, D_C), lambda g, s: (g, s, 0)),
            pl.BlockSpec((1, SUBLANES, 2 * C_LANES), lambda g, s: (g, 0, 0)),
            _full((D_C, 2 * C_LANES)), _full((SUBLANES, 2 * C_LANES)), _full((SUBLANES, 2 * C_LANES)),
            _full((2 * C_LANES, D_C)), _full((1, D_C)), _full((D_C, D_C)), _full((1, D_C)),
        ],
        out_specs=[
            pl.BlockSpec((1, rows, D_C), lambda g, s: (g, s, 0)),
            pl.BlockSpec((1, SUBLANES, 2 * C_LANES), lambda g, s: (g, 0, 0)),
        ],
        out_shape=[
            jax.ShapeDtypeStruct((groups, seq * SUBLANES, D_C), F32),
            jax.ShapeDtypeStruct((groups, SUBLANES, 2 * C_LANES), F32),
        ],
        scratch_shapes=[pltpu.VMEM((SUBLANES, 2 * C_LANES), F32), pltpu.VMEM((rows, 2 * C_LANES), F32)],
        compiler_params=_cparams("parallel", "arbitrary"),
        name="s5",
    )(uc, s0.reshape(groups, SUBLANES, 2 * C_LANES), bmat, *a_tables, cmat, d_skip.reshape(1, D_C), w_glu,
      b_glu.reshape(1, D_C))
    y = y_tm.reshape(groups, seq, SUBLANES, D_C).transpose(0, 2, 1, 3).reshape(bn, seq, D_C)
    return y, s_last.reshape(bn, 1, 2 * C_LANES)


def s5_params(a_re, a_im, log_dt, b_re, b_im, c_re, c_im):
    lam = lax.complex(a_re.astype(F32), a_im.astype(F32))
    dt = jnp.exp(log_dt.astype(F32))[:, None]
    a_bar = jnp.exp(lam * dt)
    b_bar = ((a_bar - 1.0) / lam)[:, :, None] * lax.complex(b_re.astype(F32), b_im.astype(F32))
    eye = jnp.eye(C_GROUPS, dtype=F32)

    def block_in(m):
        return jnp.einsum('gpi,gh->gihp', m, eye).reshape(D_C, C_LANES)

    def block_out(m):
        return jnp.einsum('gip,gh->gphi', m, eye).reshape(C_LANES, D_C)

    bmat = jnp.concatenate([block_in(jnp.real(b_bar)), block_in(jnp.imag(b_bar))], axis=1)
    cmat = jnp.concatenate([block_out(c_re.astype(F32)), -block_out(c_im.astype(F32))], axis=0)
    re = jnp.real(a_bar).reshape(1, C_LANES)
    im = jnp.imag(a_bar).reshape(1, C_LANES)
    a_re2 = jnp.broadcast_to(jnp.concatenate([re, re], axis=1), (SUBLANES, 2 * C_LANES))
    a_im2 = jnp.broadcast_to(jnp.concatenate([-im, im], axis=1), (SUBLANES, 2 * C_LANES))
    return bmat.astype(BF16), (a_re2, a_im2), cmat.astype(BF16)


def _rms(x, g):
    return x * lax.rsqrt(jnp.mean(jnp.square(x), axis=-1, keepdims=True) + RMS_EPS) * g


def _layer_norm(x, g, b):
    mu = jnp.mean(x, axis=-1, keepdims=True)
    xc = x - mu
    var = jnp.mean(jnp.square(xc), axis=-1, keepdims=True)
    return xc * lax.rsqrt(var + LN_EPS) * g + b


def _out_proj_kernel(ya_ref, yb_ref, yc_ref, x_ref, ga_ref, gb_ref, gc_ref, w_ref, lg_ref, lb_ref,
                     wrh_ref, wrl_ref, br_ref,
                     x1_ref, x1p_ref, topi_ref, gate_ref, pos_ref, cnt_ref, seen_ref):
    @pl.when(pl.program_id(0) == 0)
    def _():
        seen_ref[...] = jnp.zeros_like(seen_ref)

    m = jnp.dot(_rms(ya_ref[...], ga_ref[...]).astype(BF16), w_ref[0:D_A, :], preferred_element_type=F32)
    m += jnp.dot(_rms(yb_ref[...], gb_ref[...]).astype(BF16), w_ref[D_A:D_A + D_B, :], preferred_element_type=F32)
    m += jnp.dot(_rms(yc_ref[...], gc_ref[...]).astype(BF16), w_ref[D_A + D_B:, :], preferred_element_type=F32)
    x1 = _layer_norm(DEEPNORM_ALPHA * x_ref[...] + m, lg_ref[...], lb_ref[...])
    x1_ref[...] = x1
    x1h = x1.astype(BF16)
    bits = pltpu.bitcast(x1h.astype(F32), jnp.uint32)
    x1p_ref[...] = (bits[:, :D_MODEL // 2] >> 16) | (bits[:, D_MODEL // 2:] & jnp.uint32(0xFFFF0000))

    x1l = (x1 - x1h.astype(F32)).astype(BF16)
    logits = (jnp.dot(x1h, wrh_ref[...], preferred_element_type=F32)
              + jnp.dot(x1l, wrh_ref[...], preferred_element_type=F32)
              + jnp.dot(x1h, wrl_ref[...], preferred_element_type=F32)) + br_ref[...]

    tm = logits.shape[0]
    col = lax.broadcasted_iota(jnp.int32, (tm, N_EXPERTS), 1).astype(F32)
    work = logits
    sels, vals, idxs = [], [], []
    for _ in range(TOP_K):
        top = jnp.max(work, axis=1, keepdims=True)
        idx = jnp.min(jnp.where(work == top, col, float(N_EXPERTS)), axis=1, keepdims=True)
        sel = col == idx
        work = jnp.where(sel, -jnp.inf, work)
        sels.append(sel)
        vals.append(top)
        idxs.append(idx)
    exps = [jnp.exp(v - vals[0]) for v in vals]
    total = functools.reduce(lambda a, b: a + b, exps)

    chosen = functools.reduce(lambda a, b: a + b, [s.astype(F32) for s in sels])
    ri = lax.broadcasted_iota(jnp.int32, (tm, tm), 0)
    ci = lax.broadcasted_iota(jnp.int32, (tm, tm), 1)
    earlier = jnp.where(ci < ri, 1.0, 0.0).astype(BF16)
    before = jnp.dot(earlier, chosen.astype(BF16), preferred_element_type=F32) + seen_ref[...]
    seen = seen_ref[...] + jnp.sum(chosen, axis=0, keepdims=True)
    seen_ref[...] = seen
    cnt_ref[...] = seen.astype(jnp.int32)

    slot = lax.broadcasted_iota(jnp.int32, (tm, TOP_K), 1)
    top_i = jnp.zeros((tm, TOP_K), F32)
    gates = jnp.zeros((tm, TOP_K), F32)
    pos = jnp.zeros((tm, TOP_K), F32)
    for k in range(TOP_K):
        rank = jnp.sum(jnp.where(sels[k], before, 0.0), axis=1, keepdims=True)
        top_i = jnp.where(slot == k, idxs[k], top_i)
        gates = jnp.where(slot == k, exps[k] / total, gates)
        pos = jnp.where(slot == k, rank, pos)
    topi_ref[...] = top_i.astype(jnp.int32)
    gate_ref[...] = gates
    pos_ref[...] = pos.astype(jnp.int32)


def out_proj(ya, yb, yc, x2d, g_a, g_b, g_c, w_out_bf16, ln_g, ln_b, w_router, b_router):
    n = x2d.shape[0]
    tm = min(ROW_TILE, n)
    rows = lambda w: pl.BlockSpec((tm, w), lambda i: (i, 0))
    wr = w_router.astype(F32)
    wr_hi = wr.astype(BF16)
    wr_lo = (wr - wr_hi.astype(F32)).astype(BF16)
    return pl.pallas_call(
        _out_proj_kernel,
        grid=(n // tm,),
        in_specs=[rows(D_A), rows(D_B), rows(D_C), rows(D_MODEL),
                  _full((1, D_A)), _full((1, D_B)), _full((1, D_C)), _full((D_MODEL, D_MODEL)),
                  _full((1, D_MODEL)), _full((1, D_MODEL)),
                  _full((D_MODEL, N_EXPERTS)), _full((D_MODEL, N_EXPERTS)), _full((1, N_EXPERTS))],
        out_specs=[rows(D_MODEL), rows(D_MODEL // 2), rows(TOP_K), rows(TOP_K), rows(TOP_K),
                   _full((1, N_EXPERTS))],
        out_shape=[jax.ShapeDtypeStruct((n, D_MODEL), F32), jax.ShapeDtypeStruct((n, D_MODEL // 2), jnp.uint32),
                   jax.ShapeDtypeStruct((n, TOP_K), jnp.int32), jax.ShapeDtypeStruct((n, TOP_K), F32),
                   jax.ShapeDtypeStruct((n, TOP_K), jnp.int32), jax.ShapeDtypeStruct((1, N_EXPERTS), jnp.int32)],
        scratch_shapes=[pltpu.VMEM((1, N_EXPERTS), F32)],
        compiler_params=_cparams("arbitrary"),
        name="out_proj",
    )(ya, yb, yc, x2d, g_a.reshape(1, D_A), g_b.reshape(1, D_B), g_c.reshape(1, D_C), w_out_bf16,
      ln_g.reshape(1, D_MODEL), ln_b.reshape(1, D_MODEL), wr_hi, wr_lo, b_router.reshape(1, N_EXPERTS))


CAST_ROWS = 128


def _cast_weight(src_ref, dst_ref):
    def body(c, carry):
        r = pl.multiple_of(c * CAST_ROWS, CAST_ROWS)
        dst_ref[pl.ds(r, CAST_ROWS), :] = src_ref[0, 0, pl.ds(r, CAST_ROWS), :].astype(BF16)
        return carry

    lax.fori_loop(0, dst_ref.shape[0] // CAST_ROWS, body, 0)


def _moe_kernel(blk_e_ref, n_used_ref, tok_ref, tok_next_ref, x_ref, wgu_ref, bgu_ref, wdn_ref, bdn_ref, o_ref,
                wgu_s, wdn_s, xbuf_ref, xh_ref, sem_ref):
    i = pl.program_id(0)
    n_used = n_used_ref[0]
    slot = i % 2
    expert_changed = (i == 0) | (blk_e_ref[i] != blk_e_ref[jnp.maximum(i - 1, 0)])

    def row_copy(tok, r, s):
        return pltpu.make_async_copy(x_ref.at[pl.ds(tok, 1), :], xbuf_ref.at[s, pl.ds(r, 1), :], sem_ref.at[s])

    @pl.when(i == 0)
    def _():
        def body(r, carry):
            row_copy(tok_ref[0, 0, r], r, 0).start()
            return carry

        lax.fori_loop(0, MOE_ROWS, body, 0, unroll=8)

    @pl.when(expert_changed)
    def _():
        _cast_weight(wgu_ref, wgu_s)
        _cast_weight(wdn_ref, wdn_s)

    @pl.when(i <= n_used)
    def _():
        pltpu.make_async_copy(x_ref.at[pl.ds(0, MOE_ROWS), :], xbuf_ref.at[slot], sem_ref.at[slot]).wait()

    @pl.when(i < n_used)
    def _():
        packed = xbuf_ref[slot]
        xh_ref[:, :D_MODEL // 2] = pltpu.bitcast(packed << 16, F32).astype(BF16)
        xh_ref[:, D_MODEL // 2:] = pltpu.bitcast(packed & jnp.uint32(0xFFFF0000), F32).astype(BF16)
        cw = D_FF // MOE_CHUNKS
        per_chunk = MOE_ROWS // MOE_START_CHUNKS
        acc = None
        for c in range(MOE_CHUNKS):
            for r in range(min(c, MOE_START_CHUNKS) * per_chunk, min(c + 1, MOE_START_CHUNKS) * per_chunk):
                row_copy(tok_next_ref[0, 0, r], r, 1 - slot).start(priority=r % DMA_THREADS)
            words = xbuf_ref[slot, 0:SUBLANES, 0:cw]
            zero = ((words >> 16) >> 16).astype(F32)[0:1, :]
            xh = xh_ref[...]
            hg = (jnp.dot(xh, wgu_s[:, c * cw:(c + 1) * cw], preferred_element_type=F32)
                  + (bgu_ref[0, 0, :, c * cw:(c + 1) * cw] + zero))
            hu = (jnp.dot(xh, wgu_s[:, D_FF + c * cw:D_FF + (c + 1) * cw], preferred_element_type=F32)
                  + bgu_ref[0, 0, :, D_FF + c * cw:D_FF + (c + 1) * cw])
            gate = jnp.minimum(hg, SWIGLU_LIMIT)
            up = jnp.clip(hu, -SWIGLU_LIMIT, SWIGLU_LIMIT)
            glu = gate * jax.nn.sigmoid(gate * SWIGLU_ALPHA)
            act = ((up + 1.0) * glu).astype(BF16)
            part = jnp.dot(act, wdn_s[c * cw:(c + 1) * cw, :], preferred_element_type=F32)
            acc = part if acc is None else acc + part
        o_ref[...] = acc + bdn_ref[0, 0]

    @pl.when(i >= n_used)
    def _():
        o_ref[...] = jnp.zeros_like(o_ref)


def moe_experts(x, row_tok, blk_e, n_used, layer, w_gu, b_gu, w_dn, b_dn):
    n_rows = row_tok.shape[0]
    n_blocks = n_rows // MOE_ROWS
    depth = w_gu.shape[0]
    tok_blocks = row_tok.reshape(n_blocks, 1, MOE_ROWS)
    tok_spec = lambda index_map: pl.BlockSpec((1, 1, MOE_ROWS), index_map, memory_space=pltpu.SMEM)
    grid_spec = pltpu.PrefetchScalarGridSpec(
        num_scalar_prefetch=2,
        grid=(n_blocks,),
        in_specs=[
            tok_spec(lambda i, be, nu: (i, 0, 0)),
            tok_spec(lambda i, be, nu: (jnp.minimum(i + 1, n_blocks - 1), 0, 0)),
            pl.BlockSpec(memory_space=pl.ANY),
            pl.BlockSpec((1, 1, D_MODEL, 2 * D_FF), lambda i, be, nu: (layer, be[i], 0, 0)),
            pl.BlockSpec((1, 1, 1, 2 * D_FF), lambda i, be, nu: (layer, be[i], 0, 0)),
            pl.BlockSpec((1, 1, D_FF, D_MODEL), lambda i, be, nu: (layer, be[i], 0, 0)),
            pl.BlockSpec((1, 1, 1, D_MODEL), lambda i, be, nu: (layer, be[i], 0, 0)),
        ],
        out_specs=pl.BlockSpec((MOE_ROWS, D_MODEL), lambda i, be, nu: (i, 0)),
        scratch_shapes=[pltpu.VMEM((D_MODEL, 2 * D_FF), BF16), pltpu.VMEM((D_FF, D_MODEL), BF16),
                        pltpu.VMEM((2, MOE_ROWS, D_MODEL // 2), jnp.uint32), pltpu.VMEM((MOE_ROWS, D_MODEL), BF16),
                        pltpu.SemaphoreType.DMA((2,))],
    )
    return pl.pallas_call(
        _moe_kernel,
        grid_spec=grid_spec,
        out_shape=jax.ShapeDtypeStruct((n_rows, D_MODEL), F32),
        compiler_params=_cparams("arbitrary"),
        name="moe_experts",
    )(blk_e, n_used, tok_blocks, tok_blocks, x, w_gu, b_gu.reshape(depth, N_EXPERTS, 1, 2 * D_FF), w_dn,
      b_dn.reshape(depth, N_EXPERTS, 1, D_MODEL))


def route(top_i, pos, counts):
    n_tok = top_i.shape[0]
    n_slots = n_tok * TOP_K
    padded = ((counts + MOE_ROWS - 1) // MOE_ROWS) * MOE_ROWS
    pad_end = jnp.cumsum(padded)
    pad_start = pad_end - padded
    start = jnp.cumsum(counts) - counts
    experts = jnp.arange(N_EXPERTS, dtype=jnp.int32)
    slot_start = jnp.sum(jnp.where(top_i[:, :, None] == experts, pad_start, 0), axis=-1)
    dest = (slot_start + pos).astype(jnp.int32).T
    n_blocks = -(-n_slots // MOE_ROWS) + N_EXPERTS
    blk_first = jnp.arange(n_blocks, dtype=jnp.int32) * MOE_ROWS
    blk_e = jnp.minimum(jnp.sum((pad_end[None, :] <= blk_first[:, None]).astype(jnp.int32), axis=1), N_EXPERTS - 1)
    order = jnp.argsort(top_i.reshape(-1)).astype(jnp.int32)
    within = jnp.arange(MOE_ROWS, dtype=jnp.int32)[None, :] + (blk_first - pad_start[blk_e])[:, None]
    src = jnp.clip(start[blk_e][:, None] + within, 0, n_slots - 1).reshape(-1)
    valid = (within < counts[blk_e][:, None]).reshape(-1)
    row_tok = jnp.where(valid, order[src] // TOP_K, 0).astype(jnp.int32)
    n_used = (pad_end[-1:] // MOE_ROWS).astype(jnp.int32)
    return dest, row_tok, blk_e.astype(jnp.int32), n_used


COMBINE_ROWS = 256


def _combine_ln_kernel(dcur_ref, dnext_ref, x_ref, gate_ref, g_ref, b_ref, yb_ref, o_ref, buf_ref, sem_ref, *,
                       n_tiles):
    i = pl.program_id(0)
    tm = x_ref.shape[0]
    n_rows = TOP_K * tm
    slot = i % 2

    def row_copy(row, r, s):
        return pltpu.make_async_copy(yb_ref.at[pl.ds(row, 1), :], buf_ref.at[s, pl.ds(r, 1), :], sem_ref.at[s])

    def start_gather(d_ref, s):
        def body(r, carry):
            row_copy(d_ref[0, 0, r], r, s).start()
            return carry

        lax.fori_loop(0, n_rows, body, 0, unroll=8)

    @pl.when(i == 0)
    def _():
        start_gather(dcur_ref, 0)

    @pl.when(i + 1 < n_tiles)
    def _():
        for r in range(n_rows):
            row_copy(dnext_ref[0, 0, r], r, 1 - slot).start(priority=r % DMA_THREADS)

    pltpu.make_async_copy(yb_ref.at[pl.ds(0, n_rows), :], buf_ref.at[slot], sem_ref.at[slot]).wait()
    gates = gate_ref[...]
    moe = buf_ref[slot, 0:tm, :] * gates[:, 0:1]
    for k in range(1, TOP_K):
        moe = moe + buf_ref[slot, k * tm:(k + 1) * tm, :] * gates[:, k:k + 1]
    o_ref[...] = _layer_norm(DEEPNORM_ALPHA * x_ref[...] + moe, g_ref[...], b_ref[...])


def combine_ln(x2d, yb, dest, gates, ln_g, ln_b):
    n = x2d.shape[0]
    tm = min(COMBINE_ROWS, n)
    n_tiles = n // tm
    rows = pl.BlockSpec((tm, D_MODEL), lambda i: (i, 0))
    dest_tiles = dest.reshape(TOP_K, n_tiles, tm).transpose(1, 0, 2).reshape(n_tiles, 1, TOP_K * tm)
    dest_spec = lambda index_map: pl.BlockSpec((1, 1, TOP_K * tm), index_map, memory_space=pltpu.SMEM)
    return pl.pallas_call(
        functools.partial(_combine_ln_kernel, n_tiles=n_tiles),
        grid=(n_tiles,),
        in_specs=[dest_spec(lambda i: (i, 0, 0)), dest_spec(lambda i: (jnp.minimum(i + 1, n_tiles - 1), 0, 0)),
                  rows, pl.BlockSpec((tm, TOP_K), lambda i: (i, 0)), _full((1, D_MODEL)), _full((1, D_MODEL)),
                  pl.BlockSpec(memory_space=pl.ANY)],
        out_specs=rows,
        out_shape=jax.ShapeDtypeStruct((n, D_MODEL), F32),
        scratch_shapes=[pltpu.VMEM((2, TOP_K * tm, D_MODEL), F32), pltpu.SemaphoreType.DMA((2,))],
        compiler_params=_cparams("arbitrary"),
        name="combine_ln",
    )(dest_tiles, dest_tiles, x2d, gates, ln_g.reshape(1, D_MODEL), ln_b.reshape(1, D_MODEL), yb)


def _block_diag(w):
    h, d, _ = w.shape
    return jnp.einsum('hij,hg->higj', w.astype(F32), jnp.eye(h, dtype=F32)).reshape(h * d, h * d)


def _mixer(x, state, p, attn_bias):
    bn, seq, _ = x.shape
    n = bn * seq
    conv_buf, h0, s0, k_cache, v_cache = state
    x2d = x.reshape(n, D_MODEL)
    u = in_proj(x2d, p['w_in']).reshape(bn, seq, IN_COLS)

    y_a, new_buf, h_last = rglru(u, conv_buf, h0, p['conv_w'], p['conv_b'], p['w_r'], p['b_r'], p['w_i'], p['b_i'],
                                 p['lam'])
    q_col, k_col, v_col = OFF_Q // LANE, OFF_K // LANE, OFF_V // LANE
    if k_cache is None:
        qb = min(BAND_PAST, seq)
        y_b = attention(u, q_col, u, k_col, u, v_col, u, k_col, v_col, attn_bias, qb=qb, pb=qb,
                        prev_is_same_array=True)
    else:
        y_b = attention(u, q_col, k_cache, 0, v_cache, 0, u, k_col, v_col, attn_bias, qb=seq,
                        pb=k_cache.shape[1], prev_is_same_array=False)
    y_c, s_last = s5(u, s0, p['bmat'], p['scan_tables'], p['cmat'], p['d_c'], p['w_glu'], p['b_glu'])

    x1, x1p, top_i, gates, pos, counts = out_proj(
        y_a.reshape(n, D_A), y_b.reshape(n, D_B), y_c.reshape(n, D_C), x2d, p['g_a'], p['g_b'], p['g_c'],
        p['w_out'], p['ln1_g'], p['ln1_b'], p['w_router'], p['b_router'])

    k_rows = u[:, :, OFF_K:OFF_V]
    v_rows = u[:, :, OFF_V:OFF_C]
    if k_cache is None:
        keep = min(BAND_PAST, seq)
        k_rows = k_rows[:, seq - keep:]
        v_rows = v_rows[:, seq - keep:]
    k_rows = k_rows.reshape(bn, -1, B_HEADS, B_HEAD_DIM)
    v_rows = v_rows.reshape(bn, -1, B_HEADS, B_HEAD_DIM)
    s_re = s_last[:, 0, :C_LANES].reshape(bn, C_GROUPS, C_STATE)
    s_im = s_last[:, 0, C_LANES:].reshape(bn, C_GROUPS, C_STATE)
    routed = dict(x1=x1, x1p=x1p, top_i=top_i, gates=gates, pos=pos, counts=counts.reshape(N_EXPERTS))
    return routed, (new_buf, h_last.reshape(bn, D_A), k_rows, v_rows, s_re, s_im)


def _moe(groups, p):
    experts = jnp.arange(N_EXPERTS, dtype=jnp.int32)
    seen = jnp.zeros((N_EXPERTS,), jnp.int32)
    pos_all = []
    for g in groups:
        pos_all.append(g['pos'] + jnp.sum(jnp.where(g['top_i'][:, :, None] == experts, seen, 0), axis=-1))
        seen = seen + g['counts']
    top_i = jnp.concatenate([g['top_i'] for g in groups], axis=0)
    dest, row_tok, blk_e, n_used = route(top_i, jnp.concatenate(pos_all, axis=0), seen)
    x1p = jnp.concatenate([g['x1p'] for g in groups], axis=0)
    yb = moe_experts(x1p, row_tok, blk_e, n_used, p['layer'], p['w_gu'], p['b_gu'], p['w_dn'], p['b_dn'])
    outs, first = [], 0
    for g in groups:
        n = g['x1'].shape[0]
        outs.append(combine_ln(g['x1'], yb, dest[:, first:first + n], g['gates'], p['ln2_g'], p['ln2_b']))
        first += n
    return outs


def kernel(x_prompt, x_sample, cache_conv_a, state_h_a, cache_k_b, cache_v_b, state_s_re_c, state_s_im_c, w_in, conv_w_a, conv_b_a, w_r_a, b_r_a, w_i_a, b_i_a, lambda_a, rel_bias_b, a_re_c, a_im_c, log_dt_c, b_re_c, b_im_c, c_re_c, c_im_c, d_c, w_glu_c, b_glu_c, g_norm_a, g_norm_b, g_norm_c, w_out, ln1_g, ln1_b, w_router, b_router, w_gu, b_gu, w_dn, b_dn, ln2_g, ln2_b):
    bp, seq_p, _ = x_prompt.shape
    bs, seq_s, _ = x_sample.shape
    kv_rows = cache_k_b.shape[2]
    yp, ys = x_prompt, x_sample
    p_states = [[] for _ in range(6)]
    s_states = [[] for _ in range(6)]
    for l in range(DEPTH):
        bmat, scan_tables, cmat = s5_params(a_re_c[l], a_im_c[l], log_dt_c[l], b_re_c[l], b_im_c[l], c_re_c[l],
                                           c_im_c[l])
        p = dict(
            w_in=w_in[l].astype(BF16), conv_w=conv_w_a[l], conv_b=conv_b_a[l],
            w_r=_block_diag(w_r_a[l]).astype(BF16), b_r=b_r_a[l], w_i=_block_diag(w_i_a[l]).astype(BF16),
            b_i=b_i_a[l], lam=lambda_a[l],
            bmat=bmat, scan_tables=scan_tables, cmat=cmat, d_c=d_c[l], w_glu=w_glu_c[l].astype(BF16), b_glu=b_glu_c[l],
            g_a=g_norm_a[l], g_b=g_norm_b[l], g_c=g_norm_c[l], w_out=w_out[l].astype(BF16),
            ln1_g=ln1_g[l], ln1_b=ln1_b[l], w_router=w_router[l], b_router=b_router[l],
            layer=l, w_gu=w_gu, b_gu=b_gu, w_dn=w_dn, b_dn=b_dn,
            ln2_g=ln2_g[l], ln2_b=ln2_b[l],
        )
        zero_state = (jnp.zeros((bp, CONV_W - 1, D_A), F32), jnp.zeros((bp, D_A), F32),
                      jnp.zeros((bp, 1, 2 * C_LANES), F32), None, None)
        qb = min(BAND_PAST, seq_p)
        routed_p, st = _mixer(yp, zero_state, p, bias_table(rel_bias_b[l], qb, qb, band_mask=True))
        for lst, s in zip(p_states, st):
            lst.append(s)
        s0 = jnp.concatenate([state_s_re_c[l].reshape(bs, 1, C_LANES), state_s_im_c[l].reshape(bs, 1, C_LANES)],
                             axis=-1)
        sample_state = (cache_conv_a[l], state_h_a[l], s0,
                        cache_k_b[l].reshape(bs, kv_rows, D_B), cache_v_b[l].reshape(bs, kv_rows, D_B))
        routed_s, st = _mixer(ys, sample_state, p, bias_table(rel_bias_b[l], seq_s, kv_rows, band_mask=False))
        for lst, s in zip(s_states, st):
            lst.append(s)
        yp, ys = _moe([routed_p, routed_s], p)
        yp = yp.reshape(bp, seq_p, D_MODEL)
        ys = ys.reshape(bs, seq_s, D_MODEL)
    return (yp, ys) + tuple(jnp.stack(s) for s in p_states) + tuple(jnp.stack(s) for s in s_states)
```

```python
import functools
import math

import jax
import jax.numpy as jnp
import numpy as np
from jax import lax
from jax.experimental import pallas as pl
from jax.experimental.pallas import tpu as pltpu

F32 = jnp.float32
BF16 = jnp.bfloat16

D_MODEL = 1024
DEPTH = 2
CHUNK = 64
PREV_CHUNKS = 8
BAND_PAST = PREV_CHUNKS * CHUNK
D_A = D_MODEL // 4
D_B = D_MODEL // 2
D_C = D_MODEL // 4
A_HEADS = 4
A_HEAD_DIM = D_A // A_HEADS
CONV_W = 4
RG_C = 8.0
B_HEADS = 8
B_HEAD_DIM = D_B // B_HEADS
REL_CLIP = 128
C_GW = 16
C_GROUPS = D_C // C_GW
C_STATE = 64
C_LANES = C_GROUPS * C_STATE
OFF_GA = D_A
OFF_Q = 2 * D_A
OFF_K = OFF_Q + D_B
OFF_V = OFF_K + D_B
OFF_C = OFF_V + D_B
IN_COLS = OFF_C + D_C
N_EXPERTS = 32
TOP_K = 4
D_FF = D_MODEL
SWIGLU_LIMIT = 7.0
SWIGLU_ALPHA = 1.702
DEEPNORM_ALPHA = (2 * DEPTH) ** 0.25
LN_EPS = 1e-5
RMS_EPS = 1e-6
NEG_INF = -1e30

LANE = 128
SUBLANES = 8
HEADS_PER_SLAB = LANE // B_HEAD_DIM
VMEM_LIMIT = 56 * 1024 * 1024

ROW_TILE = 512
MOE_ROWS = 512
MOE_CHUNKS = 1
DMA_THREADS = 2
MOE_START_CHUNKS = 1
ATTN_SUB_ROWS = 256


def _cparams(*sem):
    return pltpu.CompilerParams(dimension_semantics=sem, vmem_limit_bytes=VMEM_LIMIT)


def _full(shape):
    return pl.BlockSpec(shape, lambda *_: (0,) * len(shape))


def _in_proj_kernel(x_ref, w_ref, o_ref):
    o_ref[...] = jnp.dot(x_ref[...].astype(BF16), w_ref[...], preferred_element_type=F32)


def in_proj(x2d, w_bf16):
    n = x2d.shape[0]
    tm = min(ROW_TILE, n)
    return pl.pallas_call(
        _in_proj_kernel,
        grid=(n // tm,),
        in_specs=[pl.BlockSpec((tm, D_MODEL), lambda i: (i, 0)), _full((D_MODEL, IN_COLS))],
        out_specs=pl.BlockSpec((tm, IN_COLS), lambda i: (i, 0)),
        out_shape=jax.ShapeDtypeStruct((n, IN_COLS), F32),
        compiler_params=_cparams("parallel"),
        name="in_proj",
    )(x2d, w_bf16)


def _rglru_kernel(xa_ref, ga_ref, buf_ref, h0_ref, cw_ref, cb_ref, wr_ref, br_ref, wi_ref, bi_ref, lam_ref,
                  y_ref, nbuf_ref, hl_ref, xp_ref, hc_ref, as_ref, hs_ref, *, t):
    pad = SUBLANES
    hist = CONV_W - 1

    @pl.when(pl.program_id(1) == 0)
    def _():
        xp_ref[0:pad, :] = jnp.zeros((pad, D_A), F32)
        xp_ref[pad - hist:pad, :] = buf_ref[0]
        hc_ref[...] = h0_ref[0]

    xa = xa_ref[0]
    xp_ref[pad:pad + t, :] = xa
    xc = cb_ref[...] + xa * cw_ref[hist:hist + 1, :]
    for j in range(hist):
        xc = xc + xp_ref[pad - hist + j:pad - hist + j + t, :] * cw_ref[j:j + 1, :]
    tail = xp_ref[pad + t - hist:pad + t, :]
    nbuf_ref[0] = tail
    xp_ref[pad - hist:pad, :] = tail

    xch = xc.astype(BF16)
    r = jax.nn.sigmoid(jnp.dot(xch, wr_ref[...], preferred_element_type=F32) + br_ref[...])
    i = jax.nn.sigmoid(jnp.dot(xch, wi_ref[...], preferred_element_type=F32) + bi_ref[...])
    lam = lam_ref[...]
    softplus_neg_lam = jnp.maximum(-lam, 0.0) + jnp.log(1.0 + jnp.exp(-jnp.abs(lam)))
    log_a = (-RG_C) * r * softplus_neg_lam
    a = jnp.exp(log_a)
    b = jnp.sqrt(1.0 - jnp.exp(2.0 * log_a)) * (i * xc)

    row_in_group = lax.broadcasted_iota(jnp.int32, (t, D_A), 0) & (SUBLANES - 1)
    acc_a, acc_b = a, b
    sh = 1
    while sh < SUBLANES:
        m = row_in_group >= sh
        acc_b = jnp.where(m, acc_a * pltpu.roll(acc_b, sh, 0) + acc_b, acc_b)
        acc_a = jnp.where(m, acc_a * pltpu.roll(acc_a, sh, 0), acc_a)
        sh *= 2
    as_ref[...] = acc_a
    hs_ref[...] = acc_b

    def group(g, carry):
        rows = pl.ds(pl.multiple_of(g * SUBLANES, SUBLANES), SUBLANES)
        h = hs_ref[rows, :] + as_ref[rows, :] * carry
        hs_ref[rows, :] = h
        return h[SUBLANES - 1:SUBLANES, :]

    h_last = lax.fori_loop(0, t // SUBLANES, group, hc_ref[...], unroll=8)
    hc_ref[...] = h_last
    hl_ref[0] = h_last
    y_ref[0] = hs_ref[...] * jax.nn.gelu(ga_ref[0])


def rglru(u, conv_buf, h0, cw, cb, wr_bd, br, wi_bd, bi, lam):
    bn, seq, _ = u.shape
    t = min(ROW_TILE, seq)
    vec = _full((1, D_A))
    return pl.pallas_call(
        functools.partial(_rglru_kernel, t=t),
        grid=(bn, seq // t),
        in_specs=[
            pl.BlockSpec((1, t, D_A), lambda b, s: (b, s, 0)),
            pl.BlockSpec((1, t, D_A), lambda b, s: (b, s, OFF_GA // D_A)),
            pl.BlockSpec((1, CONV_W - 1, D_A), lambda b, s: (b, 0, 0)),
            pl.BlockSpec((1, 1, D_A), lambda b, s: (b, 0, 0)),
            _full((CONV_W, D_A)), vec, _full((D_A, D_A)), vec, _full((D_A, D_A)), vec, vec,
        ],
        out_specs=[
            pl.BlockSpec((1, t, D_A), lambda b, s: (b, s, 0)),
            pl.BlockSpec((1, CONV_W - 1, D_A), lambda b, s: (b, 0, 0)),
            pl.BlockSpec((1, 1, D_A), lambda b, s: (b, 0, 0)),
        ],
        out_shape=[
            jax.ShapeDtypeStruct((bn, seq, D_A), F32),
            jax.ShapeDtypeStruct((bn, CONV_W - 1, D_A), F32),
            jax.ShapeDtypeStruct((bn, 1, D_A), F32),
        ],
        scratch_shapes=[pltpu.VMEM((t + SUBLANES, D_A), F32), pltpu.VMEM((1, D_A), F32),
                        pltpu.VMEM((t, D_A), F32), pltpu.VMEM((t, D_A), F32)],
        compiler_params=_cparams("parallel", "arbitrary"),
        name="rglru",
    )(u, u, conv_buf, h0.reshape(bn, 1, D_A), cw, cb.reshape(1, D_A), wr_bd, br.reshape(1, D_A),
      wi_bd, bi.reshape(1, D_A), lam.reshape(1, D_A))


def _attn_kernel(q_ref, kp_ref, kc_ref, vp_ref, vc_ref, bias_ref, o_ref, *, pb, sq, mask_first_prev):
    scale = B_HEAD_DIM ** -0.5
    qb = q_ref.shape[1]
    q_all = (q_ref[0] * scale).astype(BF16)
    kp_all = kp_ref[0].astype(BF16)
    kc_all = kc_ref[0].astype(BF16)
    vp_all = vp_ref[0].astype(BF16)
    vc_all = vc_ref[0].astype(BF16)
    lane = lax.broadcasted_iota(jnp.int32, (1, LANE), 1)
    contract_last = (((1,), (1,)), ((), ()))
    for r in range(qb // sq):
        lo = max(pb + r * sq - BAND_PAST, 0)
        hi = (r + 1) * sq
        q = q_all[r * sq:hi, :]
        kp, vp = kp_all[lo:pb, :], vp_all[lo:pb, :]
        kc, vc = kc_all[0:hi, :], vc_all[0:hi, :]
        out = None
        for hh in range(HEADS_PER_SLAB):
            in_head = (lane // B_HEAD_DIM) == hh
            qh = jnp.where(in_head, q, jnp.zeros_like(q))
            sp = (lax.dot_general(qh, kp, contract_last, preferred_element_type=F32)
                  + bias_ref[hh, r * sq:hi, lo:pb])
            sc = (lax.dot_general(qh, kc, contract_last, preferred_element_type=F32)
                  + bias_ref[hh, r * sq:hi, pb:pb + hi])
            if mask_first_prev:
                sp = jnp.where(pl.program_id(2) == 0, NEG_INF, sp)
            m = jnp.maximum(jnp.max(sp, axis=-1, keepdims=True), jnp.max(sc, axis=-1, keepdims=True))
            ep = jnp.exp(sp - m)
            ec = jnp.exp(sc - m)
            denom = jnp.sum(ep, axis=-1, keepdims=True) + jnp.sum(ec, axis=-1, keepdims=True)
            o = (jnp.dot(ep.astype(BF16), vp, preferred_element_type=F32)
                 + jnp.dot(ec.astype(BF16), vc, preferred_element_type=F32)) / denom
            out = o if out is None else jnp.where(in_head, o, out)
        o_ref[0, r * sq:hi, :] = out


def attention(q_arr, q_col, kprev_arr, kprev_col, vprev_arr, vprev_col, kv_arr, k_col, v_col, bias, *, qb, pb,
              prev_is_same_array):
    bn, seq, _ = q_arr.shape
    n_slabs = B_HEADS // HEADS_PER_SLAB
    if prev_is_same_array:
        prev_map = lambda col: (lambda hp, b, s: (b, jnp.maximum(s - 1, 0), col + hp))
    else:
        prev_map = lambda col: (lambda hp, b, s: (b, 0, col + hp))
    cur_map = lambda col: (lambda hp, b, s: (b, s, col + hp))
    return pl.pallas_call(
        functools.partial(_attn_kernel, pb=pb, sq=min(qb, ATTN_SUB_ROWS), mask_first_prev=prev_is_same_array),
        grid=(n_slabs, bn, seq // qb),
        in_specs=[
            pl.BlockSpec((1, qb, LANE), cur_map(q_col)),
            pl.BlockSpec((1, pb, LANE), prev_map(kprev_col)),
            pl.BlockSpec((1, qb, LANE), cur_map(k_col)),
            pl.BlockSpec((1, pb, LANE), prev_map(vprev_col)),
            pl.BlockSpec((1, qb, LANE), cur_map(v_col)),
            pl.BlockSpec((HEADS_PER_SLAB, qb, pb + qb), lambda hp, b, s: (hp, 0, 0)),
        ],
        out_specs=pl.BlockSpec((1, qb, LANE), lambda hp, b, s: (b, s, hp)),
        out_shape=jax.ShapeDtypeStruct((bn, seq, D_B), F32),
        compiler_params=_cparams("arbitrary", "arbitrary", "arbitrary"),
        name="attention",
    )(q_arr, kprev_arr, kv_arr, vprev_arr, kv_arr, bias)


def _bias_kernel(v_ref, o_ref, *, band_mask):
    qb, width = o_ref.shape[1], o_ref.shape[2]
    period = v_ref.shape[2]
    table = pltpu.roll(jnp.broadcast_to(v_ref[0], (qb, period)), 0, 1, stride=1, stride_axis=0)[:, :width]
    if band_mask:
        i = lax.broadcasted_iota(jnp.int32, (qb, width), 0)
        j = lax.broadcasted_iota(jnp.int32, (qb, width), 1)
        dc = i // CHUNK - j // CHUNK + PREV_CHUNKS
        table = jnp.where((dc >= 0) & (dc <= PREV_CHUNKS), table, NEG_INF)
    o_ref[0] = table


def bias_table(rel_bias, qb, pb, band_mask):
    n_heads = rel_bias.shape[0]
    period = pb + 2 * qb
    m = np.arange(period)
    d = np.where(m < pb + qb, m, m - period)
    idx = np.clip(pb - d, -REL_CLIP, REL_CLIP) + REL_CLIP
    v = rel_bias.astype(F32)[:, idx].reshape(n_heads, 1, period)
    return pl.pallas_call(
        functools.partial(_bias_kernel, band_mask=band_mask),
        grid=(n_heads,),
        in_specs=[pl.BlockSpec((1, 1, period), lambda h: (h, 0, 0))],
        out_specs=pl.BlockSpec((1, qb, pb + qb), lambda h: (h, 0, 0)),
        out_shape=jax.ShapeDtypeStruct((n_heads, qb, pb + qb), F32),
        compiler_params=_cparams("parallel"),
        name="bias_table",
    )(v)


def _s5_kernel(u_ref, s0_ref, bmat_ref, ar_ref, ai_ref, cmat_ref, d_ref, wg_ref, bg_ref,
               y_ref, sl_ref, sc_ref, xs_ref, *, steps):
    @pl.when(pl.program_id(1) == 0)
    def _():
        sc_ref[...] = s0_ref[0]

    u = u_ref[0]
    xs_ref[...] = jnp.dot(u.astype(BF16), bmat_ref[...], preferred_element_type=F32)
    a_re = ar_ref[...]
    a_im = ai_ref[...]

    def step(t, s):
        rows = pl.ds(pl.multiple_of(t * SUBLANES, SUBLANES), SUBLANES)
        s = xs_ref[rows, :] + a_re * s + a_im * pltpu.roll(s, C_LANES, 1)
        xs_ref[rows, :] = s
        return s

    s_last = lax.fori_loop(0, steps, step, sc_ref[...], unroll=4)
    sc_ref[...] = s_last
    sl_ref[0] = s_last
    y = jnp.dot(xs_ref[...].astype(BF16), cmat_ref[...], preferred_element_type=F32) + d_ref[...] * u
    y = jax.nn.gelu(y)
    gate = jax.nn.sigmoid(jnp.dot(y.astype(BF16), wg_ref[...], preferred_element_type=F32) + bg_ref[...])
    y_ref[0] = y * gate


S5_STEPS = 64


def s5(u, s0, bmat, a_tables, cmat, d_skip, w_glu, b_glu):
    bn, seq, _ = u.shape
    groups = bn // SUBLANES
    steps = min(S5_STEPS, seq)
    rows = steps * SUBLANES
    uc = u[:, :, OFF_C:].reshape(groups, SUBLANES, seq, D_C).transpose(0, 2, 1, 3).reshape(groups, seq * SUBLANES, D_C)
    y_tm, s_last = pl.pallas_call(
        functools.partial(_s5_kernel, steps=steps),
        grid=(groups, seq // steps),
        in_specs=[
            pl.BlockSpec((1, rows, D_C), lambda g, s: (g, s, 0)),
            pl.BlockSpec((1, SUBLANES, 2 * C_LANES), lambda g, s: (g, 0, 0)),
            _full((D_C, 2 * C_LANES)), _full((SUBLANES, 2 * C_LANES)), _full((SUBLANES, 2 * C_LANES)),
            _full((2 * C_LANES, D_C)), _full((1, D_C)), _full((D_C, D_C)), _full((1, D_C)),
        ],
        out_specs=[
            pl.BlockSpec((1, rows, D_C), lambda g, s: (g, s, 0)),
            pl.BlockSpec((1, SUBLANES, 2 * C_LANES), lambda g, s: (g, 0, 0)),
        ],
        out_shape=[
            jax.ShapeDtypeStruct((groups, seq * SUBLANES, D_C), F32),
            jax.ShapeDtypeStruct((groups, SUBLANES, 2 * C_LANES), F32),
        ],
        scratch_shapes=[pltpu.VMEM((SUBLANES, 2 * C_LANES), F32), pltpu.VMEM((rows, 2 * C_LANES), F32)],
        compiler_params=_cparams("parallel", "arbitrary"),
        name="s5",
    )(uc, s0.reshape(groups, SUBLANES, 2 * C_LANES), bmat, *a_tables, cmat, d_skip.reshape(1, D_C), w_glu,
      b_glu.reshape(1, D_C))
    y = y_tm.reshape(groups, seq, SUBLANES, D_C).transpose(0, 2, 1, 3).reshape(bn, seq, D_C)
    return y, s_last.reshape(bn, 1, 2 * C_LANES)


def s5_params(a_re, a_im, log_dt, b_re, b_im, c_re, c_im):
    lam = lax.complex(a_re.astype(F32), a_im.astype(F32))
    dt = jnp.exp(log_dt.astype(F32))[:, None]
    a_bar = jnp.exp(lam * dt)
    b_bar = ((a_bar - 1.0) / lam)[:, :, None] * lax.complex(b_re.astype(F32), b_im.astype(F32))
    eye = jnp.eye(C_GROUPS, dtype=F32)

    def block_in(m):
        return jnp.einsum('gpi,gh->gihp', m, eye).reshape(D_C, C_LANES)

    def block_out(m):
        return jnp.einsum('gip,gh->gphi', m, eye).reshape(C_LANES, D_C)

    bmat = jnp.concatenate([block_in(jnp.real(b_bar)), block_in(jnp.imag(b_bar))], axis=1)
    cmat = jnp.concatenate([block_out(c_re.astype(F32)), -block_out(c_im.astype(F32))], axis=0)
    re = jnp.real(a_bar).reshape(1, C_LANES)
    im = jnp.imag(a_bar).reshape(1, C_LANES)
    a_re2 = jnp.broadcast_to(jnp.concatenate([re, re], axis=1), (SUBLANES, 2 * C_LANES))
    a_im2 = jnp.broadcast_to(jnp.concatenate([-im, im], axis=1), (SUBLANES, 2 * C_LANES))
    return bmat.astype(BF16), (a_re2, a_im2), cmat.astype(BF16)


def _rms(x, g):
    return x * lax.rsqrt(jnp.mean(jnp.square(x), axis=-1, keepdims=True) + RMS_EPS) * g


def _layer_norm(x, g, b):
    mu = jnp.mean(x, axis=-1, keepdims=True)
    xc = x - mu
    var = jnp.mean(jnp.square(xc), axis=-1, keepdims=True)
    return xc * lax.rsqrt(var + LN_EPS) * g + b


def _out_proj_kernel(ya_ref, yb_ref, yc_ref, x_ref, ga_ref, gb_ref, gc_ref, w_ref, lg_ref, lb_ref,
                     wrh_ref, wrl_ref, br_ref,
                     x1_ref, x1p_ref, topi_ref, gate_ref, pos_ref, cnt_ref, seen_ref):
    @pl.when(pl.program_id(0) == 0)
    def _():
        seen_ref[...] = jnp.zeros_like(seen_ref)

    m = jnp.dot(_rms(ya_ref[...], ga_ref[...]).astype(BF16), w_ref[0:D_A, :], preferred_element_type=F32)
    m += jnp.dot(_rms(yb_ref[...], gb_ref[...]).astype(BF16), w_ref[D_A:D_A + D_B, :], preferred_element_type=F32)
    m += jnp.dot(_rms(yc_ref[...], gc_ref[...]).astype(BF16), w_ref[D_A + D_B:, :], preferred_element_type=F32)
    x1 = _layer_norm(DEEPNORM_ALPHA * x_ref[...] + m, lg_ref[...], lb_ref[...])
    x1_ref[...] = x1
    x1h = x1.astype(BF16)
    bits = pltpu.bitcast(x1h.astype(F32), jnp.uint32)
    x1p_ref[...] = (bits[:, :D_MODEL // 2] >> 16) | (bits[:, D_MODEL // 2:] & jnp.uint32(0xFFFF0000))

    x1l = (x1 - x1h.astype(F32)).astype(BF16)
    logits = (jnp.dot(x1h, wrh_ref[...], preferred_element_type=F32)
              + jnp.dot(x1l, wrh_ref[...], preferred_element_type=F32)
              + jnp.dot(x1h, wrl_ref[...], preferred_element_type=F32)) + br_ref[...]

    tm = logits.shape[0]
    col = lax.broadcasted_iota(jnp.int32, (tm, N_EXPERTS), 1).astype(F32)
    work = logits
    sels, vals, idxs = [], [], []
    for _ in range(TOP_K):
        top = jnp.max(work, axis=1, keepdims=True)
        idx = jnp.min(jnp.where(work == top, col, float(N_EXPERTS)), axis=1, keepdims=True)
        sel = col == idx
        work = jnp.where(sel, -jnp.inf, work)
        sels.append(sel)
        vals.append(top)
        idxs.append(idx)
    exps = [jnp.exp(v - vals[0]) for v in vals]
    total = functools.reduce(lambda a, b: a + b, exps)

    chosen = functools.reduce(lambda a, b: a + b, [s.astype(F32) for s in sels])
    ri = lax.broadcasted_iota(jnp.int32, (tm, tm), 0)
    ci = lax.broadcasted_iota(jnp.int32, (tm, tm), 1)
    earlier = jnp.where(ci < ri, 1.0, 0.0).astype(BF16)
    before = jnp.dot(earlier, chosen.astype(BF16), preferred_element_type=F32) + seen_ref[...]
    seen = seen_ref[...] + jnp.sum(chosen, axis=0, keepdims=True)
    seen_ref[...] = seen
    cnt_ref[...] = seen.astype(jnp.int32)

    slot = lax.broadcasted_iota(jnp.int32, (tm, TOP_K), 1)
    top_i = jnp.zeros((tm, TOP_K), F32)
    gates = jnp.zeros((tm, TOP_K), F32)
    pos = jnp.zeros((tm, TOP_K), F32)
    for k in range(TOP_K):
        rank = jnp.sum(jnp.where(sels[k], before, 0.0), axis=1, keepdims=True)
        top_i = jnp.where(slot == k, idxs[k], top_i)
        gates = jnp.where(slot == k, exps[k] / total, gates)
        pos = jnp.where(slot == k, rank, pos)
    topi_ref[...] = top_i.astype(jnp.int32)
    gate_ref[...] = gates
    pos_ref[...] = pos.astype(jnp.int32)


def out_proj(ya, yb, yc, x2d, g_a, g_b, g_c, w_out_bf16, ln_g, ln_b, w_router, b_router):
    n = x2d.shape[0]
    tm = min(ROW_TILE, n)
    rows = lambda w: pl.BlockSpec((tm, w), lambda i: (i, 0))
    wr = w_router.astype(F32)
    wr_hi = wr.astype(BF16)
    wr_lo = (wr - wr_hi.astype(F32)).astype(BF16)
    return pl.pallas_call(
        _out_proj_kernel,
        grid=(n // tm,),
        in_specs=[rows(D_A), rows(D_B), rows(D_C), rows(D_MODEL),
                  _full((1, D_A)), _full((1, D_B)), _full((1, D_C)), _full((D_MODEL, D_MODEL)),
                  _full((1, D_MODEL)), _full((1, D_MODEL)),
                  _full((D_MODEL, N_EXPERTS)), _full((D_MODEL, N_EXPERTS)), _full((1, N_EXPERTS))],
        out_specs=[rows(D_MODEL), rows(D_MODEL // 2), rows(TOP_K), rows(TOP_K), rows(TOP_K),
                   _full((1, N_EXPERTS))],
        out_shape=[jax.ShapeDtypeStruct((n, D_MODEL), F32), jax.ShapeDtypeStruct((n, D_MODEL // 2), jnp.uint32),
                   jax.ShapeDtypeStruct((n, TOP_K), jnp.int32), jax.ShapeDtypeStruct((n, TOP_K), F32),
                   jax.ShapeDtypeStruct((n, TOP_K), jnp.int32), jax.ShapeDtypeStruct((1, N_EXPERTS), jnp.int32)],
        scratch_shapes=[pltpu.VMEM((1, N_EXPERTS), F32)],
        compiler_params=_cparams("arbitrary"),
        name="out_proj",
    )(ya, yb, yc, x2d, g_a.reshape(1, D_A), g_b.reshape(1, D_B), g_c.reshape(1, D_C), w_out_bf16,
      ln_g.reshape(1, D_MODEL), ln_b.reshape(1, D_MODEL), wr_hi, wr_lo, b_router.reshape(1, N_EXPERTS))


CAST_ROWS = 128


def _cast_weight(src_ref, dst_ref):
    def body(c, carry):
        r = pl.multiple_of(c * CAST_ROWS, CAST_ROWS)
        dst_ref[pl.ds(r, CAST_ROWS), :] = src_ref[0, 0, pl.ds(r, CAST_ROWS), :].astype(BF16)
        return carry

    lax.fori_loop(0, dst_ref.shape[0] // CAST_ROWS, body, 0)


def _moe_kernel(blk_e_ref, n_used_ref, tok_ref, tok_next_ref, x_ref, wgu_ref, bgu_ref, wdn_ref, bdn_ref, o_ref,
                wgu_s, wdn_s, xbuf_ref, xh_ref, sem_ref):
    i = pl.program_id(0)
    n_used = n_used_ref[0]
    slot = i % 2
    expert_changed = (i == 0) | (blk_e_ref[i] != blk_e_ref[jnp.maximum(i - 1, 0)])

    def row_copy(tok, r, s):
        return pltpu.make_async_copy(x_ref.at[pl.ds(tok, 1), :], xbuf_ref.at[s, pl.ds(r, 1), :], sem_ref.at[s])

    @pl.when(i == 0)
    def _():
        def body(r, carry):
            row_copy(tok_ref[0, 0, r], r, 0).start()
            return carry

        lax.fori_loop(0, MOE_ROWS, body, 0, unroll=8)

    @pl.when(expert_changed)
    def _():
        _cast_weight(wgu_ref, wgu_s)
        _cast_weight(wdn_ref, wdn_s)

    @pl.when(i <= n_used)
    def _():
        pltpu.make_async_copy(x_ref.at[pl.ds(0, MOE_ROWS), :], xbuf_ref.at[slot], sem_ref.at[slot]).wait()

    @pl.when(i < n_used)
    def _():
        packed = xbuf_ref[slot]
        xh_ref[:, :D_MODEL // 2] = pltpu.bitcast(packed << 16, F32).astype(BF16)
        xh_ref[:, D_MODEL // 2:] = pltpu.bitcast(packed & jnp.uint32(0xFFFF0000), F32).astype(BF16)
        cw = D_FF // MOE_CHUNKS
        per_chunk = MOE_ROWS // MOE_START_CHUNKS
        acc = None
        for c in range(MOE_CHUNKS):
            for r in range(min(c, MOE_START_CHUNKS) * per_chunk, min(c + 1, MOE_START_CHUNKS) * per_chunk):
                row_copy(tok_next_ref[0, 0, r], r, 1 - slot).start(priority=r % DMA_THREADS)
            aw = min(cw, D_MODEL // 2)
            words = xbuf_ref[slot, 0:SUBLANES, 0:aw]
            zero = ((words >> 16) >> 16).astype(F32)[0:1, :]
            zero = jnp.concatenate([zero] * (cw // aw), axis=1)
            xh = xh_ref[...]
            hg = (jnp.dot(xh, wgu_s[:, c * cw:(c + 1) * cw], preferred_element_type=F32)
                  + (bgu_ref[0, 0, :, c * cw:(c + 1) * cw] + zero))
            hu = (jnp.dot(xh, wgu_s[:, D_FF + c * cw:D_FF + (c + 1) * cw], preferred_element_type=F32)
                  + bgu_ref[0, 0, :, D_FF + c * cw:D_FF + (c + 1) * cw])
            gate = jnp.minimum(hg, SWIGLU_LIMIT)
            up = jnp.clip(hu, -SWIGLU_LIMIT, SWIGLU_LIMIT)
            glu = gate * jax.nn.sigmoid(gate * SWIGLU_ALPHA)
            act = ((up + 1.0) * glu).astype(BF16)
            part = jnp.dot(act, wdn_s[c * cw:(c + 1) * cw, :], preferred_element_type=F32)
            acc = part if acc is None else acc + part
        o_ref[...] = acc + bdn_ref[0, 0]

    @pl.when(i >= n_used)
    def _():
        o_ref[...] = jnp.zeros_like(o_ref)


def moe_experts(x, row_tok, blk_e, n_used, layer, w_gu, b_gu, w_dn, b_dn):
    n_rows = row_tok.shape[0]
    n_blocks = n_rows // MOE_ROWS
    depth = w_gu.shape[0]
    tok_blocks = row_tok.reshape(n_blocks, 1, MOE_ROWS)
    tok_spec = lambda index_map: pl.BlockSpec((1, 1, MOE_ROWS), index_map, memory_space=pltpu.SMEM)
    grid_spec = pltpu.PrefetchScalarGridSpec(
        num_scalar_prefetch=2,
        grid=(n_blocks,),
        in_specs=[
            tok_spec(lambda i, be, nu: (i, 0, 0)),
            tok_spec(lambda i, be, nu: (jnp.minimum(i + 1, n_blocks - 1), 0, 0)),
            pl.BlockSpec(memory_space=pl.ANY),
            pl.BlockSpec((1, 1, D_MODEL, 2 * D_FF), lambda i, be, nu: (layer, be[i], 0, 0)),
            pl.BlockSpec((1, 1, 1, 2 * D_FF), lambda i, be, nu: (layer, be[i], 0, 0)),
            pl.BlockSpec((1, 1, D_FF, D_MODEL), lambda i, be, nu: (layer, be[i], 0, 0)),
            pl.BlockSpec((1, 1, 1, D_MODEL), lambda i, be, nu: (layer, be[i], 0, 0)),
        ],
        out_specs=pl.BlockSpec((MOE_ROWS, D_MODEL), lambda i, be, nu: (i, 0)),
        scratch_shapes=[pltpu.VMEM((D_MODEL, 2 * D_FF), BF16), pltpu.VMEM((D_FF, D_MODEL), BF16),
                        pltpu.VMEM((2, MOE_ROWS, D_MODEL // 2), jnp.uint32), pltpu.VMEM((MOE_ROWS, D_MODEL), BF16),
                        pltpu.SemaphoreType.DMA((2,))],
    )
    return pl.pallas_call(
        _moe_kernel,
        grid_spec=grid_spec,
        out_shape=jax.ShapeDtypeStruct((n_rows, D_MODEL), F32),
        compiler_params=_cparams("arbitrary"),
        name="moe_experts",
    )(blk_e, n_used, tok_blocks, tok_blocks, x, w_gu, b_gu.reshape(depth, N_EXPERTS, 1, 2 * D_FF), w_dn,
      b_dn.reshape(depth, N_EXPERTS, 1, D_MODEL))


def route(top_i, pos, counts):
    n_tok = top_i.shape[0]
    n_slots = n_tok * TOP_K
    padded = ((counts + MOE_ROWS - 1) // MOE_ROWS) * MOE_ROWS
    pad_end = jnp.cumsum(padded)
    pad_start = pad_end - padded
    start = jnp.cumsum(counts) - counts
    experts = jnp.arange(N_EXPERTS, dtype=jnp.int32)
    slot_start = jnp.sum(jnp.where(top_i[:, :, None] == experts, pad_start, 0), axis=-1)
    dest = (slot_start + pos).astype(jnp.int32).T
    n_blocks = -(-n_slots // MOE_ROWS) + N_EXPERTS
    blk_first = jnp.arange(n_blocks, dtype=jnp.int32) * MOE_ROWS
    blk_e = jnp.minimum(jnp.sum((pad_end[None, :] <= blk_first[:, None]).astype(jnp.int32), axis=1), N_EXPERTS - 1)
    order = jnp.argsort(top_i.reshape(-1)).astype(jnp.int32)
    within = jnp.arange(MOE_ROWS, dtype=jnp.int32)[None, :] + (blk_first - pad_start[blk_e])[:, None]
    src = jnp.clip(start[blk_e][:, None] + within, 0, n_slots - 1).reshape(-1)
    valid = (within < counts[blk_e][:, None]).reshape(-1)
    row_tok = jnp.where(valid, order[src] // TOP_K, 0).astype(jnp.int32)
    n_used = (pad_end[-1:] // MOE_ROWS).astype(jnp.int32)
    return dest, row_tok, blk_e.astype(jnp.int32), n_used


COMBINE_ROWS = 256


def _combine_ln_kernel(dcur_ref, dnext_ref, x_ref, gate_ref, g_ref, b_ref, yb_ref, o_ref, buf_ref, sem_ref, *,
                       n_tiles):
    i = pl.program_id(0)
    tm = x_ref.shape[0]
    n_rows = TOP_K * tm
    slot = i % 2

    def row_copy(row, r, s):
        return pltpu.make_async_copy(yb_ref.at[pl.ds(row, 1), :], buf_ref.at[s, pl.ds(r, 1), :], sem_ref.at[s])

    def start_gather(d_ref, s):
        def body(r, carry):
            row_copy(d_ref[0, 0, r], r, s).start()
            return carry

        lax.fori_loop(0, n_rows, body, 0, unroll=8)

    @pl.when(i == 0)
    def _():
        start_gather(dcur_ref, 0)

    @pl.when(i + 1 < n_tiles)
    def _():
        for r in range(n_rows):
            row_copy(dnext_ref[0, 0, r], r, 1 - slot).start(priority=r % DMA_THREADS)

    pltpu.make_async_copy(yb_ref.at[pl.ds(0, n_rows), :], buf_ref.at[slot], sem_ref.at[slot]).wait()
    gates = gate_ref[...]
    moe = buf_ref[slot, 0:tm, :] * gates[:, 0:1]
    for k in range(1, TOP_K):
        moe = moe + buf_ref[slot, k * tm:(k + 1) * tm, :] * gates[:, k:k + 1]
    o_ref[...] = _layer_norm(DEEPNORM_ALPHA * x_ref[...] + moe, g_ref[...], b_ref[...])


def combine_ln(x2d, yb, dest, gates, ln_g, ln_b):
    n = x2d.shape[0]
    tm = min(COMBINE_ROWS, n)
    n_tiles = n // tm
    rows = pl.BlockSpec((tm, D_MODEL), lambda i: (i, 0))
    dest_tiles = dest.reshape(TOP_K, n_tiles, tm).transpose(1, 0, 2).reshape(n_tiles, 1, TOP_K * tm)
    dest_spec = lambda index_map: pl.BlockSpec((1, 1, TOP_K * tm), index_map, memory_space=pltpu.SMEM)
    return pl.pallas_call(
        functools.partial(_combine_ln_kernel, n_tiles=n_tiles),
        grid=(n_tiles,),
        in_specs=[dest_spec(lambda i: (i, 0, 0)), dest_spec(lambda i: (jnp.minimum(i + 1, n_tiles - 1), 0, 0)),
                  rows, pl.BlockSpec((tm, TOP_K), lambda i: (i, 0)), _full((1, D_MODEL)), _full((1, D_MODEL)),
                  pl.BlockSpec(memory_space=pl.ANY)],
        out_specs=rows,
        out_shape=jax.ShapeDtypeStruct((n, D_MODEL), F32),
        scratch_shapes=[pltpu.VMEM((2, TOP_K * tm, D_MODEL), F32), pltpu.SemaphoreType.DMA((2,))],
        compiler_params=_cparams("arbitrary"),
        name="combine_ln",
    )(dest_tiles, dest_tiles, x2d, gates, ln_g.reshape(1, D_MODEL), ln_b.reshape(1, D_MODEL), yb)


def _block_diag(w):
    h, d, _ = w.shape
    return jnp.einsum('hij,hg->higj', w.astype(F32), jnp.eye(h, dtype=F32)).reshape(h * d, h * d)


def _mixer(x, state, p, attn_bias):
    bn, seq, _ = x.shape
    n = bn * seq
    conv_buf, h0, s0, k_cache, v_cache = state
    x2d = x.reshape(n, D_MODEL)
    u = in_proj(x2d, p['w_in']).reshape(bn, seq, IN_COLS)

    y_a, new_buf, h_last = rglru(u, conv_buf, h0, p['conv_w'], p['conv_b'], p['w_r'], p['b_r'], p['w_i'], p['b_i'],
                                 p['lam'])
    q_col, k_col, v_col = OFF_Q // LANE, OFF_K // LANE, OFF_V // LANE
    if k_cache is None:
        qb = min(BAND_PAST, seq)
        y_b = attention(u, q_col, u, k_col, u, v_col, u, k_col, v_col, attn_bias, qb=qb, pb=qb,
                        prev_is_same_array=True)
    else:
        y_b = attention(u, q_col, k_cache, 0, v_cache, 0, u, k_col, v_col, attn_bias, qb=seq,
                        pb=k_cache.shape[1], prev_is_same_array=False)
    y_c, s_last = s5(u, s0, p['bmat'], p['scan_tables'], p['cmat'], p['d_c'], p['w_glu'], p['b_glu'])

    x1, x1p, top_i, gates, pos, counts = out_proj(
        y_a.reshape(n, D_A), y_b.reshape(n, D_B), y_c.reshape(n, D_C), x2d, p['g_a'], p['g_b'], p['g_c'],
        p['w_out'], p['ln1_g'], p['ln1_b'], p['w_router'], p['b_router'])

    k_rows = u[:, :, OFF_K:OFF_V]
    v_rows = u[:, :, OFF_V:OFF_C]
    if k_cache is None:
        keep = min(BAND_PAST, seq)
        k_rows = k_rows[:, seq - keep:]
        v_rows = v_rows[:, seq - keep:]
    k_rows = k_rows.reshape(bn, -1, B_HEADS, B_HEAD_DIM)
    v_rows = v_rows.reshape(bn, -1, B_HEADS, B_HEAD_DIM)
    s_re = s_last[:, 0, :C_LANES].reshape(bn, C_GROUPS, C_STATE)
    s_im = s_last[:, 0, C_LANES:].reshape(bn, C_GROUPS, C_STATE)
    routed = dict(x1=x1, x1p=x1p, top_i=top_i, gates=gates, pos=pos, counts=counts.reshape(N_EXPERTS))
    return routed, (new_buf, h_last.reshape(bn, D_A), k_rows, v_rows, s_re, s_im)


def _moe(groups, p):
    experts = jnp.arange(N_EXPERTS, dtype=jnp.int32)
    seen = jnp.zeros((N_EXPERTS,), jnp.int32)
    pos_all = []
    for g in groups:
        pos_all.append(g['pos'] + jnp.sum(jnp.where(g['top_i'][:, :, None] == experts, seen, 0), axis=-1))
        seen = seen + g['counts']
    top_i = jnp.concatenate([g['top_i'] for g in groups], axis=0)
    dest, row_tok, blk_e, n_used = route(top_i, jnp.concatenate(pos_all, axis=0), seen)
    x1p = jnp.concatenate([g['x1p'] for g in groups], axis=0)
    yb = moe_experts(x1p, row_tok, blk_e, n_used, p['layer'], p['w_gu'], p['b_gu'], p['w_dn'], p['b_dn'])
    outs, first = [], 0
    for g in groups:
        n = g['x1'].shape[0]
        outs.append(combine_ln(g['x1'], yb, dest[:, first:first + n], g['gates'], p['ln2_g'], p['ln2_b']))
        first += n
    return outs


def kernel(x_prompt, x_sample, cache_conv_a, state_h_a, cache_k_b, cache_v_b, state_s_re_c, state_s_im_c, w_in, conv_w_a, conv_b_a, w_r_a, b_r_a, w_i_a, b_i_a, lambda_a, rel_bias_b, a_re_c, a_im_c, log_dt_c, b_re_c, b_im_c, c_re_c, c_im_c, d_c, w_glu_c, b_glu_c, g_norm_a, g_norm_b, g_norm_c, w_out, ln1_g, ln1_b, w_router, b_router, w_gu, b_gu, w_dn, b_dn, ln2_g, ln2_b):
    bp, seq_p, _ = x_prompt.shape
    bs, seq_s, _ = x_sample.shape
    kv_rows = cache_k_b.shape[2]
    yp, ys = x_prompt, x_sample
    p_states = [[] for _ in range(6)]
    s_states = [[] for _ in range(6)]
    for l in range(DEPTH):
        bmat, scan_tables, cmat = s5_params(a_re_c[l], a_im_c[l], log_dt_c[l], b_re_c[l], b_im_c[l], c_re_c[l],
                                           c_im_c[l])
        p = dict(
            w_in=w_in[l].astype(BF16), conv_w=conv_w_a[l], conv_b=conv_b_a[l],
            w_r=_block_diag(w_r_a[l]).astype(BF16), b_r=b_r_a[l], w_i=_block_diag(w_i_a[l]).astype(BF16),
            b_i=b_i_a[l], lam=lambda_a[l],
            bmat=bmat, scan_tables=scan_tables, cmat=cmat, d_c=d_c[l], w_glu=w_glu_c[l].astype(BF16), b_glu=b_glu_c[l],
            g_a=g_norm_a[l], g_b=g_norm_b[l], g_c=g_norm_c[l], w_out=w_out[l].astype(BF16),
            ln1_g=ln1_g[l], ln1_b=ln1_b[l], w_router=w_router[l], b_router=b_router[l],
            layer=l, w_gu=w_gu, b_gu=b_gu, w_dn=w_dn, b_dn=b_dn,
            ln2_g=ln2_g[l], ln2_b=ln2_b[l],
        )
        zero_state = (jnp.zeros((bp, CONV_W - 1, D_A), F32), jnp.zeros((bp, D_A), F32),
                      jnp.zeros((bp, 1, 2 * C_LANES), F32), None, None)
        qb = min(BAND_PAST, seq_p)
        routed_p, st = _mixer(yp, zero_state, p, bias_table(rel_bias_b[l], qb, qb, band_mask=True))
        for lst, s in zip(p_states, st):
            lst.append(s)
        s0 = jnp.concatenate([state_s_re_c[l].reshape(bs, 1, C_LANES), state_s_im_c[l].reshape(bs, 1, C_LANES)],
                             axis=-1)
        sample_state = (cache_conv_a[l], state_h_a[l], s0,
                        cache_k_b[l].reshape(bs, kv_rows, D_B), cache_v_b[l].reshape(bs, kv_rows, D_B))
        routed_s, st = _mixer(ys, sample_state, p, bias_table(rel_bias_b[l], seq_s, kv_rows, band_mask=False))
        for lst, s in zip(s_states, st):
            lst.append(s)
        yp, ys = _moe([routed_p, routed_s], p)
        yp = yp.reshape(bp, seq_p, D_MODEL)
        ys = ys.reshape(bs, seq_s, D_MODEL)
    return (yp, ys) + tuple(jnp.stack(s) for s in p_states) + tuple(jnp.stack(s) for s in s_states)
```

```python
import functools
import math

import jax
import jax.numpy as jnp
import numpy as np
from jax import lax
from jax.experimental import pallas as pl
from jax.experimental.pallas import tpu as pltpu

F32 = jnp.float32
BF16 = jnp.bfloat16

D_MODEL = 1024
DEPTH = 2
CHUNK = 64
PREV_CHUNKS = 8
BAND_PAST = PREV_CHUNKS * CHUNK
D_A = D_MODEL // 4
D_B = D_MODEL // 2
D_C = D_MODEL // 4
A_HEADS = 4
A_HEAD_DIM = D_A // A_HEADS
CONV_W = 4
RG_C = 8.0
B_HEADS = 8
B_HEAD_DIM = D_B // B_HEADS
REL_CLIP = 128
C_GW = 16
C_GROUPS = D_C // C_GW
C_STATE = 64
C_LANES = C_GROUPS * C_STATE
OFF_GA = D_A
OFF_Q = 2 * D_A
OFF_K = OFF_Q + D_B
OFF_V = OFF_K + D_B
OFF_C = OFF_V + D_B
IN_COLS = OFF_C + D_C
N_EXPERTS = 32
TOP_K = 4
D_FF = D_MODEL
SWIGLU_LIMIT = 7.0
SWIGLU_ALPHA = 1.702
DEEPNORM_ALPHA = (2 * DEPTH) ** 0.25
LN_EPS = 1e-5
RMS_EPS = 1e-6
NEG_INF = -1e30

LANE = 128
SUBLANES = 8
HEADS_PER_SLAB = LANE // B_HEAD_DIM
VMEM_LIMIT = 56 * 1024 * 1024

ROW_TILE = 512
MOE_ROWS = 512
MOE_CHUNKS = 1
MOE_ROW_SPLITS = 2
DMA_THREADS = 2
MOE_START_CHUNKS = 1
ATTN_SUB_ROWS = 256


def _cparams(*sem):
    return pltpu.CompilerParams(dimension_semantics=sem, vmem_limit_bytes=VMEM_LIMIT)


def _full(shape):
    return pl.BlockSpec(shape, lambda *_: (0,) * len(shape))


def _in_proj_kernel(x_ref, w_ref, o_ref):
    o_ref[...] = jnp.dot(x_ref[...].astype(BF16), w_ref[...], preferred_element_type=F32)


def in_proj(x2d, w_bf16):
    n = x2d.shape[0]
    tm = min(ROW_TILE, n)
    return pl.pallas_call(
        _in_proj_kernel,
        grid=(n // tm,),
        in_specs=[pl.BlockSpec((tm, D_MODEL), lambda i: (i, 0)), _full((D_MODEL, IN_COLS))],
        out_specs=pl.BlockSpec((tm, IN_COLS), lambda i: (i, 0)),
        out_shape=jax.ShapeDtypeStruct((n, IN_COLS), F32),
        compiler_params=_cparams("parallel"),
        name="in_proj",
    )(x2d, w_bf16)


def _rglru_kernel(xa_ref, ga_ref, buf_ref, h0_ref, cw_ref, cb_ref, wr_ref, br_ref, wi_ref, bi_ref, lam_ref,
                  y_ref, nbuf_ref, hl_ref, xp_ref, hc_ref, as_ref, hs_ref, *, t):
    pad = SUBLANES
    hist = CONV_W - 1

    @pl.when(pl.program_id(1) == 0)
    def _():
        xp_ref[0:pad, :] = jnp.zeros((pad, D_A), F32)
        xp_ref[pad - hist:pad, :] = buf_ref[0]
        hc_ref[...] = h0_ref[0]

    xa = xa_ref[0]
    xp_ref[pad:pad + t, :] = xa
    xc = cb_ref[...] + xa * cw_ref[hist:hist + 1, :]
    for j in range(hist):
        xc = xc + xp_ref[pad - hist + j:pad - hist + j + t, :] * cw_ref[j:j + 1, :]
    tail = xp_ref[pad + t - hist:pad + t, :]
    nbuf_ref[0] = tail
    xp_ref[pad - hist:pad, :] = tail

    xch = xc.astype(BF16)
    r = jax.nn.sigmoid(jnp.dot(xch, wr_ref[...], preferred_element_type=F32) + br_ref[...])
    i = jax.nn.sigmoid(jnp.dot(xch, wi_ref[...], preferred_element_type=F32) + bi_ref[...])
    lam = lam_ref[...]
    softplus_neg_lam = jnp.maximum(-lam, 0.0) + jnp.log(1.0 + jnp.exp(-jnp.abs(lam)))
    log_a = (-RG_C) * r * softplus_neg_lam
    a = jnp.exp(log_a)
    b = jnp.sqrt(1.0 - jnp.exp(2.0 * log_a)) * (i * xc)

    row_in_group = lax.broadcasted_iota(jnp.int32, (t, D_A), 0) & (SUBLANES - 1)
    acc_a, acc_b = a, b
    sh = 1
    while sh < SUBLANES:
        m = row_in_group >= sh
        acc_b = jnp.where(m, acc_a * pltpu.roll(acc_b, sh, 0) + acc_b, acc_b)
        acc_a = jnp.where(m, acc_a * pltpu.roll(acc_a, sh, 0), acc_a)
        sh *= 2
    as_ref[...] = acc_a
    hs_ref[...] = acc_b

    def group(g, carry):
        rows = pl.ds(pl.multiple_of(g * SUBLANES, SUBLANES), SUBLANES)
        h = hs_ref[rows, :] + as_ref[rows, :] * carry
        hs_ref[rows, :] = h
        return h[SUBLANES - 1:SUBLANES, :]

    h_last = lax.fori_loop(0, t // SUBLANES, group, hc_ref[...], unroll=8)
    hc_ref[...] = h_last
    hl_ref[0] = h_last
    y_ref[0] = hs_ref[...] * jax.nn.gelu(ga_ref[0])


def rglru(u, conv_buf, h0, cw, cb, wr_bd, br, wi_bd, bi, lam):
    bn, seq, _ = u.shape
    t = min(ROW_TILE, seq)
    vec = _full((1, D_A))
    return pl.pallas_call(
        functools.partial(_rglru_kernel, t=t),
        grid=(bn, seq // t),
        in_specs=[
            pl.BlockSpec((1, t, D_A), lambda b, s: (b, s, 0)),
            pl.BlockSpec((1, t, D_A), lambda b, s: (b, s, OFF_GA // D_A)),
            pl.BlockSpec((1, CONV_W - 1, D_A), lambda b, s: (b, 0, 0)),
            pl.BlockSpec((1, 1, D_A), lambda b, s: (b, 0, 0)),
            _full((CONV_W, D_A)), vec, _full((D_A, D_A)), vec, _full((D_A, D_A)), vec, vec,
        ],
        out_specs=[
            pl.BlockSpec((1, t, D_A), lambda b, s: (b, s, 0)),
            pl.BlockSpec((1, CONV_W - 1, D_A), lambda b, s: (b, 0, 0)),
            pl.BlockSpec((1, 1, D_A), lambda b, s: (b, 0, 0)),
        ],
        out_shape=[
            jax.ShapeDtypeStruct((bn, seq, D_A), F32),
            jax.ShapeDtypeStruct((bn, CONV_W - 1, D_A), F32),
            jax.ShapeDtypeStruct((bn, 1, D_A), F32),
        ],
        scratch_shapes=[pltpu.VMEM((t + SUBLANES, D_A), F32), pltpu.VMEM((1, D_A), F32),
                        pltpu.VMEM((t, D_A), F32), pltpu.VMEM((t, D_A), F32)],
        compiler_params=_cparams("parallel", "arbitrary"),
        name="rglru",
    )(u, u, conv_buf, h0.reshape(bn, 1, D_A), cw, cb.reshape(1, D_A), wr_bd, br.reshape(1, D_A),
      wi_bd, bi.reshape(1, D_A), lam.reshape(1, D_A))


def _attn_kernel(q_ref, kp_ref, kc_ref, vp_ref, vc_ref, bias_ref, o_ref, *, pb, sq, mask_first_prev):
    scale = B_HEAD_DIM ** -0.5
    qb = q_ref.shape[1]
    q_all = (q_ref[0] * scale).astype(BF16)
    kp_all = kp_ref[0].astype(BF16)
    kc_all = kc_ref[0].astype(BF16)
    vp_all = vp_ref[0].astype(BF16)
    vc_all = vc_ref[0].astype(BF16)
    lane = lax.broadcasted_iota(jnp.int32, (1, LANE), 1)
    contract_last = (((1,), (1,)), ((), ()))
    for r in range(qb // sq):
        lo = max(pb + r * sq - BAND_PAST, 0)
        hi = (r + 1) * sq
        q = q_all[r * sq:hi, :]
        kp, vp = kp_all[lo:pb, :], vp_all[lo:pb, :]
        kc, vc = kc_all[0:hi, :], vc_all[0:hi, :]
        out = None
        for hh in range(HEADS_PER_SLAB):
            in_head = (lane // B_HEAD_DIM) == hh
            qh = jnp.where(in_head, q, jnp.zeros_like(q))
            sp = (lax.dot_general(qh, kp, contract_last, preferred_element_type=F32)
                  + bias_ref[hh, r * sq:hi, lo:pb])
            sc = (lax.dot_general(qh, kc, contract_last, preferred_element_type=F32)
                  + bias_ref[hh, r * sq:hi, pb:pb + hi])
            if mask_first_prev:
                sp = jnp.where(pl.program_id(2) == 0, NEG_INF, sp)
            m = jnp.maximum(jnp.max(sp, axis=-1, keepdims=True), jnp.max(sc, axis=-1, keepdims=True))
            ep = jnp.exp(sp - m)
            ec = jnp.exp(sc - m)
            denom = jnp.sum(ep, axis=-1, keepdims=True) + jnp.sum(ec, axis=-1, keepdims=True)
            o = (jnp.dot(ep.astype(BF16), vp, preferred_element_type=F32)
                 + jnp.dot(ec.astype(BF16), vc, preferred_element_type=F32)) / denom
            out = o if out is None else jnp.where(in_head, o, out)
        o_ref[0, r * sq:hi, :] = out


def attention(q_arr, q_col, kprev_arr, kprev_col, vprev_arr, vprev_col, kv_arr, k_col, v_col, bias, *, qb, pb,
              prev_is_same_array):
    bn, seq, _ = q_arr.shape
    n_slabs = B_HEADS // HEADS_PER_SLAB
    if prev_is_same_array:
        prev_map = lambda col: (lambda hp, b, s: (b, jnp.maximum(s - 1, 0), col + hp))
    else:
        prev_map = lambda col: (lambda hp, b, s: (b, 0, col + hp))
    cur_map = lambda col: (lambda hp, b, s: (b, s, col + hp))
    return pl.pallas_call(
        functools.partial(_attn_kernel, pb=pb, sq=min(qb, ATTN_SUB_ROWS), mask_first_prev=prev_is_same_array),
        grid=(n_slabs, bn, seq // qb),
        in_specs=[
            pl.BlockSpec((1, qb, LANE), cur_map(q_col)),
            pl.BlockSpec((1, pb, LANE), prev_map(kprev_col)),
            pl.BlockSpec((1, qb, LANE), cur_map(k_col)),
            pl.BlockSpec((1, pb, LANE), prev_map(vprev_col)),
            pl.BlockSpec((1, qb, LANE), cur_map(v_col)),
            pl.BlockSpec((HEADS_PER_SLAB, qb, pb + qb), lambda hp, b, s: (hp, 0, 0)),
        ],
        out_specs=pl.BlockSpec((1, qb, LANE), lambda hp, b, s: (b, s, hp)),
        out_shape=jax.ShapeDtypeStruct((bn, seq, D_B), F32),
        compiler_params=_cparams("arbitrary", "arbitrary", "arbitrary"),
        name="attention",
    )(q_arr, kprev_arr, kv_arr, vprev_arr, kv_arr, bias)


def _bias_kernel(v_ref, o_ref, *, band_mask):
    qb, width = o_ref.shape[1], o_ref.shape[2]
    period = v_ref.shape[2]
    table = pltpu.roll(jnp.broadcast_to(v_ref[0], (qb, period)), 0, 1, stride=1, stride_axis=0)[:, :width]
    if band_mask:
        i = lax.broadcasted_iota(jnp.int32, (qb, width), 0)
        j = lax.broadcasted_iota(jnp.int32, (qb, width), 1)
        dc = i // CHUNK - j // CHUNK + PREV_CHUNKS
        table = jnp.where((dc >= 0) & (dc <= PREV_CHUNKS), table, NEG_INF)
    o_ref[0] = table


def bias_table(rel_bias, qb, pb, band_mask):
    n_heads = rel_bias.shape[0]
    period = pb + 2 * qb
    m = np.arange(period)
    d = np.where(m < pb + qb, m, m - period)
    idx = np.clip(pb - d, -REL_CLIP, REL_CLIP) + REL_CLIP
    v = rel_bias.astype(F32)[:, idx].reshape(n_heads, 1, period)
    return pl.pallas_call(
        functools.partial(_bias_kernel, band_mask=band_mask),
        grid=(n_heads,),
        in_specs=[pl.BlockSpec((1, 1, period), lambda h: (h, 0, 0))],
        out_specs=pl.BlockSpec((1, qb, pb + qb), lambda h: (h, 0, 0)),
        out_shape=jax.ShapeDtypeStruct((n_heads, qb, pb + qb), F32),
        compiler_params=_cparams("parallel"),
        name="bias_table",
    )(v)


def _s5_kernel(u_ref, s0_ref, bmat_ref, ar_ref, ai_ref, cmat_ref, d_ref, wg_ref, bg_ref,
               y_ref, sl_ref, sc_ref, xs_ref, *, steps):
    @pl.when(pl.program_id(1) == 0)
    def _():
        sc_ref[...] = s0_ref[0]

    u = u_ref[0]
    xs_ref[...] = jnp.dot(u.astype(BF16), bmat_ref[...], preferred_element_type=F32)
    a_re = ar_ref[...]
    a_im = ai_ref[...]

    def step(t, s):
        rows = pl.ds(pl.multiple_of(t * SUBLANES, SUBLANES), SUBLANES)
        s = xs_ref[rows, :] + a_re * s + a_im * pltpu.roll(s, C_LANES, 1)
        xs_ref[rows, :] = s
        return s

    s_last = lax.fori_loop(0, steps, step, sc_ref[...], unroll=4)
    sc_ref[...] = s_last
    sl_ref[0] = s_last
    y = jnp.dot(xs_ref[...].astype(BF16), cmat_ref[...], preferred_element_type=F32) + d_ref[...] * u
    y = jax.nn.gelu(y)
    gate = jax.nn.sigmoid(jnp.dot(y.astype(BF16), wg_ref[...], preferred_element_type=F32) + bg_ref[...])
    y_ref[0] = y * gate


S5_STEPS = 64


def s5(u, s0, bmat, a_tables, cmat, d_skip, w_glu, b_glu):
    bn, seq, _ = u.shape
    groups = bn // SUBLANES
    steps = min(S5_STEPS, seq)
    rows = steps * SUBLANES
    uc = u[:, :, OFF_C:].reshape(groups, SUBLANES, seq, D_C).transpose(0, 2, 1, 3).reshape(groups, seq * SUBLANES, D_C)
    y_tm, s_last = pl.pallas_call(
        functools.partial(_s5_kernel, steps=steps),
        grid=(groups, seq // steps),
        in_specs=[
            pl.BlockSpec((1, rows, D_C), lambda g, s: (g, s, 0)),
            pl.BlockSpec((1, SUBLANES, 2 * C_LANES), lambda g, s: (g, 0, 0)),
            _full((D_C, 2 * C_LANES)), _full((SUBLANES, 2 * C_LANES)), _full((SUBLANES, 2 * C_LANES)),
            _full((2 * C_LANES, D_C)), _full((1, D_C)), _full((D_C, D_C)), _full((1, D_C)),
        ],
        out_specs=[
            pl.BlockSpec((1, rows, D_C), lambda g, s: (g, s, 0)),
            pl.BlockSpec((1, SUBLANES, 2 * C_LANES), lambda g, s: (g, 0, 0)),
        ],
        out_shape=[
            jax.ShapeDtypeStruct((groups, seq * SUBLANES, D_C), F32),
            jax.ShapeDtypeStruct((groups, SUBLANES, 2 * C_LANES), F32),
        ],
        scratch_shapes=[pltpu.VMEM((SUBLANES, 2 * C_LANES), F32), pltpu.VMEM((rows, 2 * C_LANES), F32)],
        compiler_params=_cparams("parallel", "arbitrary"),
        name="s5",
    )(uc, s0.reshape(groups, SUBLANES, 2 * C_LANES), bmat, *a_tables, cmat, d_skip.reshape(1, D_C), w_glu,
      b_glu.reshape(1, D_C))
    y = y_tm.reshape(groups, seq, SUBLANES, D_C).transpose(0, 2, 1, 3).reshape(bn, seq, D_C)
    return y, s_last.reshape(bn, 1, 2 * C_LANES)


def s5_params(a_re, a_im, log_dt, b_re, b_im, c_re, c_im):
    lam = lax.complex(a_re.astype(F32), a_im.astype(F32))
    dt = jnp.exp(log_dt.astype(F32))[:, None]
    a_bar = jnp.exp(lam * dt)
    b_bar = ((a_bar - 1.0) / lam)[:, :, None] * lax.complex(b_re.astype(F32), b_im.astype(F32))
    eye = jnp.eye(C_GROUPS, dtype=F32)

    def block_in(m):
        return jnp.einsum('gpi,gh->gihp', m, eye).reshape(D_C, C_LANES)

    def block_out(m):
        return jnp.einsum('gip,gh->gphi', m, eye).reshape(C_LANES, D_C)

    bmat = jnp.concatenate([block_in(jnp.real(b_bar)), block_in(jnp.imag(b_bar))], axis=1)
    cmat = jnp.concatenate([block_out(c_re.astype(F32)), -block_out(c_im.astype(F32))], axis=0)
    re = jnp.real(a_bar).reshape(1, C_LANES)
    im = jnp.imag(a_bar).reshape(1, C_LANES)
    a_re2 = jnp.broadcast_to(jnp.concatenate([re, re], axis=1), (SUBLANES, 2 * C_LANES))
    a_im2 = jnp.broadcast_to(jnp.concatenate([-im, im], axis=1), (SUBLANES, 2 * C_LANES))
    return bmat.astype(BF16), (a_re2, a_im2), cmat.astype(BF16)


def _rms(x, g):
    return x * lax.rsqrt(jnp.mean(jnp.square(x), axis=-1, keepdims=True) + RMS_EPS) * g


def _layer_norm(x, g, b):
    mu = jnp.mean(x, axis=-1, keepdims=True)
    xc = x - mu
    var = jnp.mean(jnp.square(xc), axis=-1, keepdims=True)
    return xc * lax.rsqrt(var + LN_EPS) * g + b


def _out_proj_kernel(ya_ref, yb_ref, yc_ref, x_ref, ga_ref, gb_ref, gc_ref, w_ref, lg_ref, lb_ref,
                     wrh_ref, wrl_ref, br_ref,
                     x1_ref, x1p_ref, topi_ref, gate_ref, pos_ref, cnt_ref, seen_ref):
    @pl.when(pl.program_id(0) == 0)
    def _():
        seen_ref[...] = jnp.zeros_like(seen_ref)

    m = jnp.dot(_rms(ya_ref[...], ga_ref[...]).astype(BF16), w_ref[0:D_A, :], preferred_element_type=F32)
    m += jnp.dot(_rms(yb_ref[...], gb_ref[...]).astype(BF16), w_ref[D_A:D_A + D_B, :], preferred_element_type=F32)
    m += jnp.dot(_rms(yc_ref[...], gc_ref[...]).astype(BF16), w_ref[D_A + D_B:, :], preferred_element_type=F32)
    x1 = _layer_norm(DEEPNORM_ALPHA * x_ref[...] + m, lg_ref[...], lb_ref[...])
    x1_ref[...] = x1
    x1h = x1.astype(BF16)
    bits = pltpu.bitcast(x1h.astype(F32), jnp.uint32)
    x1p_ref[...] = (bits[:, :D_MODEL // 2] >> 16) | (bits[:, D_MODEL // 2:] & jnp.uint32(0xFFFF0000))

    x1l = (x1 - x1h.astype(F32)).astype(BF16)
    logits = (jnp.dot(x1h, wrh_ref[...], preferred_element_type=F32)
              + jnp.dot(x1l, wrh_ref[...], preferred_element_type=F32)
              + jnp.dot(x1h, wrl_ref[...], preferred_element_type=F32)) + br_ref[...]

    tm = logits.shape[0]
    col = lax.broadcasted_iota(jnp.int32, (tm, N_EXPERTS), 1).astype(F32)
    work = logits
    sels, vals, idxs = [], [], []
    for _ in range(TOP_K):
        top = jnp.max(work, axis=1, keepdims=True)
        idx = jnp.min(jnp.where(work == top, col, float(N_EXPERTS)), axis=1, keepdims=True)
        sel = col == idx
        work = jnp.where(sel, -jnp.inf, work)
        sels.append(sel)
        vals.append(top)
        idxs.append(idx)
    exps = [jnp.exp(v - vals[0]) for v in vals]
    total = functools.reduce(lambda a, b: a + b, exps)

    chosen = functools.reduce(lambda a, b: a + b, [s.astype(F32) for s in sels])
    ri = lax.broadcasted_iota(jnp.int32, (tm, tm), 0)
    ci = lax.broadcasted_iota(jnp.int32, (tm, tm), 1)
    earlier = jnp.where(ci < ri, 1.0, 0.0).astype(BF16)
    before = jnp.dot(earlier, chosen.astype(BF16), preferred_element_type=F32) + seen_ref[...]
    seen = seen_ref[...] + jnp.sum(chosen, axis=0, keepdims=True)
    seen_ref[...] = seen
    cnt_ref[...] = seen.astype(jnp.int32)

    slot = lax.broadcasted_iota(jnp.int32, (tm, TOP_K), 1)
    top_i = jnp.zeros((tm, TOP_K), F32)
    gates = jnp.zeros((tm, TOP_K), F32)
    pos = jnp.zeros((tm, TOP_K), F32)
    for k in range(TOP_K):
        rank = jnp.sum(jnp.where(sels[k], before, 0.0), axis=1, keepdims=True)
        top_i = jnp.where(slot == k, idxs[k], top_i)
        gates = jnp.where(slot == k, exps[k] / total, gates)
        pos = jnp.where(slot == k, rank, pos)
    topi_ref[...] = top_i.astype(jnp.int32)
    gate_ref[...] = gates
    pos_ref[...] = pos.astype(jnp.int32)


def out_proj(ya, yb, yc, x2d, g_a, g_b, g_c, w_out_bf16, ln_g, ln_b, w_router, b_router):
    n = x2d.shape[0]
    tm = min(ROW_TILE, n)
    rows = lambda w: pl.BlockSpec((tm, w), lambda i: (i, 0))
    wr = w_router.astype(F32)
    wr_hi = wr.astype(BF16)
    wr_lo = (wr - wr_hi.astype(F32)).astype(BF16)
    return pl.pallas_call(
        _out_proj_kernel,
        grid=(n // tm,),
        in_specs=[rows(D_A), rows(D_B), rows(D_C), rows(D_MODEL),
                  _full((1, D_A)), _full((1, D_B)), _full((1, D_C)), _full((D_MODEL, D_MODEL)),
                  _full((1, D_MODEL)), _full((1, D_MODEL)),
                  _full((D_MODEL, N_EXPERTS)), _full((D_MODEL, N_EXPERTS)), _full((1, N_EXPERTS))],
        out_specs=[rows(D_MODEL), rows(D_MODEL // 2), rows(TOP_K), rows(TOP_K), rows(TOP_K),
                   _full((1, N_EXPERTS))],
        out_shape=[jax.ShapeDtypeStruct((n, D_MODEL), F32), jax.ShapeDtypeStruct((n, D_MODEL // 2), jnp.uint32),
                   jax.ShapeDtypeStruct((n, TOP_K), jnp.int32), jax.ShapeDtypeStruct((n, TOP_K), F32),
                   jax.ShapeDtypeStruct((n, TOP_K), jnp.int32), jax.ShapeDtypeStruct((1, N_EXPERTS), jnp.int32)],
        scratch_shapes=[pltpu.VMEM((1, N_EXPERTS), F32)],
        compiler_params=_cparams("arbitrary"),
        name="out_proj",
    )(ya, yb, yc, x2d, g_a.reshape(1, D_A), g_b.reshape(1, D_B), g_c.reshape(1, D_C), w_out_bf16,
      ln_g.reshape(1, D_MODEL), ln_b.reshape(1, D_MODEL), wr_hi, wr_lo, b_router.reshape(1, N_EXPERTS))


CAST_ROWS = 128


def _cast_weight(src_ref, dst_ref):
    def body(c, carry):
        r = pl.multiple_of(c * CAST_ROWS, CAST_ROWS)
        dst_ref[pl.ds(r, CAST_ROWS), :] = src_ref[0, 0, pl.ds(r, CAST_ROWS), :].astype(BF16)
        return carry

    lax.fori_loop(0, dst_ref.shape[0] // CAST_ROWS, body, 0)


def _moe_kernel(blk_e_ref, n_used_ref, tok_ref, tok_next_ref, x_ref, wgu_ref, bgu_ref, wdn_ref, bdn_ref, o_ref,
                wgu_s, wdn_s, xbuf_ref, xh_ref, sem_ref):
    i = pl.program_id(0)
    n_used = n_used_ref[0]
    slot = i % 2
    expert_changed = (i == 0) | (blk_e_ref[i] != blk_e_ref[jnp.maximum(i - 1, 0)])

    def row_copy(tok, r, s):
        return pltpu.make_async_copy(x_ref.at[pl.ds(tok, 1), :], xbuf_ref.at[s, pl.ds(r, 1), :], sem_ref.at[s])

    @pl.when(i == 0)
    def _():
        def body(r, carry):
            row_copy(tok_ref[0, 0, r], r, 0).start()
            return carry

        lax.fori_loop(0, MOE_ROWS, body, 0, unroll=8)

    @pl.when(expert_changed)
    def _():
        _cast_weight(wgu_ref, wgu_s)
        _cast_weight(wdn_ref, wdn_s)

    @pl.when(i <= n_used)
    def _():
        pltpu.make_async_copy(x_ref.at[pl.ds(0, MOE_ROWS), :], xbuf_ref.at[slot], sem_ref.at[slot]).wait()

    @pl.when(i < n_used)
    def _():
        packed = xbuf_ref[slot]
        xh_ref[:, :D_MODEL // 2] = pltpu.bitcast(packed << 16, F32).astype(BF16)
        xh_ref[:, D_MODEL // 2:] = pltpu.bitcast(packed & jnp.uint32(0xFFFF0000), F32).astype(BF16)
        cw = D_FF // MOE_CHUNKS
        per_chunk = MOE_ROWS // MOE_START_CHUNKS
        half = MOE_ROWS // MOE_ROW_SPLITS
        acc = [None] * MOE_ROW_SPLITS
        for c in range(MOE_CHUNKS):
            for r in range(min(c, MOE_START_CHUNKS) * per_chunk, min(c + 1, MOE_START_CHUNKS) * per_chunk):
                row_copy(tok_next_ref[0, 0, r], r, 1 - slot).start(priority=r % DMA_THREADS)
            aw = min(cw, D_MODEL // 2)
            words = xbuf_ref[slot, 0:SUBLANES, 0:aw]
            zero = ((words >> 16) >> 16).astype(F32)[0:1, :]
            zero = jnp.concatenate([zero] * (cw // aw), axis=1)
            for s in range(MOE_ROW_SPLITS):
                xh = xh_ref[s * half:(s + 1) * half, :]
                hg = (jnp.dot(xh, wgu_s[:, c * cw:(c + 1) * cw], preferred_element_type=F32)
                      + (bgu_ref[0, 0, :, c * cw:(c + 1) * cw] + zero))
                hu = (jnp.dot(xh, wgu_s[:, D_FF + c * cw:D_FF + (c + 1) * cw], preferred_element_type=F32)
                      + bgu_ref[0, 0, :, D_FF + c * cw:D_FF + (c + 1) * cw])
                gate = jnp.minimum(hg, SWIGLU_LIMIT)
                up = jnp.clip(hu, -SWIGLU_LIMIT, SWIGLU_LIMIT)
                glu = gate * jax.nn.sigmoid(gate * SWIGLU_ALPHA)
                act = ((up + 1.0) * glu).astype(BF16)
                part = jnp.dot(act, wdn_s[c * cw:(c + 1) * cw, :], preferred_element_type=F32)
                acc[s] = part if acc[s] is None else acc[s] + part
        for s in range(MOE_ROW_SPLITS):
            o_ref[s * half:(s + 1) * half, :] = acc[s] + bdn_ref[0, 0]

    @pl.when(i >= n_used)
    def _():
        o_ref[...] = jnp.zeros_like(o_ref)


def moe_experts(x, row_tok, blk_e, n_used, layer, w_gu, b_gu, w_dn, b_dn):
    n_rows = row_tok.shape[0]
    n_blocks = n_rows // MOE_ROWS
    depth = w_gu.shape[0]
    tok_blocks = row_tok.reshape(n_blocks, 1, MOE_ROWS)
    tok_spec = lambda index_map: pl.BlockSpec((1, 1, MOE_ROWS), index_map, memory_space=pltpu.SMEM)
    grid_spec = pltpu.PrefetchScalarGridSpec(
        num_scalar_prefetch=2,
        grid=(n_blocks,),
        in_specs=[
            tok_spec(lambda i, be, nu: (i, 0, 0)),
            tok_spec(lambda i, be, nu: (jnp.minimum(i + 1, n_blocks - 1), 0, 0)),
            pl.BlockSpec(memory_space=pl.ANY),
            pl.BlockSpec((1, 1, D_MODEL, 2 * D_FF), lambda i, be, nu: (layer, be[i], 0, 0)),
            pl.BlockSpec((1, 1, 1, 2 * D_FF), lambda i, be, nu: (layer, be[i], 0, 0)),
            pl.BlockSpec((1, 1, D_FF, D_MODEL), lambda i, be, nu: (layer, be[i], 0, 0)),
            pl.BlockSpec((1, 1, 1, D_MODEL), lambda i, be, nu: (layer, be[i], 0, 0)),
        ],
        out_specs=pl.BlockSpec((MOE_ROWS, D_MODEL), lambda i, be, nu: (i, 0)),
        scratch_shapes=[pltpu.VMEM((D_MODEL, 2 * D_FF), BF16), pltpu.VMEM((D_FF, D_MODEL), BF16),
                        pltpu.VMEM((2, MOE_ROWS, D_MODEL // 2), jnp.uint32), pltpu.VMEM((MOE_ROWS, D_MODEL), BF16),
                        pltpu.SemaphoreType.DMA((2,))],
    )
    return pl.pallas_call(
        _moe_kernel,
        grid_spec=grid_spec,
        out_shape=jax.ShapeDtypeStruct((n_rows, D_MODEL), F32),
        compiler_params=_cparams("arbitrary"),
        name="moe_experts",
    )(blk_e, n_used, tok_blocks, tok_blocks, x, w_gu, b_gu.reshape(depth, N_EXPERTS, 1, 2 * D_FF), w_dn,
      b_dn.reshape(depth, N_EXPERTS, 1, D_MODEL))


def route(top_i, pos, counts):
    n_tok = top_i.shape[0]
    n_slots = n_tok * TOP_K
    padded = ((counts + MOE_ROWS - 1) // MOE_ROWS) * MOE_ROWS
    pad_end = jnp.cumsum(padded)
    pad_start = pad_end - padded
    start = jnp.cumsum(counts) - counts
    experts = jnp.arange(N_EXPERTS, dtype=jnp.int32)
    slot_start = jnp.sum(jnp.where(top_i[:, :, None] == experts, pad_start, 0), axis=-1)
    dest = (slot_start + pos).astype(jnp.int32).T
    n_blocks = -(-n_slots // MOE_ROWS) + N_EXPERTS
    blk_first = jnp.arange(n_blocks, dtype=jnp.int32) * MOE_ROWS
    blk_e = jnp.minimum(jnp.sum((pad_end[None, :] <= blk_first[:, None]).astype(jnp.int32), axis=1), N_EXPERTS - 1)
    order = jnp.argsort(top_i.reshape(-1)).astype(jnp.int32)
    within = jnp.arange(MOE_ROWS, dtype=jnp.int32)[None, :] + (blk_first - pad_start[blk_e])[:, None]
    src = jnp.clip(start[blk_e][:, None] + within, 0, n_slots - 1).reshape(-1)
    valid = (within < counts[blk_e][:, None]).reshape(-1)
    row_tok = jnp.where(valid, order[src] // TOP_K, 0).astype(jnp.int32)
    n_used = (pad_end[-1:] // MOE_ROWS).astype(jnp.int32)
    return dest, row_tok, blk_e.astype(jnp.int32), n_used


COMBINE_ROWS = 256


def _combine_ln_kernel(dcur_ref, dnext_ref, x_ref, gate_ref, g_ref, b_ref, yb_ref, o_ref, buf_ref, sem_ref, *,
                       n_tiles):
    i = pl.program_id(0)
    tm = x_ref.shape[0]
    n_rows = TOP_K * tm
    slot = i % 2

    def row_copy(row, r, s):
        return pltpu.make_async_copy(yb_ref.at[pl.ds(row, 1), :], buf_ref.at[s, pl.ds(r, 1), :], sem_ref.at[s])

    def start_gather(d_ref, s):
        def body(r, carry):
            row_copy(d_ref[0, 0, r], r, s).start()
            return carry

        lax.fori_loop(0, n_rows, body, 0, unroll=8)

    @pl.when(i == 0)
    def _():
        start_gather(dcur_ref, 0)

    @pl.when(i + 1 < n_tiles)
    def _():
        for r in range(n_rows):
            row_copy(dnext_ref[0, 0, r], r, 1 - slot).start(priority=r % DMA_THREADS)

    pltpu.make_async_copy(yb_ref.at[pl.ds(0, n_rows), :], buf_ref.at[slot], sem_ref.at[slot]).wait()
    gates = gate_ref[...]
    moe = buf_ref[slot, 0:tm, :] * gates[:, 0:1]
    for k in range(1, TOP_K):
        moe = moe + buf_ref[slot, k * tm:(k + 1) * tm, :] * gates[:, k:k + 1]
    o_ref[...] = _layer_norm(DEEPNORM_ALPHA * x_ref[...] + moe, g_ref[...], b_ref[...])


def combine_ln(x2d, yb, dest, gates, ln_g, ln_b):
    n = x2d.shape[0]
    tm = min(COMBINE_ROWS, n)
    n_tiles = n // tm
    rows = pl.BlockSpec((tm, D_MODEL), lambda i: (i, 0))
    dest_tiles = dest.reshape(TOP_K, n_tiles, tm).transpose(1, 0, 2).reshape(n_tiles, 1, TOP_K * tm)
    dest_spec = lambda index_map: pl.BlockSpec((1, 1, TOP_K * tm), index_map, memory_space=pltpu.SMEM)
    return pl.pallas_call(
        functools.partial(_combine_ln_kernel, n_tiles=n_tiles),
        grid=(n_tiles,),
        in_specs=[dest_spec(lambda i: (i, 0, 0)), dest_spec(lambda i: (jnp.minimum(i + 1, n_tiles - 1), 0, 0)),
                  rows, pl.BlockSpec((tm, TOP_K), lambda i: (i, 0)), _full((1, D_MODEL)), _full((1, D_MODEL)),
                  pl.BlockSpec(memory_space=pl.ANY)],
        out_specs=rows,
        out_shape=jax.ShapeDtypeStruct((n, D_MODEL), F32),
        scratch_shapes=[pltpu.VMEM((2, TOP_K * tm, D_MODEL), F32), pltpu.SemaphoreType.DMA((2,))],
        compiler_params=_cparams("arbitrary"),
        name="combine_ln",
    )(dest_tiles, dest_tiles, x2d, gates, ln_g.reshape(1, D_MODEL), ln_b.reshape(1, D_MODEL), yb)


def _block_diag(w):
    h, d, _ = w.shape
    return jnp.einsum('hij,hg->higj', w.astype(F32), jnp.eye(h, dtype=F32)).reshape(h * d, h * d)


def _mixer(x, state, p, attn_bias):
    bn, seq, _ = x.shape
    n = bn * seq
    conv_buf, h0, s0, k_cache, v_cache = state
    x2d = x.reshape(n, D_MODEL)
    u = in_proj(x2d, p['w_in']).reshape(bn, seq, IN_COLS)

    y_a, new_buf, h_last = rglru(u, conv_buf, h0, p['conv_w'], p['conv_b'], p['w_r'], p['b_r'], p['w_i'], p['b_i'],
                                 p['lam'])
    q_col, k_col, v_col = OFF_Q // LANE, OFF_K // LANE, OFF_V // LANE
    if k_cache is None:
        qb = min(BAND_PAST, seq)
        y_b = attention(u, q_col, u, k_col, u, v_col, u, k_col, v_col, attn_bias, qb=qb, pb=qb,
                        prev_is_same_array=True)
    else:
        y_b = attention(u, q_col, k_cache, 0, v_cache, 0, u, k_col, v_col, attn_bias, qb=seq,
                        pb=k_cache.shape[1], prev_is_same_array=False)
    y_c, s_last = s5(u, s0, p['bmat'], p['scan_tables'], p['cmat'], p['d_c'], p['w_glu'], p['b_glu'])

    x1, x1p, top_i, gates, pos, counts = out_proj(
        y_a.reshape(n, D_A), y_b.reshape(n, D_B), y_c.reshape(n, D_C), x2d, p['g_a'], p['g_b'], p['g_c'],
        p['w_out'], p['ln1_g'], p['ln1_b'], p['w_router'], p['b_router'])

    k_rows = u[:, :, OFF_K:OFF_V]
    v_rows = u[:, :, OFF_V:OFF_C]
    if k_cache is None:
        keep = min(BAND_PAST, seq)
        k_rows = k_rows[:, seq - keep:]
        v_rows = v_rows[:, seq - keep:]
    k_rows = k_rows.reshape(bn, -1, B_HEADS, B_HEAD_DIM)
    v_rows = v_rows.reshape(bn, -1, B_HEADS, B_HEAD_DIM)
    s_re = s_last[:, 0, :C_LANES].reshape(bn, C_GROUPS, C_STATE)
    s_im = s_last[:, 0, C_LANES:].reshape(bn, C_GROUPS, C_STATE)
    routed = dict(x1=x1, x1p=x1p, top_i=top_i, gates=gates, pos=pos, counts=counts.reshape(N_EXPERTS))
    return routed, (new_buf, h_last.reshape(bn, D_A), k_rows, v_rows, s_re, s_im)


def _moe(groups, p):
    experts = jnp.arange(N_EXPERTS, dtype=jnp.int32)
    seen = jnp.zeros((N_EXPERTS,), jnp.int32)
    pos_all = []
    for g in groups:
        pos_all.append(g['pos'] + jnp.sum(jnp.where(g['top_i'][:, :, None] == experts, seen, 0), axis=-1))
        seen = seen + g['counts']
    top_i = jnp.concatenate([g['top_i'] for g in groups], axis=0)
    dest, row_tok, blk_e, n_used = route(top_i, jnp.concatenate(pos_all, axis=0), seen)
    x1p = jnp.concatenate([g['x1p'] for g in groups], axis=0)
    yb = moe_experts(x1p, row_tok, blk_e, n_used, p['layer'], p['w_gu'], p['b_gu'], p['w_dn'], p['b_dn'])
    outs, first = [], 0
    for g in groups:
        n = g['x1'].shape[0]
        outs.append(combine_ln(g['x1'], yb, dest[:, first:first + n], g['gates'], p['ln2_g'], p['ln2_b']))
        first += n
    return outs


def kernel(x_prompt, x_sample, cache_conv_a, state_h_a, cache_k_b, cache_v_b, state_s_re_c, state_s_im_c, w_in, conv_w_a, conv_b_a, w_r_a, b_r_a, w_i_a, b_i_a, lambda_a, rel_bias_b, a_re_c, a_im_c, log_dt_c, b_re_c, b_im_c, c_re_c, c_im_c, d_c, w_glu_c, b_glu_c, g_norm_a, g_norm_b, g_norm_c, w_out, ln1_g, ln1_b, w_router, b_router, w_gu, b_gu, w_dn, b_dn, ln2_g, ln2_b):
    bp, seq_p, _ = x_prompt.shape
    bs, seq_s, _ = x_sample.shape
    kv_rows = cache_k_b.shape[2]
    yp, ys = x_prompt, x_sample
    p_states = [[] for _ in range(6)]
    s_states = [[] for _ in range(6)]
    for l in range(DEPTH):
        bmat, scan_tables, cmat = s5_params(a_re_c[l], a_im_c[l], log_dt_c[l], b_re_c[l], b_im_c[l], c_re_c[l],
                                           c_im_c[l])
        p = dict(
            w_in=w_in[l].astype(BF16), conv_w=conv_w_a[l], conv_b=conv_b_a[l],
            w_r=_block_diag(w_r_a[l]).astype(BF16), b_r=b_r_a[l], w_i=_block_diag(w_i_a[l]).astype(BF16),
            b_i=b_i_a[l], lam=lambda_a[l],
            bmat=bmat, scan_tables=scan_tables, cmat=cmat, d_c=d_c[l], w_glu=w_glu_c[l].astype(BF16), b_glu=b_glu_c[l],
            g_a=g_norm_a[l], g_b=g_norm_b[l], g_c=g_norm_c[l], w_out=w_out[l].astype(BF16),
            ln1_g=ln1_g[l], ln1_b=ln1_b[l], w_router=w_router[l], b_router=b_router[l],
            layer=l, w_gu=w_gu, b_gu=b_gu, w_dn=w_dn, b_dn=b_dn,
            ln2_g=ln2_g[l], ln2_b=ln2_b[l],
        )
        zero_state = (jnp.zeros((bp, CONV_W - 1, D_A), F32), jnp.zeros((bp, D_A), F32),
                      jnp.zeros((bp, 1, 2 * C_LANES), F32), None, None)
        qb = min(BAND_PAST, seq_p)
        routed_p, st = _mixer(yp, zero_state, p, bias_table(rel_bias_b[l], qb, qb, band_mask=True))
        for lst, s in zip(p_states, st):
            lst.append(s)
        s0 = jnp.concatenate([state_s_re_c[l].reshape(bs, 1, C_LANES), state_s_im_c[l].reshape(bs, 1, C_LANES)],
                             axis=-1)
        sample_state = (cache_conv_a[l], state_h_a[l], s0,
                        cache_k_b[l].reshape(bs, kv_rows, D_B), cache_v_b[l].reshape(bs, kv_rows, D_B))
        routed_s, st = _mixer(ys, sample_state, p, bias_table(rel_bias_b[l], seq_s, kv_rows, band_mask=False))
        for lst, s in zip(s_states, st):
            lst.append(s)
        yp, ys = _moe([routed_p, routed_s], p)
        yp = yp.reshape(bp, seq_p, D_MODEL)
        ys = ys.reshape(bs, seq_s, D_MODEL)
    return (yp, ys) + tuple(jnp.stack(s) for s in p_states) + tuple(jnp.stack(s) for s in s_states)
```
